```python
import jax, jax.numpy as jnp
from jax import lax
import numpy as np

D_MODEL = 2048
BATCH = 8
SEQ = 2048
DEPTH = 1
DEC_BATCH = 32
DEC_SEQ = 4
PAST_LEN = 8192
PAGE_SIZE = 128

A_HEADS = 16
A_HEAD_DIM = 64
A_WIDTH = A_HEADS * A_HEAD_DIM
DECAY_RANK = 64
ICLR_RANK = 64
A_COLS = 4 * A_WIDTH + DECAY_RANK + ICLR_RANK
GN_EPS = 64e-5
B_HEADS = 16
B_KV_HEADS = 4
B_GROUP = B_HEADS // B_KV_HEADS
B_HEAD_DIM = 64
B_WIDTH = B_HEADS * B_HEAD_DIM
B_KV_WIDTH = B_KV_HEADS * B_HEAD_DIM
B_COLS = 2 * B_WIDTH + 6 * B_KV_WIDTH + 3 * B_HEADS
CMP_BLOCK = 32
CMP_STRIDE = 16
SEL_BLOCK = 64
N_SELECT = 16
N_LOCAL = 2
WINDOW = 512
Q_BLOCK = 128
SEL_Q_BLOCK = 16
FORCED_SCORE = 1e4
ATTN_SCALE = B_HEAD_DIM ** -0.5
IN_COLS = A_COLS + B_COLS + 2 * D_MODEL
RMS_EPS = 1e-6
NEG = -1e30

kernel_name = 'hybrid_rwkv7_nsa_step'


def rms_norm(x, g):
    xf = x.astype(jnp.float32)
    y = xf * lax.rsqrt(jnp.mean(xf * xf, axis=-1, keepdims=True) + RMS_EPS)
    return (y * g.astype(jnp.float32)).astype(x.dtype)


def masked_softmax(s, mask):
    s = jnp.where(mask, s, NEG)
    m = jnp.max(s, axis=-1, keepdims=True)
    p = jnp.where(mask, jnp.exp(s - m), 0.0)
    return p / jnp.maximum(jnp.sum(p, axis=-1, keepdims=True), 1e-30)


def alibi_slopes():
    h = jnp.arange(1, B_HEADS + 1, dtype=jnp.float32)
    return jnp.exp2(-8.0 * h / B_HEADS).reshape(B_KV_HEADS, B_GROUP)


def rwkv7_mixer(u, u_prev0, S0, mu_shift, w0, w_up, a0, a_up, k_k, k_a, r_k, gn_w, gn_b):
    f32 = jnp.float32
    B, T, _ = u.shape
    H, N = A_HEADS, A_HEAD_DIM
    u_prev = jnp.concatenate([u_prev0[:, None].astype(u.dtype), u[:, :-1]], axis=1)
    us = u + mu_shift * (u_prev - u)
    r, k, v, g, wd, ad = jnp.split(us, [A_WIDTH, 2 * A_WIDTH, 3 * A_WIDTH, 4 * A_WIDTH,
                                        4 * A_WIDTH + DECAY_RANK], axis=-1)
    w_log = -jax.nn.softplus(-(w0 + jnp.tanh(wd) @ w_up).astype(f32)) - 0.5
    decay = jnp.exp(-jnp.exp(w_log))
    a = jax.nn.sigmoid((a0 + ad @ a_up).astype(f32))
    r, k, v = r.astype(f32), k.astype(f32), v.astype(f32)
    kk = (k * k_k).reshape(B, T, H, N)
    kk = kk * lax.rsqrt(jnp.maximum(jnp.sum(kk * kk, axis=-1, keepdims=True), 1e-24))
    k = k * (1.0 + (a - 1.0) * k_a)
    r, k, v, a, decay = (t.reshape(B, T, H, N) for t in (r, k, v, a, decay))

    def step(S, inp):
        r_t, k_t, v_t, kk_t, a_t, w_t = inp
        sa = jnp.einsum('bhij,bhj->bhi', S, kk_t)
        S = (S * w_t[:, :, None, :] - sa[..., None] * (kk_t * a_t)[:, :, None, :]
             + v_t[..., None] * k_t[:, :, None, :])
        return S, jnp.einsum('bhij,bhj->bhi', S, r_t)

    xs = tuple(jnp.moveaxis(t, 1, 0) for t in (r, k, v, kk, a, decay))
    S, o = lax.scan(step, S0.astype(f32), xs)
    o = jnp.moveaxis(o, 0, 1)
    mean = jnp.mean(o, axis=-1, keepdims=True)
    var = jnp.mean(jnp.square(o - mean), axis=-1, keepdims=True)
    o = (o - mean) * lax.rsqrt(var + GN_EPS) * gn_w.reshape(H, N) + gn_b.reshape(H, N)
    o = o + jnp.sum(r * k * r_k, axis=-1, keepdims=True) * v
    o = o.reshape(B, T, A_WIDTH) * jax.nn.silu(g.astype(f32))
    return o.astype(u.dtype), S.astype(u.dtype), u[:, -1]


def compress_blocks(rows, pos_w, w_c):
    B, L, G, D = rows.shape
    ratio = CMP_BLOCK // CMP_STRIDE
    n_ch = L // CMP_STRIDE
    n_cmp = n_ch - ratio + 1
    chunks = rows[:, :n_ch * CMP_STRIDE].reshape(B, n_ch, CMP_STRIDE, G, D)
    pw = pos_w.reshape(ratio, CMP_STRIDE, D)
    pooled = jnp.einsum('bnsgd,sd->bngd', chunks[:, :n_cmp], pw[0])
    for i in range(1, ratio):
        pooled = pooled + jnp.einsum('bnsgd,sd->bngd', chunks[:, i:i + n_cmp], pw[i])
    return jnp.einsum('bngd,de->bnge', pooled, w_c)


def compressed_attention(q, kv, t_pos, slopes, cmp_pos_k, cmp_pos_v, w_cmp_k, w_cmp_v):
    f32 = jnp.float32
    kc = compress_blocks(kv[:, :, 0], cmp_pos_k, w_cmp_k)
    vc = compress_blocks(kv[:, :, 1], cmp_pos_v, w_cmp_v)
    n_cmp = kc.shape[1]
    blk_end = jnp.arange(n_cmp) * CMP_STRIDE + (CMP_BLOCK - 1)
    dist = t_pos[:, None] - blk_end[None, :]
    s = jnp.einsum('btgrd,bngd->btgrn', q, kc, preferred_element_type=f32) * ATTN_SCALE
    s = s - slopes[None, None, :, :, None] * dist[None, :, None, None, :]
    p = masked_softmax(s, (dist >= 0)[None, :, None, None, :])
    o = jnp.einsum('btgrn,bngd->btgrd', p, vc.astype(f32))
    return o, p


def select_blocks(p_cmp, t_pos, L):
    f32 = jnp.float32
    n_cmp = p_cmp.shape[-1]
    n_sel = -(-L // SEL_BLOCK)
    cs = jnp.arange(n_cmp) * CMP_STRIDE
    ss = jnp.arange(n_sel) * SEL_BLOCK
    overlap = jnp.clip(jnp.minimum(cs[:, None] + CMP_BLOCK, ss[None, :] + SEL_BLOCK)
                       - jnp.maximum(cs[:, None], ss[None, :]), 0, None).astype(f32) / CMP_BLOCK
    imp = jnp.einsum('btgrn,nj->btgj', p_cmp, overlap)
    j = jnp.arange(n_sel)
    back = (t_pos // SEL_BLOCK)[:, None] - j[None, :]
    forced = (j[None, :] == 0) | ((back >= 0) & (back < N_LOCAL))
    score = jnp.where(forced[None, :, None], FORCED_SCORE,
                      jnp.where((back >= 0)[None, :, None], imp, -1.0))
    top_s, top_i = lax.top_k(score, min(N_SELECT, n_sel))
    return top_i, top_s >= 0


def selected_attention(q, kv, top_i, top_ok, t_pos, slopes):
    f32 = jnp.float32
    B, T, G, R, D = q.shape
    L = kv.shape[1]
    n_sel = -(-L // SEL_BLOCK)
    kv = jnp.pad(kv, ((0, 0), (0, n_sel * SEL_BLOCK - L), (0, 0), (0, 0), (0, 0)))
    blocks = kv.reshape(B, n_sel, SEL_BLOCK, 2, G, D).transpose(0, 4, 1, 2, 3, 5)
    qb = SEL_Q_BLOCK if T % SEL_Q_BLOCK == 0 else T
    nqb = T // qb
    b_idx = jnp.arange(B)[:, None, None, None]
    g_idx = jnp.arange(G)[None, None, :, None]
    offs = jnp.arange(SEL_BLOCK)

    def to_blocks(a):
        return jnp.moveaxis(a.reshape((B, nqb, qb) + a.shape[2:]), 1, 0)

    def one_block(args):
        q_b, i_b, ok_b, pos_b = args
        nk = i_b.shape[-1] * SEL_BLOCK
        kvg = blocks[b_idx, g_idx, i_b]
        k_pos = i_b[..., None] * SEL_BLOCK + offs
        dist = (pos_b[None, :, None, None, None] - k_pos).reshape(B, qb, G, 1, nk)
        mask = (dist >= 0) & ok_b[..., None].repeat(SEL_BLOCK, axis=-1).reshape(B, qb, G, 1, nk)
        kg = kvg[..., 0, :].reshape(B, qb, G, nk, D)
        vg = kvg[..., 1, :].reshape(B, qb, G, nk, D)
        s = jnp.einsum('bqgrd,bqgnd->bqgrn', q_b, kg, preferred_element_type=f32) * ATTN_SCALE
        s = s - slopes[None, None, :, :, None] * dist
        p = masked_softmax(s, mask)
        return jnp.einsum('bqgrn,bqgnd->bqgrd', p, vg.astype(f32))

    o = lax.map(one_block, (to_blocks(q), to_blocks(top_i), to_blocks(top_ok), t_pos.reshape(nqb, qb)))
    return jnp.moveaxis(o, 0, 1).reshape(B, T, G, R, D)


def window_attention(q, kv, q_pos0, slopes):
    f32 = jnp.float32
    B, T, G, R, D = q.shape
    Lw = kv.shape[1]
    qb = Q_BLOCK if T % Q_BLOCK == 0 else T
    nqb = T // qb
    band = WINDOW + qb
    kvp = jnp.pad(kv, ((0, 0), (WINDOW, 0), (0, 0), (0, 0), (0, 0)))
    k_pos0 = q_pos0 + T - Lw
    q_blocks = jnp.moveaxis(q.reshape(B, nqb, qb, G, R, D), 1, 0)

    def one_block(args):
        i, q_b = args
        start = Lw - T + i * qb
        kv_b = lax.dynamic_slice_in_dim(kvp, start, band, axis=1)
        k_pos = k_pos0 - WINDOW + start + jnp.arange(band)
        q_pos = q_pos0 + i * qb + jnp.arange(qb)
        dist = q_pos[:, None] - k_pos[None, :]
        mask = (dist >= 0) & (dist <= WINDOW) & (k_pos >= k_pos0)[None, :]
        s = jnp.einsum('bqgrd,bkgd->bqgrk', q_b, kv_b[:, :, 0], preferred_element_type=f32) * ATTN_SCALE
        s = s - slopes[None, None, :, :, None] * dist[None, :, None, None, :]
        p = masked_softmax(s, mask[None, :, None, None, :])
        return jnp.einsum('bqgrk,bkgd->bqgrd', p, kv_b[:, :, 1].astype(f32))

    o = lax.map(one_block, (jnp.arange(nqb), q_blocks))
    return jnp.moveaxis(o, 0, 1).reshape(B, T, G, R, D)


def nsa_mixer(q, gates, cmp_kv, sel_kv, win_kv, q_pos0, cmp_pos_k, cmp_pos_v, w_cmp_k, w_cmp_v):
    f32 = jnp.float32
    B, T, _ = q.shape
    q = q.reshape(B, T, B_KV_HEADS, B_GROUP, B_HEAD_DIM)
    t_pos = q_pos0 + jnp.arange(T)
    slopes = alibi_slopes()
    o_cmp, p_cmp = compressed_attention(q, cmp_kv, t_pos, slopes, cmp_pos_k, cmp_pos_v, w_cmp_k, w_cmp_v)
    top_i, top_ok = select_blocks(p_cmp, t_pos, sel_kv.shape[1])
    o_sel = selected_attention(q, sel_kv, top_i, top_ok, t_pos, slopes)
    o_win = window_attention(q, win_kv, q_pos0, slopes)
    g = jax.nn.sigmoid(gates.astype(f32)).reshape(B, T, 3, B_KV_HEADS, B_GROUP, 1)
    o = g[:, :, 0] * o_cmp + g[:, :, 1] * o_sel + g[:, :, 2] * o_win
    return o.reshape(B, T, B_WIDTH)


def hybrid_layer(x, u_prev0, S0, cmp_past, sel_past, win_buf, q_pos0, win_keep,
                 norm_in, w_in, mu_shift, w0, w_up, a0, a_up, k_k, k_a, r_k, gn_w, gn_b,
                 cmp_pos_k, cmp_pos_v, w_cmp_k, w_cmp_v, w_pa, w_pb, w_o):
    f32 = jnp.float32
    B, T, _ = x.shape
    xn = rms_norm(x, norm_in)
    u = jnp.einsum('btd,dc->btc', xn, w_in)
    u_a, u_b, u_m = jnp.split(u, [A_COLS, A_COLS + B_COLS], axis=-1)
    o_a, S_new, shift_new = rwkv7_mixer(u_a, u_prev0, S0, mu_shift, w0, w_up, a0, a_up,
                                        k_k, k_a, r_k, gn_w, gn_b)
    q, g_b, kv_c, kv_s, kv_w, nsa_gates = jnp.split(
        u_b, [B_WIDTH, 2 * B_WIDTH, 2 * B_WIDTH + 2 * B_KV_WIDTH,
              2 * B_WIDTH + 4 * B_KV_WIDTH, 2 * B_WIDTH + 6 * B_KV_WIDTH], axis=-1)
    new_c = kv_c.reshape(B, T, 2, B_KV_HEADS, B_HEAD_DIM)
    new_s = kv_s.reshape(B, T, 2, B_KV_HEADS, B_HEAD_DIM)
    new_w = kv_w.reshape(B, T, 2, B_KV_HEADS, B_HEAD_DIM)
    cmp_all = jnp.concatenate([cmp_past.astype(x.dtype), new_c], axis=1)
    sel_all = jnp.concatenate([sel_past.astype(x.dtype), new_s], axis=1)
    win_all = jnp.concatenate([win_buf.astype(x.dtype), new_w], axis=1)
    o_b = nsa_mixer(q, nsa_gates, cmp_all, sel_all, win_all, q_pos0, cmp_pos_k, cmp_pos_v, w_cmp_k, w_cmp_v)
    o_b = (o_b * jax.nn.silu(g_b.astype(f32))).astype(x.dtype)
    gate_a, gate_b = jnp.split(jax.nn.sigmoid(u_m.astype(f32)), 2, axis=-1)
    merged = gate_a * (o_a @ w_pa) + gate_b * (o_b @ w_pb)
    h = x + merged.astype(x.dtype) @ w_o
    return h, new_c, new_s, win_all[:, win_all.shape[1] - win_keep:], S_new, shift_new


def setup_inputs(seed: int = 0) -> dict:
    key = jax.random.key(seed)
    ks = jax.random.split(key, 32)
    f32 = jnp.float32

    def nrm(k, shape, scale):
        return scale * jax.random.normal(k, shape, f32)

    n_pages = PAST_LEN // PAGE_SIZE
    n_used = DEC_BATCH * n_pages
    n_pool = n_used + n_used // 4
    win_keep = min(WINDOW, PAST_LEN)
    L = DEPTH
    kvs = (2, B_KV_HEADS, B_HEAD_DIM)
    return {
        'x_prompt': nrm(ks[0], (BATCH, SEQ, D_MODEL), 1.0),
        'x_sample': nrm(ks[1], (DEC_BATCH, DEC_SEQ, D_MODEL), 1.0),
        'cache_cmp_kv': nrm(ks[2], (L, n_pool, PAGE_SIZE) + kvs, 1.0),
        'cache_sel_kv': nrm(ks[3], (L, n_pool, PAGE_SIZE) + kvs, 1.0),
        'cache_win_kv': nrm(ks[4], (L, DEC_BATCH, win_keep) + kvs, 1.0),
        'state_rwkv': nrm(ks[5], (L, DEC_BATCH, A_HEADS, A_HEAD_DIM, A_HEAD_DIM), 0.5),
        'state_shift': nrm(ks[6], (L, DEC_BATCH, A_COLS), 1.0),
        'page_table': jax.random.permutation(ks[7], n_pool)[:n_used].reshape(DEC_BATCH, n_pages).astype(jnp.int32),
        'norm_in': 1.0 + nrm(ks[8], (L, D_MODEL), 0.05),
        'w_in': nrm(ks[9], (L, D_MODEL, IN_COLS), D_MODEL ** -0.5),
        'mu_shift': jax.random.uniform(ks[10], (L, A_COLS), f32, 0.1, 0.9),
        'w0': jax.random.uniform(ks[11], (L, A_WIDTH), f32, -6.0, -1.0),
        'w_up': nrm(ks[12], (L, DECAY_RANK, A_WIDTH), 0.1 * DECAY_RANK ** -0.5),
        'a0': nrm(ks[13], (L, A_WIDTH), 0.5),
        'a_up': nrm(ks[14], (L, ICLR_RANK, A_WIDTH), ICLR_RANK ** -0.5),
        'k_k': 0.85 + nrm(ks[15], (L, A_WIDTH), 0.05),
        'k_a': 1.0 + nrm(ks[16], (L, A_WIDTH), 0.05),
        'r_k': nrm(ks[17], (L, A_HEADS, A_HEAD_DIM), 0.1),
        'gn_w': 1.0 + nrm(ks[18], (L, A_WIDTH), 0.05),
        'gn_b': nrm(ks[19], (L, A_WIDTH), 0.02),
        'cmp_pos_k': (1.0 + nrm(ks[20], (L, CMP_BLOCK, B_HEAD_DIM), 0.1)) * CMP_BLOCK ** -0.5,
        'cmp_pos_v': (1.0 + nrm(ks[21], (L, CMP_BLOCK, B_HEAD_DIM), 0.1)) * CMP_BLOCK ** -0.5,
        'w_cmp_k': nrm(ks[22], (L, B_HEAD_DIM, B_HEAD_DIM), B_HEAD_DIM ** -0.5),
        'w_cmp_v': nrm(ks[23], (L, B_HEAD_DIM, B_HEAD_DIM), B_HEAD_DIM ** -0.5),
        'w_pa': nrm(ks[24], (L, A_WIDTH, D_MODEL), A_WIDTH ** -0.5),
        'w_pb': nrm(ks[25], (L, B_WIDTH, D_MODEL), B_WIDTH ** -0.5),
        'w_o': nrm(ks[26], (L, D_MODEL, D_MODEL), D_MODEL ** -0.5),
        'norm_out': 1.0 + nrm(ks[27], (D_MODEL,), 0.05),
    }


def reference(x_prompt, x_sample, cache_cmp_kv, cache_sel_kv, cache_win_kv, state_rwkv, state_shift,
              page_table, norm_in, w_in, mu_shift, w0, w_up, a0, a_up, k_k, k_a, r_k, gn_w, gn_b,
              cmp_pos_k, cmp_pos_v, w_cmp_k, w_cmp_v, w_pa, w_pb, w_o, norm_out):
    f32 = jnp.float32
    B, T, _ = x_prompt.shape
    DB = x_sample.shape[0]
    n_pages = page_table.shape[1]
    past_len = n_pages * PAGE_SIZE
    dt = x_prompt.dtype
    empty = jnp.zeros((B, 0, 2, B_KV_HEADS, B_HEAD_DIM), dt)
    h_p, h_s = x_prompt, x_sample
    outs_p = [[] for _ in range(5)]
    outs_s = [[] for _ in range(5)]
    for l in range(DEPTH):
        lw = (norm_in[l], w_in[l], mu_shift[l], w0[l], w_up[l], a0[l], a_up[l], k_k[l], k_a[l], r_k[l],
              gn_w[l], gn_b[l], cmp_pos_k[l], cmp_pos_v[l], w_cmp_k[l], w_cmp_v[l], w_pa[l], w_pb[l], w_o[l])
        res_p = hybrid_layer(h_p, jnp.zeros((B, A_COLS), dt),
                             jnp.zeros((B, A_HEADS, A_HEAD_DIM, A_HEAD_DIM), f32),
                             empty, empty, empty, 0, min(WINDOW, T), *lw)
        h_p = res_p[0]
        for acc, val in zip(outs_p, res_p[1:]):
            acc.append(val)
        past_c = cache_cmp_kv[l][page_table].reshape(DB, past_len, 2, B_KV_HEADS, B_HEAD_DIM)
        past_s = cache_sel_kv[l][page_table].reshape(DB, past_len, 2, B_KV_HEADS, B_HEAD_DIM)
        res_s = hybrid_layer(h_s, state_shift[l], state_rwkv[l], past_c, past_s, cache_win_kv[l],
                             past_len, cache_win_kv.shape[2], *lw)
        h_s = res_s[0]
        for acc, val in zip(outs_s, res_s[1:]):
            acc.append(val)
    y_prompt = rms_norm(h_p, norm_out)
    y_sample = rms_norm(h_s, norm_out)
    return (y_prompt, y_sample,
            jnp.stack(outs_p[0]), jnp.stack(outs_p[1]), jnp.stack(outs_p[2]), jnp.stack(outs_p[3]), jnp.stack(outs_p[4]),
            jnp.stack(outs_s[0]), jnp.stack(outs_s[1]), jnp.stack(outs_s[2]), jnp.stack(outs_s[3]), jnp.stack(outs_s[4]))
```

```python
import functools
import math

import jax
import jax.numpy as jnp
from jax import lax
from jax.experimental import pallas as pl
from jax.experimental.pallas import tpu as pltpu

f32 = jnp.float32
bf16 = jnp.bfloat16

D_MODEL = 2048
PAGE_SIZE = 128
A_HEADS = 16
A_HEAD_DIM = 64
A_WIDTH = A_HEADS * A_HEAD_DIM
DECAY_RANK = 64
ICLR_RANK = 64
A_COLS = 4 * A_WIDTH + DECAY_RANK + ICLR_RANK
GN_EPS = 64e-5
B_HEADS = 16
B_KV_HEADS = 4
B_GROUP = B_HEADS // B_KV_HEADS
B_HEAD_DIM = 64
B_WIDTH = B_HEADS * B_HEAD_DIM
B_KV_WIDTH = B_KV_HEADS * B_HEAD_DIM
B_COLS = 2 * B_WIDTH + 6 * B_KV_WIDTH + 3 * B_HEADS
CMP_BLOCK = 32
CMP_STRIDE = 16
SEL_BLOCK = 64
N_SELECT = 16
N_LOCAL = 2
WINDOW = 512
FORCED_SCORE = 1e4
ATTN_SCALE = B_HEAD_DIM ** -0.5
RMS_EPS = 1e-6
NEG = -1e30

C_RKVG = 0
C_Q = 4096
C_GB = 5120
C_GA_M = 6144
C_GB_M = 8192
C_KVC = 10240
C_KVS = 10752
C_KVW = 11264
C_LR = 11776
C_NG = 11904
U_COLS = 12288

VMEM_LIMIT = 56 * 1024 * 1024
HI = lax.Precision.HIGHEST


def _cparams(sem):
    return pltpu.CompilerParams(dimension_semantics=sem, vmem_limit_bytes=VMEM_LIMIT)


def _pack_w_in(w):
    a, b, m = w[:, :A_COLS], w[:, A_COLS:A_COLS + B_COLS], w[:, A_COLS + B_COLS:]
    z = jnp.zeros((w.shape[0], U_COLS - C_NG - 3 * B_HEADS), w.dtype)
    return jnp.concatenate(
        [a[:, :4 * A_WIDTH], b[:, :2 * B_WIDTH], m, b[:, 2 * B_WIDTH:2 * B_WIDTH + 6 * B_KV_WIDTH],
         a[:, 4 * A_WIDTH:], b[:, 2 * B_WIDTH + 6 * B_KV_WIDTH:], z], axis=1).astype(bf16)


def _proj_in_kernel(x_ref, g_ref, w_ref, o_ref, xn_ref):
    @pl.when(pl.program_id(1) == 0)
    def _():
        x = x_ref[...]
        ms = jnp.mean(x * x, axis=-1, keepdims=True)
        xn_ref[...] = (x * lax.rsqrt(ms + RMS_EPS) * g_ref[...]).astype(bf16)

    o_ref[...] = jnp.dot(xn_ref[...], w_ref[...], preferred_element_type=f32)


def _proj_in(x2d, norm_g, w_packed):
    m = x2d.shape[0]
    tm = min(1024, m)
    tn = 1024
    return pl.pallas_call(
        _proj_in_kernel,
        out_shape=jax.ShapeDtypeStruct((m, U_COLS), f32),
        grid=(m // tm, U_COLS // tn),
        in_specs=[pl.BlockSpec((tm, D_MODEL), lambda i, j: (i, 0)),
                  pl.BlockSpec((1, D_MODEL), lambda i, j: (0, 0)),
                  pl.BlockSpec((D_MODEL, tn), lambda i, j: (0, j))],
        out_specs=pl.BlockSpec((tm, tn), lambda i, j: (i, j)),
        scratch_shapes=[pltpu.VMEM((tm, D_MODEL), bf16)],
        compiler_params=_cparams(("parallel", "arbitrary")),
        name="proj_in",
    )(x2d, norm_g.reshape(1, D_MODEL), w_packed)


def _mm(a, b):
    return jnp.dot(a.astype(bf16), b.astype(bf16), preferred_element_type=f32)


def _mm_nt(a, b):
    return lax.dot_general(a.astype(bf16), b.astype(bf16), (((1,), (1,)), ((), ())), preferred_element_type=f32)


def _mm_tn(a, b):
    return lax.dot_general(a.astype(bf16), b.astype(bf16), (((0,), (0,)), ((), ())), preferred_element_type=f32)


def _rwkv_kernel(um_ref, ulr_ref, pm_ref, plr_ref, s0_ref, mum_ref, mulr_ref, w0_ref, wup_ref, a0_ref,
                 aup_ref, kk_ref, ka_ref, rk_ref, gnw_ref, gnb_ref, o_ref, sout_ref,
                 s_ref, prevm_ref, prevlr_ref, *, chunk, t_valid):
    C = chunk
    H, N = A_HEADS, A_HEAD_DIM
    c = pl.program_id(1)

    @pl.when(c == 0)
    def _():
        s_ref[...] = s0_ref[...]
        prevm_ref[...] = pm_ref[...]
        prevlr_ref[...] = plr_ref[...]

    um = um_ref[...]
    ulr = ulr_ref[...]
    row = lax.broadcasted_iota(jnp.int32, (C, 1), 0)

    def shifted(u, prev):
        return jnp.where(row == 0, prev, pltpu.roll(u, 1, 0))

    usm = um + mum_ref[...] * (shifted(um, prevm_ref[...]) - um)
    uslr = ulr + mulr_ref[...] * (shifted(ulr, prevlr_ref[...]) - ulr)
    prevm_ref[...] = um[C - 1:C, :]
    prevlr_ref[...] = ulr[C - 1:C, :]

    wd = uslr[:, :DECAY_RANK]
    ad = uslr[:, DECAY_RANK:]
    zw = w0_ref[...] + jnp.dot(jnp.tanh(wd), wup_ref[...], preferred_element_type=f32, precision=HI)
    softplus_neg = jnp.maximum(-zw, 0.0) + jnp.log(1.0 + jnp.exp(-jnp.abs(zw)))
    logw = -jnp.exp(-softplus_neg - 0.5)
    a_all = jax.nn.sigmoid(a0_ref[...] + jnp.dot(ad, aup_ref[...], preferred_element_type=f32, precision=HI))
    if t_valid < C:
        valid = (row < t_valid).astype(f32)
        logw = logw * valid
    ti = lax.broadcasted_iota(jnp.int32, (C, C), 0)
    si = lax.broadcasted_iota(jnp.int32, (C, C), 1)
    tril_incl = ti >= si
    tril_strict = ti > si
    cum = jnp.dot(tril_incl.astype(f32), logw, preferred_element_type=f32, precision=HI)
    e_pos = jnp.exp(cum)
    e_neg = jnp.exp(-cum)
    e_prev = jnp.exp(cum - logw)
    e_last = e_pos[C - 1:C, :]
    eye = (ti == si).astype(f32)
    n_sq = int(math.log2(C)) - 1

    for h in range(H):
        sl = slice(h * N, (h + 1) * N)
        r_h = usm[:, sl]
        k_h = usm[:, A_WIDTH + h * N:A_WIDTH + (h + 1) * N]
        v_h = usm[:, 2 * A_WIDTH + h * N:2 * A_WIDTH + (h + 1) * N]
        g_h = usm[:, 3 * A_WIDTH + h * N:3 * A_WIDTH + (h + 1) * N]
        a_h = a_all[:, sl]
        kk_h = k_h * kk_ref[:, sl]
        kk_h = kk_h * lax.rsqrt(jnp.maximum(jnp.sum(kk_h * kk_h, axis=-1, keepdims=True), 1e-24))
        k2_h = k_h * (1.0 + (a_h - 1.0) * ka_ref[:, sl])
        b_h = kk_h * a_h
        kt = kk_h * e_prev[:, sl]
        rt = r_h * e_pos[:, sl]
        kn = k2_h * e_neg[:, sl]
        bn = b_h * e_neg[:, sl]
        if t_valid < C:
            kn = kn * valid
            bn = bn * valid
        s0 = s_ref[h]
        lhs = jnp.concatenate([kt, rt], axis=0)
        qk_b = _mm_nt(lhs, bn)
        qk_k = _mm_nt(lhs, kn)
        a_b = jnp.where(tril_strict, qk_b[:C], 0.0)
        a_k = jnp.where(tril_strict, qk_k[:C], 0.0)
        p_b = jnp.where(tril_incl, qk_b[C:], 0.0)
        p_k = jnp.where(tril_incl, qk_k[C:], 0.0)
        x = -a_b
        tinv = eye + x
        pw = x
        for _ in range(n_sq):
            pw = _mm(pw, pw)
            tinv = tinv + _mm(tinv, pw)
        w0s = _mm_nt(lhs, s0)
        u = _mm(tinv, w0s[:C] + _mm(a_k, v_h))
        o = w0s[C:] + _mm(p_k, v_h) - _mm(p_b, u)
        s_new = (s0 + _mm_tn(v_h, kn) - _mm_tn(u, bn)) * e_last[:, sl]
        s_ref[h] = s_new

        mean = jnp.mean(o, axis=-1, keepdims=True)
        d = o - mean
        var = jnp.mean(d * d, axis=-1, keepdims=True)
        on = d * lax.rsqrt(var + GN_EPS) * gnw_ref[:, sl] + gnb_ref[:, sl]
        on = on + jnp.sum(r_h * k2_h * rk_ref[:, sl], axis=-1, keepdims=True) * v_h
        o_ref[:, sl] = on * (g_h * jax.nn.sigmoid(g_h))

    @pl.when(c == pl.num_programs(1) - 1)
    def _():
        sout_ref[...] = s_ref[...]


def _rwkv(u3, prev_m, prev_lr, s0, p, *, chunk, t_valid):
    B, T, _ = u3.shape
    C = chunk
    H, N = A_HEADS, A_HEAD_DIM
    row = lambda v: v.reshape(1, -1)
    vec = lambda n: pl.BlockSpec((1, n), lambda b, c: (0, 0))
    in_specs = [
        pl.BlockSpec((None, C, 4 * A_WIDTH), lambda b, c: (b, c, 0)),
        pl.BlockSpec((None, C, 128), lambda b, c: (b, c, C_LR // 128)),
        pl.BlockSpec((None, 1, 4 * A_WIDTH), lambda b, c: (b, 0, 0)),
        pl.BlockSpec((None, 1, 128), lambda b, c: (b, 0, 0)),
        pl.BlockSpec((None, H, N, N), lambda b, c: (b, 0, 0, 0)),
        vec(4 * A_WIDTH), vec(128), vec(A_WIDTH),
        pl.BlockSpec((DECAY_RANK, A_WIDTH), lambda b, c: (0, 0)),
        vec(A_WIDTH),
        pl.BlockSpec((ICLR_RANK, A_WIDTH), lambda b, c: (0, 0)),
        vec(A_WIDTH), vec(A_WIDTH), vec(A_WIDTH), vec(A_WIDTH), vec(A_WIDTH),
    ]
    return pl.pallas_call(
        functools.partial(_rwkv_kernel, chunk=C, t_valid=t_valid),
        out_shape=(jax.ShapeDtypeStruct((B, T, A_WIDTH), f32), jax.ShapeDtypeStruct((B, H, N, N), f32)),
        grid=(B, T // C),
        in_specs=in_specs,
        out_specs=(pl.BlockSpec((None, C, A_WIDTH), lambda b, c: (b, c, 0)),
                   pl.BlockSpec((None, H, N, N), lambda b, c: (b, 0, 0, 0))),
        scratch_shapes=[pltpu.VMEM((H, N, N), f32), pltpu.VMEM((1, 4 * A_WIDTH), f32), pltpu.VMEM((1, 128), f32)],
        compiler_params=_cparams(("parallel", "arbitrary")),
        name="rwkv7",
    )(u3, u3, prev_m, prev_lr, s0, row(p["mu_shift"][:4 * A_WIDTH]), row(p["mu_shift"][4 * A_WIDTH:]),
      row(p["w0"]), p["w_up"], row(p["a0"]), p["a_up"], row(p["k_k"]), row(p["k_a"]), row(p["r_k"]),
      row(p["gn_w"]), row(p["gn_b"]))


def _merge_kernel(x_ref, oa_ref, ob_ref, ga_ref, gb_ref, wpa_ref, wpb_ref, wo_ref, gout_ref, y_ref):
    pa = jnp.dot(oa_ref[...].astype(bf16), wpa_ref[...], preferred_element_type=f32)
    pb = jnp.dot(ob_ref[...].astype(bf16), wpb_ref[...], preferred_element_type=f32)
    merged = jax.nn.sigmoid(ga_ref[...]) * pa + jax.nn.sigmoid(gb_ref[...]) * pb
    h = x_ref[...] + jnp.dot(merged.astype(bf16), wo_ref[...], preferred_element_type=f32)
    ms = jnp.mean(h * h, axis=-1, keepdims=True)
    y_ref[...] = h * lax.rsqrt(ms + RMS_EPS) * gout_ref[...]


def _merge(x2d, o_a, o_b, u2d, w_pa, w_pb, w_o, norm_out):
    m = x2d.shape[0]
    tm = min(256, m)
    const = lambda shape: pl.BlockSpec(shape, lambda i: (0, 0), pipeline_mode=pl.Buffered(1))
    return pl.pallas_call(
        _merge_kernel,
        out_shape=jax.ShapeDtypeStruct((m, D_MODEL), f32),
        grid=(m // tm,),
        in_specs=[pl.BlockSpec((tm, D_MODEL), lambda i: (i, 0)),
                  pl.BlockSpec((tm, A_WIDTH), lambda i: (i, 0)),
                  pl.BlockSpec((tm, B_WIDTH), lambda i: (i, 0)),
                  pl.BlockSpec((tm, D_MODEL), lambda i: (i, C_GA_M // D_MODEL)),
                  pl.BlockSpec((tm, D_MODEL), lambda i: (i, C_GB_M // D_MODEL)),
                  const((A_WIDTH, D_MODEL)), const((B_WIDTH, D_MODEL)), const((D_MODEL, D_MODEL)),
                  const((1, D_MODEL))],
        out_specs=pl.BlockSpec((tm, D_MODEL), lambda i: (i, 0)),
        compiler_params=_cparams(("parallel",)),
        name="merge_out",
    )(x2d, o_a, o_b, u2d, u2d, w_pa, w_pb, w_o, norm_out.reshape(1, D_MODEL))


def _alibi_slope(head):
    return 2.0 ** (-8.0 * (head + 1) / B_HEADS)


def _slope_row(g, tq):
    lane_head = lax.broadcasted_iota(jnp.int32, (1, B_GROUP * tq), 1) // tq
    out = jnp.zeros((1, B_GROUP * tq), f32)
    for r in range(B_GROUP):
        out = jnp.where(lane_head == r, _alibi_slope(g * B_GROUP + r), out)
    return out


def _tile_heads(row):
    return jnp.concatenate([row] * B_GROUP, axis=1)


def _overlap_t(n_sel_rows, n_ch):
    j = lax.broadcasted_iota(jnp.int32, (n_sel_rows, n_ch), 0) * SEL_BLOCK
    n = lax.broadcasted_iota(jnp.int32, (n_sel_rows, n_ch), 1) * CMP_STRIDE
    ov = jnp.minimum(n + CMP_BLOCK, j + SEL_BLOCK) - jnp.maximum(n, j)
    return jnp.maximum(ov, 0).astype(f32) * (1.0 / CMP_BLOCK)


def _attn_step(s, dist, slope_row, extra_masks, carry, v_t, v_rows=False):
    m, l, acc = carry
    s = s - slope_row * dist
    s = jnp.where(dist >= 0.0, s, NEG)
    for keep in extra_masks:
        s = jnp.where(keep, s, NEG)
    m_new = jnp.maximum(m, jnp.max(s, axis=0, keepdims=True))
    alpha = jnp.exp(m - m_new)
    p = jnp.where(s > 0.5 * NEG, jnp.exp(s - m_new), 0.0)
    l = l * alpha + jnp.sum(p, axis=0, keepdims=True)
    if v_rows:
        pv = _mm_tn(v_t, p)
    else:
        pv = jnp.dot(v_t, p.astype(bf16), preferred_element_type=f32)
    return m_new, l, acc * alpha + pv


def _attn_init(lanes):
    return (jnp.full((1, lanes), NEG, f32), jnp.zeros((1, lanes), f32), jnp.zeros((B_HEAD_DIM, lanes), f32))


def _attn_finish(carry):
    _, l, acc = carry
    return acc * (1.0 / jnp.maximum(l, 1e-30))


def _compressed_branch(kc, vc_t, q_t, t_row, slope_row, n_ch):
    s = jnp.dot(kc, q_t, preferred_element_type=f32)
    blk_end = (lax.broadcasted_iota(jnp.int32, (n_ch, 1), 0) * CMP_STRIDE + (CMP_BLOCK - 1)).astype(f32)
    dist = t_row - blk_end
    s = s - slope_row * dist
    mask = dist >= 0.0
    s = jnp.where(mask, s, NEG)
    m = jnp.max(s, axis=0, keepdims=True)
    p = jnp.where(mask, jnp.exp(s - m), 0.0)
    p = p * (1.0 / jnp.maximum(jnp.sum(p, axis=0, keepdims=True), 1e-30))
    return jnp.dot(vc_t, p.astype(bf16), preferred_element_type=f32), p


def _selection_scores(p, tq, tok_row_i, n_rows, n_sel, n_ch):
    psum = p[:, 0:tq]
    for r in range(1, B_GROUP):
        psum = psum + p[:, r * tq:(r + 1) * tq]
    imp = jnp.dot(_overlap_t(n_rows, n_ch), psum, preferred_element_type=f32, precision=HI)
    j = lax.broadcasted_iota(jnp.int32, (n_rows, 1), 0)
    back = tok_row_i // SEL_BLOCK - j
    forced = (j == 0) | ((back >= 0) & (back < N_LOCAL))
    score = jnp.where(forced, FORCED_SCORE, jnp.where(back >= 0, imp, -1.0))
    return jnp.where(j < n_sel, score, -2.0)


def _kv_pool_partials(x, pw0, pw1):
    rows = x.shape[0]
    n = rows // CMP_STRIDE
    pool = (lax.broadcasted_iota(jnp.int32, (n, rows), 1) // CMP_STRIDE
            == lax.broadcasted_iota(jnp.int32, (n, rows), 0)).astype(f32)
    a = jnp.dot(pool, x * pw0, preferred_element_type=f32, precision=HI)
    b = jnp.dot(pool, x * pw1, preferred_element_type=f32, precision=HI)
    return a, b


def _compress_kv(pooled, wck, wcv, kc_ref, vct_ref):
    for g in range(B_KV_HEADS):
        kc_ref[g] = _mm(pooled[:, g * 64:(g + 1) * 64], wck).astype(bf16)
        vc = _mm(pooled[:, B_KV_WIDTH + g * 64:B_KV_WIDTH + (g + 1) * 64], wcv)
        vct_ref[g] = vc.T.astype(bf16)


def _nsa_prep_kernel(q_ref, kvc_ref, kvs_ref, kvw_ref, pw0_ref, pw1_ref,
                     qt_ref, ks_ref, vst_ref, kw_ref, vwt_ref, a_ref, b_ref):
    qt_ref[...] = (q_ref[...] * ATTN_SCALE).T.astype(bf16)
    for src, k_ref, vt_ref in ((kvs_ref, ks_ref, vst_ref), (kvw_ref, kw_ref, vwt_ref)):
        x = src[...]
        for g in range(B_KV_HEADS):
            k_ref[g] = x[:, g * 64:(g + 1) * 64].astype(bf16)
        vt_ref[...] = x[:, B_KV_WIDTH:].T.astype(bf16)
    a, b = _kv_pool_partials(kvc_ref[...], pw0_ref[...], pw1_ref[...])
    a_ref[...] = a
    b_ref[...] = b


def _nsa_prep(u3, pw0, pw1, tt):
    B, T, _ = u3.shape
    n_ch = tt // CMP_STRIDE
    ublk = lambda w, off: pl.BlockSpec((None, tt, w), lambda b, t: (b, t, off // w))
    const = pl.BlockSpec((tt, 2 * B_KV_WIDTH), lambda b, t: (0, 0))
    return pl.pallas_call(
        _nsa_prep_kernel,
        out_shape=(jax.ShapeDtypeStruct((B, B_WIDTH, T), bf16),
                   jax.ShapeDtypeStruct((B, B_KV_HEADS, T, B_HEAD_DIM), bf16),
                   jax.ShapeDtypeStruct((B, B_KV_WIDTH, T), bf16),
                   jax.ShapeDtypeStruct((B, B_KV_HEADS, T, B_HEAD_DIM), bf16),
                   jax.ShapeDtypeStruct((B, B_KV_WIDTH, T), bf16),
                   jax.ShapeDtypeStruct((B, T // CMP_STRIDE, 2 * B_KV_WIDTH), f32),
                   jax.ShapeDtypeStruct((B, T // CMP_STRIDE, 2 * B_KV_WIDTH), f32)),
        grid=(B, T // tt),
        in_specs=[ublk(B_WIDTH, C_Q), ublk(2 * B_KV_WIDTH, C_KVC), ublk(2 * B_KV_WIDTH, C_KVS),
                  ublk(2 * B_KV_WIDTH, C_KVW), const, const],
        out_specs=(pl.BlockSpec((None, B_WIDTH, tt), lambda b, t: (b, 0, t)),
                   pl.BlockSpec((None, B_KV_HEADS, tt, B_HEAD_DIM), lambda b, t: (b, 0, t, 0)),
                   pl.BlockSpec((None, B_KV_WIDTH, tt), lambda b, t: (b, 0, t)),
                   pl.BlockSpec((None, B_KV_HEADS, tt, B_HEAD_DIM), lambda b, t: (b, 0, t, 0)),
                   pl.BlockSpec((None, B_KV_WIDTH, tt), lambda b, t: (b, 0, t)),
                   pl.BlockSpec((None, n_ch, 2 * B_KV_WIDTH), lambda b, t: (b, t, 0)),
                   pl.BlockSpec((None, n_ch, 2 * B_KV_WIDTH), lambda b, t: (b, t, 0))),
        compiler_params=_cparams(("parallel", "parallel")),
        name="nsa_prep",
    )(u3, u3, u3, u3, pw0, pw1)


def _rank_select(score_ref, n_rows, n_sel, lanes):
    score = score_ref[...].reshape(n_rows, lanes)
    j = lax.broadcasted_iota(jnp.int32, (n_rows, 1), 0)

    def body(i, rank):
        row = score_ref[i]
        beats = jnp.where(row > score, 1.0, jnp.where(row == score, jnp.where(i < j, 1.0, 0.0), 0.0))
        return rank + beats

    rank = lax.fori_loop(0, n_sel, body, jnp.zeros((n_rows, lanes), f32))
    return jnp.where(rank < float(min(N_SELECT, n_sel)), jnp.where(score >= 0.0, 1.0, 0.0), 0.0)


def _nsa_prompt_kernel(qt_ref, ks_ref, vst_ref, kw_ref, vwt_ref, a_ref, b_ref, wck_ref, wcv_ref, gates_ref, gb_ref,
                       o_ref, kc_ref, vct_ref, score_ref, sel_ref, ot_ref, *, seq, tq):
    i = pl.program_id(1)
    n_ch = seq // CMP_STRIDE
    n_sel = seq // SEL_BLOCK
    lanes = B_GROUP * tq
    tk = 2 * SEL_BLOCK

    @pl.when(i == 0)
    def _():
        pooled = a_ref[...] + pltpu.roll(b_ref[...], n_ch - 1, 0)
        _compress_kv(pooled, wck_ref[...], wcv_ref[...], kc_ref, vct_ref)

    tok_i = i * tq + lax.broadcasted_iota(jnp.int32, (1, tq), 1)
    t_row = _tile_heads(tok_i.astype(f32))
    gates_t = jax.nn.sigmoid(gates_ref[...]).T
    key_off = lax.broadcasted_iota(jnp.int32, (tk, 1), 0).astype(f32)

    for g in range(B_KV_HEADS):
        q_t = jnp.concatenate([qt_ref[(g * B_GROUP + r) * 64:(g * B_GROUP + r + 1) * 64, :] for r in range(B_GROUP)],
                              axis=1)
        slope_row = _slope_row(g, tq)

        o_cmp, p = _compressed_branch(kc_ref[g], vct_ref[g], q_t, t_row, slope_row, n_ch)
        score = _selection_scores(p, tq, tok_i, n_sel, n_sel, n_ch)
        score4 = _tile_heads(score)
        for j in range(n_sel):
            score_ref[j] = score4[j:j + 1, :]
        sel = _rank_select(score_ref, n_sel, n_sel, lanes)
        for j in range(n_sel):
            sel_ref[j] = sel[j:j + 1, :]

        def sel_body(kt, carry):
            off = pl.multiple_of(kt * tk, tk)
            s = jnp.dot(ks_ref[g, pl.ds(off, tk), :], q_t, preferred_element_type=f32)
            dist = t_row - (key_off + (kt * tk).astype(f32))
            keep = jnp.concatenate([jnp.broadcast_to(sel_ref[2 * kt], (SEL_BLOCK, lanes)),
                                    jnp.broadcast_to(sel_ref[2 * kt + 1], (SEL_BLOCK, lanes))], axis=0) > 0.5
            return _attn_step(s, dist, slope_row, (keep,), carry, vst_ref[g * 64:(g + 1) * 64, pl.ds(off, tk)])

        o_sel = _attn_finish(lax.fori_loop(0, (i * tq) // tk + tq // tk, sel_body, _attn_init(lanes)))

        def win_body(kt, carry):
            off = pl.multiple_of(kt * tk, tk)
            s = jnp.dot(kw_ref[g, pl.ds(off, tk), :], q_t, preferred_element_type=f32)
            dist = t_row - (key_off + (kt * tk).astype(f32))
            return _attn_step(s, dist, slope_row, (dist <= float(WINDOW),), carry,
                              vwt_ref[g * 64:(g + 1) * 64, pl.ds(off, tk)])

        lo = jnp.maximum(i * tq - WINDOW, 0) // tk
        o_win = _attn_finish(lax.fori_loop(lo, (i * tq) // tk + tq // tk, win_body, _attn_init(lanes)))

        def gate_row(branch):
            base = branch * B_HEADS + g * B_GROUP
            return jnp.concatenate([gates_t[base + r:base + r + 1, :] for r in range(B_GROUP)], axis=1)

        o_t = gate_row(0) * o_cmp + gate_row(1) * o_sel + gate_row(2) * o_win
        for r in range(B_GROUP):
            ot_ref[(g * B_GROUP + r) * 64:(g * B_GROUP + r + 1) * 64, :] = o_t[:, r * tq:(r + 1) * tq]

    gb = gb_ref[...]
    o_ref[...] = ot_ref[...].T * (gb * jax.nn.sigmoid(gb))


def _nsa_prompt(u3, qt, ks, vst, kw, vwt, a, b, wck, wcv, tq):
    B, T, _ = u3.shape
    n_ch = T // CMP_STRIDE
    n_sel = T // SEL_BLOCK
    lanes = B_GROUP * tq
    per_b3 = lambda s1, s2: pl.BlockSpec((None, s1, s2), lambda b, i: (b, 0, 0))
    per_b4 = pl.BlockSpec((None, B_KV_HEADS, T, B_HEAD_DIM), lambda b, i: (b, 0, 0, 0))
    w_spec = pl.BlockSpec((B_HEAD_DIM, B_HEAD_DIM), lambda b, i: (0, 0))
    return pl.pallas_call(
        functools.partial(_nsa_prompt_kernel, seq=T, tq=tq),
        out_shape=jax.ShapeDtypeStruct((B, T, B_WIDTH), f32),
        grid=(B, T // tq),
        in_specs=[pl.BlockSpec((None, B_WIDTH, tq), lambda b, i: (b, 0, i)),
                  per_b4, per_b3(B_KV_WIDTH, T), per_b4, per_b3(B_KV_WIDTH, T),
                  per_b3(n_ch, 2 * B_KV_WIDTH), per_b3(n_ch, 2 * B_KV_WIDTH), w_spec, w_spec,
                  pl.BlockSpec((None, tq, 128), lambda b, i: (b, i, C_NG // 128)),
                  pl.BlockSpec((None, tq, B_WIDTH), lambda b, i: (b, i, C_GB // B_WIDTH))],
        out_specs=pl.BlockSpec((None, tq, B_WIDTH), lambda b, i: (b, i, 0)),
        scratch_shapes=[pltpu.VMEM((B_KV_HEADS, n_ch, B_HEAD_DIM), bf16),
                        pltpu.VMEM((B_KV_HEADS, B_HEAD_DIM, n_ch), bf16),
                        pltpu.VMEM((n_sel, 1, lanes), f32),
                        pltpu.VMEM((n_sel, 1, lanes), f32),
                        pltpu.VMEM((B_WIDTH, tq), f32)],
        compiler_params=_cparams(("parallel", "arbitrary")),
        name="nsa_prompt",
    )(qt, ks, vst, kw, vwt, a, b, wck, wcv, u3, u3)


def _pad_rows_transpose(x, rows):
    return jnp.concatenate([x, jnp.zeros((rows - x.shape[0], x.shape[1]), x.dtype)], axis=0).T


def _nsa_sample_kernel(pt_ref, cmp_ref, selp_ref, win_ref, q_ref, news_ref, neww_ref, pw0_ref, pw1_ref, wck_ref,
                       wcv_ref, gates_ref, gb_ref, o_ref,
                       a_s, b_s, kc_ref, vct_ref, qg_ref, score_ref, keep_ref, m_ref, l_ref, acc_ref, ocmp_ref, ot_ref,
                       *, n_pages, t_new, t_pad, tq):
    p = pl.program_id(1)
    past = n_pages * PAGE_SIZE
    n_ch = past // CMP_STRIDE
    n_sel = past // SEL_BLOCK + 1
    n_rows = keep_ref.shape[1]
    lanes = B_GROUP * tq
    tok_i = past + lax.broadcasted_iota(jnp.int32, (1, tq), 1)
    t_row = _tile_heads(tok_i.astype(f32))
    key_off = lax.broadcasted_iota(jnp.int32, (PAGE_SIZE, 1), 0).astype(f32)
    new_off = lax.broadcasted_iota(jnp.int32, (t_pad, 1), 0).astype(f32)

    @pl.when(p == 0)
    def _():
        q_t = (_pad_rows_transpose(q_ref[...], 128) * ATTN_SCALE)[:, :tq]
        for g in range(B_KV_HEADS):
            qg_ref[g] = jnp.concatenate(
                [q_t[(g * B_GROUP + r) * 64:(g * B_GROUP + r + 1) * 64, :] for r in range(B_GROUP)], axis=1).astype(bf16)
            m, l, acc = _attn_init(lanes)
            m_ref[g] = m
            l_ref[g] = l
            acc_ref[g] = acc

    @pl.when(p < n_pages)
    def _():
        a, b = _kv_pool_partials(cmp_ref[...], pw0_ref[...], pw1_ref[...])
        off = pl.multiple_of(p * (PAGE_SIZE // CMP_STRIDE), PAGE_SIZE // CMP_STRIDE)
        a_s[pl.ds(off, PAGE_SIZE // CMP_STRIDE), :] = a
        b_s[pl.ds(off, PAGE_SIZE // CMP_STRIDE), :] = b

    @pl.when(p == n_pages - 1)
    def _():
        pooled = a_s[...] + pltpu.roll(b_s[...], n_ch - 1, 0)
        _compress_kv(pooled, wck_ref[...], wcv_ref[...], kc_ref, vct_ref)
        for g in range(B_KV_HEADS):
            slope_row = _slope_row(g, tq)
            o_cmp, prob = _compressed_branch(kc_ref[g], vct_ref[g], qg_ref[g], t_row, slope_row, n_ch)
            ocmp_ref[g] = o_cmp
            score_ref[...] = _tile_heads(_selection_scores(prob, tq, tok_i, n_rows, n_sel, n_ch)).reshape(n_rows, 1, lanes)
            keep_ref[g] = _rank_select(score_ref, n_rows, n_sel, lanes).reshape(n_rows, 1, lanes)

    def attend(g, k, v, dist, masks, v_rows=False):
        s = jnp.dot(k, qg_ref[g], preferred_element_type=f32)
        m, l, acc = _attn_step(s, dist, _slope_row(g, tq), masks, (m_ref[g], l_ref[g], acc_ref[g]), v, v_rows)
        m_ref[g] = m
        l_ref[g] = l
        acc_ref[g] = acc

    @pl.when(p >= n_pages)
    def _():
        pp = p - n_pages
        x = selp_ref[...]
        dist = t_row - (key_off + (pp * PAGE_SIZE).astype(f32))
        for g in range(B_KV_HEADS):
            keep = jnp.concatenate([jnp.broadcast_to(keep_ref[g, 2 * pp], (SEL_BLOCK, lanes)),
                                    jnp.broadcast_to(keep_ref[g, 2 * pp + 1], (SEL_BLOCK, lanes))], axis=0) > 0.5
            attend(g, x[:, g * 64:(g + 1) * 64].astype(bf16),
                   x[:, B_KV_WIDTH + g * 64:B_KV_WIDTH + (g + 1) * 64].T.astype(bf16), dist, (keep,))

    @pl.when(p == 2 * n_pages - 1)
    def _():
        gates_t = _pad_rows_transpose(jax.nn.sigmoid(gates_ref[...]), 128)
        ot_ref[...] = jnp.zeros(ot_ref.shape, f32)
        xs = news_ref[...]
        xw = neww_ref[...]
        new_dist = t_row - (new_off + float(past))
        n_win = win_ref.shape[0]
        for g in range(B_KV_HEADS):
            ksl = slice(g * 64, (g + 1) * 64)
            vsl = slice(B_KV_WIDTH + g * 64, B_KV_WIDTH + (g + 1) * 64)
            keep = jnp.broadcast_to(keep_ref[g, n_sel - 1], (t_pad, lanes)) > 0.5
            attend(g, xs[:, ksl].astype(bf16), xs[:, vsl], new_dist, (keep,), v_rows=True)
            o_sel = _attn_finish((m_ref[g], l_ref[g], acc_ref[g]))

            m, l, acc = _attn_init(lanes)
            m_ref[g] = m
            l_ref[g] = l
            acc_ref[g] = acc
            for wt in range(n_win // PAGE_SIZE):
                x = win_ref[wt * PAGE_SIZE:(wt + 1) * PAGE_SIZE, :]
                dist = t_row - (key_off + float(past - n_win + wt * PAGE_SIZE))
                attend(g, x[:, ksl].astype(bf16), x[:, vsl].T.astype(bf16), dist, (dist <= float(WINDOW),))
            attend(g, xw[:, ksl].astype(bf16), xw[:, vsl], new_dist, (new_dist <= float(WINDOW),), v_rows=True)
            o_win = _attn_finish((m_ref[g], l_ref[g], acc_ref[g]))

            def gate_row(branch):
                base = branch * B_HEADS + g * B_GROUP
                return jnp.concatenate([gates_t[base + r:base + r + 1, :tq] for r in range(B_GROUP)], axis=1)

            o_t = gate_row(0) * ocmp_ref[g] + gate_row(1) * o_sel + gate_row(2) * o_win
            for r in range(B_GROUP):
                ot_ref[(g * B_GROUP + r) * 64:(g * B_GROUP + r + 1) * 64, 0:tq] = o_t[:, r * tq:(r + 1) * tq]
        gb = gb_ref[...]
        o_ref[...] = ot_ref[...].T[:t_pad, :] * (gb * jax.nn.sigmoid(gb))


def _nsa_sample(u3, cache_cmp, cache_sel, cache_win, page_table, pw0, pw1, wck, wcv, *, t_new, tq):
    DB, t_pad, _ = u3.shape
    n_pages = page_table.shape[1]
    assert t_new < CMP_STRIDE and t_new <= t_pad <= tq and cache_win.shape[1] % PAGE_SIZE == 0
    past = n_pages * PAGE_SIZE
    n_ch = past // CMP_STRIDE
    n_sel = past // SEL_BLOCK + 1
    n_rows = -(-n_sel // 8) * 8
    lanes = B_GROUP * tq
    kvw = 2 * B_KV_WIDTH
    ublk = lambda w, off: pl.BlockSpec((None, t_pad, w), lambda b, p, pt: (b, 0, off // w))
    const = lambda s1, s2: pl.BlockSpec((s1, s2), lambda b, p, pt: (0, 0))
    grid_spec = pltpu.PrefetchScalarGridSpec(
        num_scalar_prefetch=1,
        grid=(DB, 2 * n_pages),
        in_specs=[
            pl.BlockSpec((None, PAGE_SIZE, kvw), lambda b, p, pt: (pt[b, jnp.minimum(p, n_pages - 1)], 0, 0)),
            pl.BlockSpec((None, PAGE_SIZE, kvw), lambda b, p, pt: (pt[b, jnp.maximum(p - n_pages, 0)], 0, 0)),
            pl.BlockSpec((None, cache_win.shape[1], kvw), lambda b, p, pt: (b, 0, 0)),
            ublk(B_WIDTH, C_Q), ublk(kvw, C_KVS), ublk(kvw, C_KVW),
            const(PAGE_SIZE, kvw), const(PAGE_SIZE, kvw), const(B_HEAD_DIM, B_HEAD_DIM), const(B_HEAD_DIM, B_HEAD_DIM),
            ublk(128, C_NG), ublk(B_WIDTH, C_GB)],
        out_specs=pl.BlockSpec((None, t_pad, B_WIDTH), lambda b, p, pt: (b, 0, 0)),
        scratch_shapes=[pltpu.VMEM((n_ch, kvw), f32), pltpu.VMEM((n_ch, kvw), f32),
                        pltpu.VMEM((B_KV_HEADS, n_ch, B_HEAD_DIM), bf16),
                        pltpu.VMEM((B_KV_HEADS, B_HEAD_DIM, n_ch), bf16),
                        pltpu.VMEM((B_KV_HEADS, B_HEAD_DIM, lanes), bf16),
                        pltpu.VMEM((n_rows, 1, lanes), f32),
                        pltpu.VMEM((B_KV_HEADS, n_rows, 1, lanes), f32),
                        pltpu.VMEM((B_KV_HEADS, 1, lanes), f32), pltpu.VMEM((B_KV_HEADS, 1, lanes), f32),
                        pltpu.VMEM((B_KV_HEADS, B_HEAD_DIM, lanes), f32),
                        pltpu.VMEM((B_KV_HEADS, B_HEAD_DIM, lanes), f32),
                        pltpu.VMEM((B_WIDTH, 128), f32)])
    return pl.pallas_call(
        functools.partial(_nsa_sample_kernel, n_pages=n_pages, t_new=t_new, t_pad=t_pad, tq=tq),
        out_shape=jax.ShapeDtypeStruct((DB, t_pad, B_WIDTH), f32),
        grid_spec=grid_spec,
        compiler_params=_cparams(("parallel", "arbitrary")),
        name="nsa_sample",
    )(page_table, cache_cmp, cache_sel, cache_win, u3, u3, u3, pw0, pw1, wck, wcv, u3, u3)


def _pos_weight_tiles(pos_k, pos_v, rows):
    def half(lo):
        t = jnp.concatenate([jnp.tile(pos_k[lo:lo + CMP_STRIDE], (1, B_KV_HEADS)),
                             jnp.tile(pos_v[lo:lo + CMP_STRIDE], (1, B_KV_HEADS))], axis=1)
        return jnp.tile(t, (rows // CMP_STRIDE, 1))
    return half(0), half(CMP_STRIDE)


def _a_cols(u_rows):
    return jnp.concatenate([u_rows[..., :4 * A_WIDTH], u_rows[..., C_LR:C_LR + DECAY_RANK + ICLR_RANK]], axis=-1)


def kernel(x_prompt, x_sample, cache_cmp_kv, cache_sel_kv, cache_win_kv, state_rwkv, state_shift, page_table, norm_in,
           w_in, mu_shift, w0, w_up, a0, a_up, k_k, k_a, r_k, gn_w, gn_b, cmp_pos_k, cmp_pos_v, w_cmp_k, w_cmp_v,
           w_pa, w_pb, w_o, norm_out):
    assert w_in.shape[0] == 1, "one layer"
    B, T, _ = x_prompt.shape
    DB, TS, _ = x_sample.shape
    H, N = A_HEADS, A_HEAD_DIM
    kvs = (2, B_KV_HEADS, B_HEAD_DIM)
    p = dict(mu_shift=mu_shift[0], w0=w0[0], w_up=w_up[0], a0=a0[0], a_up=a_up[0], k_k=k_k[0], k_a=k_a[0],
             r_k=r_k[0].reshape(-1), gn_w=gn_w[0], gn_b=gn_b[0])
    w_packed = _pack_w_in(w_in[0])
    wpa, wpb, wo = w_pa[0].astype(bf16), w_pb[0].astype(bf16), w_o[0].astype(bf16)

    xp = x_prompt.reshape(B * T, D_MODEL)
    up = _proj_in(xp, norm_in[0], w_packed)
    up3 = up.reshape(B, T, U_COLS)
    oa_p, s_p = _rwkv(up3, jnp.zeros((B, 1, 4 * A_WIDTH), f32), jnp.zeros((B, 1, 128), f32),
                      jnp.zeros((B, H, N, N), f32), p, chunk=64, t_valid=64)
    tt = 512
    pw0, pw1 = _pos_weight_tiles(cmp_pos_k[0], cmp_pos_v[0], tt)
    qt, ks, vst, kw, vwt, pa, pb = _nsa_prep(up3, pw0, pw1, tt)
    ob_p = _nsa_prompt(up3, qt, ks, vst, kw, vwt, pa, pb, w_cmp_k[0], w_cmp_v[0], 128)
    y_p = _merge(xp, oa_p.reshape(B * T, A_WIDTH), ob_p.reshape(B * T, B_WIDTH), up, wpa, wpb, wo, norm_out)

    t_pad = 8
    xs = x_sample.reshape(DB * TS, D_MODEL)
    us = _proj_in(xs, norm_in[0], w_packed)
    us3 = us.reshape(DB, TS, U_COLS)
    us3p = jnp.pad(us3, ((0, 0), (0, t_pad - TS), (0, 0)))
    shift0 = state_shift[0]
    oa_s, s_s = _rwkv(us3p, shift0[:, None, :4 * A_WIDTH], shift0[:, None, 4 * A_WIDTH:], state_rwkv[0], p,
                      chunk=t_pad, t_valid=TS)
    n_pool = cache_cmp_kv.shape[1]
    win = cache_win_kv[0]
    ob_s = _nsa_sample(us3p, cache_cmp_kv[0].reshape(n_pool, PAGE_SIZE, 2 * B_KV_WIDTH),
                       cache_sel_kv[0].reshape(n_pool, PAGE_SIZE, 2 * B_KV_WIDTH),
                       win.reshape(DB, win.shape[1], 2 * B_KV_WIDTH), page_table,
                       pw0[:PAGE_SIZE], pw1[:PAGE_SIZE], w_cmp_k[0], w_cmp_v[0], t_new=TS, tq=32)
    y_s = _merge(xs, oa_s[:, :TS].reshape(DB * TS, A_WIDTH), ob_s[:, :TS].reshape(DB * TS, B_WIDTH), us, wpa, wpb, wo,
                 norm_out)

    def kv_out(u3_, col, lead, t):
        return u3_[..., col:col + 2 * B_KV_WIDTH].reshape((1, lead, t) + kvs)

    wk = min(WINDOW, T)
    new_w_s = kv_out(us3, C_KVW, DB, TS)[0]
    s_win = jnp.concatenate([win, new_w_s], axis=1)[:, TS:][None]
    return (y_p.reshape(B, T, D_MODEL), y_s.reshape(DB, TS, D_MODEL),
            kv_out(up3, C_KVC, B, T), kv_out(up3, C_KVS, B, T), kv_out(up3[:, T - wk:], C_KVW, B, wk),
            s_p[None], _a_cols(up3[:, T - 1])[None],
            kv_out(us3, C_KVC, DB, TS), kv_out(us3, C_KVS, DB, TS), s_win,
            s_s[None], _a_cols(us3[:, TS - 1])[None])
```

```python
import functools
import math

import jax
import jax.numpy as jnp
from jax import lax
from jax.experimental import pallas as pl
from jax.experimental.pallas import tpu as pltpu

f32 = jnp.float32
bf16 = jnp.bfloat16

D_MODEL = 2048
PAGE_SIZE = 128
A_HEADS = 16
A_HEAD_DIM = 64
A_WIDTH = A_HEADS * A_HEAD_DIM
DECAY_RANK = 64
ICLR_RANK = 64
A_COLS = 4 * A_WIDTH + DECAY_RANK + ICLR_RANK
GN_EPS = 64e-5
B_HEADS = 16
B_KV_HEADS = 4
B_GROUP = B_HEADS // B_KV_HEADS
B_HEAD_DIM = 64
B_WIDTH = B_HEADS * B_HEAD_DIM
B_KV_WIDTH = B_KV_HEADS * B_HEAD_DIM
B_COLS = 2 * B_WIDTH + 6 * B_KV_WIDTH + 3 * B_HEADS
CMP_BLOCK = 32
CMP_STRIDE = 16
SEL_BLOCK = 64
N_SELECT = 16
N_LOCAL = 2
WINDOW = 512
FORCED_SCORE = 1e4
ATTN_SCALE = B_HEAD_DIM ** -0.5
RMS_EPS = 1e-6
NEG = -1e30

C_RKVG = 0
C_Q = 4096
C_GB = 5120
C_GA_M = 6144
C_GB_M = 8192
C_KVC = 10240
C_KVS = 10752
C_KVW = 11264
C_LR = 11776
C_NG = 11904
U_COLS = 12288

VMEM_LIMIT = 56 * 1024 * 1024
HI = lax.Precision.HIGHEST


def _cparams(sem):
    return pltpu.CompilerParams(dimension_semantics=sem, vmem_limit_bytes=VMEM_LIMIT)


def _pack_w_in(w):
    a, b, m = w[:, :A_COLS], w[:, A_COLS:A_COLS + B_COLS], w[:, A_COLS + B_COLS:]
    z = jnp.zeros((w.shape[0], U_COLS - C_NG - 3 * B_HEADS), w.dtype)
    return jnp.concatenate(
        [a[:, :4 * A_WIDTH], b[:, :2 * B_WIDTH], m, b[:, 2 * B_WIDTH:2 * B_WIDTH + 6 * B_KV_WIDTH],
         a[:, 4 * A_WIDTH:], b[:, 2 * B_WIDTH + 6 * B_KV_WIDTH:], z], axis=1).astype(bf16)


def _proj_in_kernel(x_ref, g_ref, w_ref, o_ref, xn_ref):
    @pl.when(pl.program_id(1) == 0)
    def _():
        x = x_ref[...]
        ms = jnp.mean(x * x, axis=-1, keepdims=True)
        xn_ref[...] = (x * lax.rsqrt(ms + RMS_EPS) * g_ref[...]).astype(bf16)

    o_ref[...] = jnp.dot(xn_ref[...], w_ref[...], preferred_element_type=f32)


def _proj_in(x2d, norm_g, w_packed):
    m = x2d.shape[0]
    tm = min(1024, m)
    tn = 1024
    return pl.pallas_call(
        _proj_in_kernel,
        out_shape=jax.ShapeDtypeStruct((m, U_COLS), f32),
        grid=(m // tm, U_COLS // tn),
        in_specs=[pl.BlockSpec((tm, D_MODEL), lambda i, j: (i, 0)),
                  pl.BlockSpec((1, D_MODEL), lambda i, j: (0, 0)),
                  pl.BlockSpec((D_MODEL, tn), lambda i, j: (0, j))],
        out_specs=pl.BlockSpec((tm, tn), lambda i, j: (i, j)),
        scratch_shapes=[pltpu.VMEM((tm, D_MODEL), bf16)],
        compiler_params=_cparams(("parallel", "arbitrary")),
        name="proj_in",
    )(x2d, norm_g.reshape(1, D_MODEL), w_packed)


def _mm(a, b):
    return jnp.dot(a.astype(bf16), b.astype(bf16), preferred_element_type=f32)


def _mm_nt(a, b):
    return lax.dot_general(a.astype(bf16), b.astype(bf16), (((1,), (1,)), ((), ())), preferred_element_type=f32)


def _mm_tn(a, b):
    return lax.dot_general(a.astype(bf16), b.astype(bf16), (((0,), (0,)), ((), ())), preferred_element_type=f32)


def _rwkv_kernel(um_ref, ulr_ref, pm_ref, plr_ref, s0_ref, mum_ref, mulr_ref, w0_ref, wup_ref, a0_ref,
                 aup_ref, kk_ref, ka_ref, rk_ref, gnw_ref, gnb_ref, o_ref, sout_ref,
                 s_ref, prevm_ref, prevlr_ref, *, chunk, t_valid):
    C = chunk
    H, N = A_HEADS, A_HEAD_DIM
    c = pl.program_id(1)

    @pl.when(c == 0)
    def _():
        s_ref[...] = s0_ref[...]
        prevm_ref[...] = pm_ref[...]
        prevlr_ref[...] = plr_ref[...]

    um = um_ref[...]
    ulr = ulr_ref[...]
    row = lax.broadcasted_iota(jnp.int32, (C, 1), 0)

    def shifted(u, prev):
        return jnp.where(row == 0, prev, pltpu.roll(u, 1, 0))

    usm = um + mum_ref[...] * (shifted(um, prevm_ref[...]) - um)
    uslr = ulr + mulr_ref[...] * (shifted(ulr, prevlr_ref[...]) - ulr)
    prevm_ref[...] = um[C - 1:C, :]
    prevlr_ref[...] = ulr[C - 1:C, :]

    wd = uslr[:, :DECAY_RANK]
    ad = uslr[:, DECAY_RANK:]
    zw = w0_ref[...] + jnp.dot(jnp.tanh(wd), wup_ref[...], preferred_element_type=f32, precision=HI)
    softplus_neg = jnp.maximum(-zw, 0.0) + jnp.log(1.0 + jnp.exp(-jnp.abs(zw)))
    logw = -jnp.exp(-softplus_neg - 0.5)
    a_all = jax.nn.sigmoid(a0_ref[...] + jnp.dot(ad, aup_ref[...], preferred_element_type=f32, precision=HI))
    if t_valid < C:
        valid = (row < t_valid).astype(f32)
        logw = logw * valid
    ti = lax.broadcasted_iota(jnp.int32, (C, C), 0)
    si = lax.broadcasted_iota(jnp.int32, (C, C), 1)
    tril_incl = ti >= si
    tril_strict = ti > si
    cum = jnp.dot(tril_incl.astype(f32), logw, preferred_element_type=f32, precision=HI)
    e_pos = jnp.exp(cum)
    e_neg = jnp.exp(-cum)
    e_prev = jnp.exp(cum - logw)
    e_last = e_pos[C - 1:C, :]
    eye = (ti == si).astype(f32)
    n_sq = int(math.log2(C)) - 1

    hs = range(H)
    sls = [slice(h * N, (h + 1) * N) for h in hs]
    sec = lambda i: [usm[:, i * A_WIDTH + h * N:i * A_WIDTH + (h + 1) * N] for h in hs]
    r, k, v, g = sec(0), sec(1), sec(2), sec(3)
    a = [a_all[:, sl] for sl in sls]
    kk = [k[h] * kk_ref[:, sls[h]] for h in hs]
    ssq = [jnp.sum(kk[h] * kk[h], axis=-1, keepdims=True) for h in hs]
    kk = [kk[h] * lax.rsqrt(jnp.maximum(ssq[h], 1e-24)) for h in hs]
    k2 = [k[h] * (1.0 + (a[h] - 1.0) * ka_ref[:, sls[h]]) for h in hs]
    bonus = [jnp.sum(r[h] * k2[h] * rk_ref[:, sls[h]], axis=-1, keepdims=True) for h in hs]
    kn = [k2[h] * e_neg[:, sls[h]] for h in hs]
    bn = [kk[h] * a[h] * e_neg[:, sls[h]] for h in hs]
    if t_valid < C:
        kn = [x * valid for x in kn]
        bn = [x * valid for x in bn]
    lhs = [jnp.concatenate([kk[h] * e_prev[:, sls[h]], r[h] * e_pos[:, sls[h]]], axis=0) for h in hs]
    s0 = [s_ref[h] for h in hs]
    qk_b = [_mm_nt(lhs[h], bn[h]) for h in hs]
    qk_k = [_mm_nt(lhs[h], kn[h]) for h in hs]
    w0s = [_mm_nt(lhs[h], s0[h]) for h in hs]
    pw = [jnp.where(tril_strict, -qk_b[h][:C], 0.0) for h in hs]
    tinv = [eye + pw[h] for h in hs]
    for _ in range(n_sq):
        pw = [_mm(pw[h], pw[h]) for h in hs]
        tinv = [tinv[h] + _mm(tinv[h], pw[h]) for h in hs]
    rhs_u = [w0s[h][:C] + _mm(jnp.where(tril_strict, qk_k[h][:C], 0.0), v[h]) for h in hs]
    u = [_mm(tinv[h], rhs_u[h]) for h in hs]
    o = [w0s[h][C:] + _mm(jnp.where(tril_incl, qk_k[h][C:], 0.0), v[h])
         - _mm(jnp.where(tril_incl, qk_b[h][C:], 0.0), u[h]) for h in hs]
    for h in hs:
        s_ref[h] = (s0[h] + _mm_tn(v[h], kn[h]) - _mm_tn(u[h], bn[h])) * e_last[:, sls[h]]
    mean = [jnp.mean(o[h], axis=-1, keepdims=True) for h in hs]
    d = [o[h] - mean[h] for h in hs]
    var = [jnp.mean(d[h] * d[h], axis=-1, keepdims=True) for h in hs]
    for h in hs:
        on = d[h] * lax.rsqrt(var[h] + GN_EPS) * gnw_ref[:, sls[h]] + gnb_ref[:, sls[h]] + bonus[h] * v[h]
        o_ref[:, sls[h]] = on * (g[h] * jax.nn.sigmoid(g[h]))

    @pl.when(c == pl.num_programs(1) - 1)
    def _():
        sout_ref[...] = s_ref[...]


def _rwkv(u3, prev_m, prev_lr, s0, p, *, chunk, t_valid):
    B, T, _ = u3.shape
    C = chunk
    H, N = A_HEADS, A_HEAD_DIM
    row = lambda v: v.reshape(1, -1)
    vec = lambda n: pl.BlockSpec((1, n), lambda b, c: (0, 0))
    in_specs = [
        pl.BlockSpec((None, C, 4 * A_WIDTH), lambda b, c: (b, c, 0)),
        pl.BlockSpec((None, C, 128), lambda b, c: (b, c, C_LR // 128)),
        pl.BlockSpec((None, 1, 4 * A_WIDTH), lambda b, c: (b, 0, 0)),
        pl.BlockSpec((None, 1, 128), lambda b, c: (b, 0, 0)),
        pl.BlockSpec((None, H, N, N), lambda b, c: (b, 0, 0, 0)),
        vec(4 * A_WIDTH), vec(128), vec(A_WIDTH),
        pl.BlockSpec((DECAY_RANK, A_WIDTH), lambda b, c: (0, 0)),
        vec(A_WIDTH),
        pl.BlockSpec((ICLR_RANK, A_WIDTH), lambda b, c: (0, 0)),
        vec(A_WIDTH), vec(A_WIDTH), vec(A_WIDTH), vec(A_WIDTH), vec(A_WIDTH),
    ]
    return pl.pallas_call(
        functools.partial(_rwkv_kernel, chunk=C, t_valid=t_valid),
        out_shape=(jax.ShapeDtypeStruct((B, T, A_WIDTH), f32), jax.ShapeDtypeStruct((B, H, N, N), f32)),
        grid=(B, T // C),
        in_specs=in_specs,
        out_specs=(pl.BlockSpec((None, C, A_WIDTH), lambda b, c: (b, c, 0)),
                   pl.BlockSpec((None, H, N, N), lambda b, c: (b, 0, 0, 0))),
        scratch_shapes=[pltpu.VMEM((H, N, N), f32), pltpu.VMEM((1, 4 * A_WIDTH), f32), pltpu.VMEM((1, 128), f32)],
        compiler_params=_cparams(("parallel", "arbitrary")),
        name="rwkv7",
    )(u3, u3, prev_m, prev_lr, s0, row(p["mu_shift"][:4 * A_WIDTH]), row(p["mu_shift"][4 * A_WIDTH:]),
      row(p["w0"]), p["w_up"], row(p["a0"]), p["a_up"], row(p["k_k"]), row(p["k_a"]), row(p["r_k"]),
      row(p["gn_w"]), row(p["gn_b"]))


def _merge_kernel(x_ref, oa_ref, ob_ref, ga_ref, gb_ref, wpa_ref, wpb_ref, wo_ref, gout_ref, y_ref):
    pa = jnp.dot(oa_ref[...].astype(bf16), wpa_ref[...], preferred_element_type=f32)
    pb = jnp.dot(ob_ref[...].astype(bf16), wpb_ref[...], preferred_element_type=f32)
    merged = jax.nn.sigmoid(ga_ref[...]) * pa + jax.nn.sigmoid(gb_ref[...]) * pb
    h = x_ref[...] + jnp.dot(merged.astype(bf16), wo_ref[...], preferred_element_type=f32)
    ms = jnp.mean(h * h, axis=-1, keepdims=True)
    y_ref[...] = h * lax.rsqrt(ms + RMS_EPS) * gout_ref[...]


def _merge(x2d, o_a, o_b, u2d, w_pa, w_pb, w_o, norm_out):
    m = x2d.shape[0]
    tm = min(256, m)
    const = lambda shape: pl.BlockSpec(shape, lambda i: (0, 0), pipeline_mode=pl.Buffered(1))
    return pl.pallas_call(
        _merge_kernel,
        out_shape=jax.ShapeDtypeStruct((m, D_MODEL), f32),
        grid=(m // tm,),
        in_specs=[pl.BlockSpec((tm, D_MODEL), lambda i: (i, 0)),
                  pl.BlockSpec((tm, A_WIDTH), lambda i: (i, 0)),
                  pl.BlockSpec((tm, B_WIDTH), lambda i: (i, 0)),
                  pl.BlockSpec((tm, D_MODEL), lambda i: (i, C_GA_M // D_MODEL)),
                  pl.BlockSpec((tm, D_MODEL), lambda i: (i, C_GB_M // D_MODEL)),
                  const((A_WIDTH, D_MODEL)), const((B_WIDTH, D_MODEL)), const((D_MODEL, D_MODEL)),
                  const((1, D_MODEL))],
        out_specs=pl.BlockSpec((tm, D_MODEL), lambda i: (i, 0)),
        compiler_params=_cparams(("parallel",)),
        name="merge_out",
    )(x2d, o_a, o_b, u2d, u2d, w_pa, w_pb, w_o, norm_out.reshape(1, D_MODEL))


def _alibi_slope(head):
    return 2.0 ** (-8.0 * (head + 1) / B_HEADS)


def _slope_row(g, tq):
    lane_head = lax.broadcasted_iota(jnp.int32, (1, B_GROUP * tq), 1) // tq
    out = jnp.zeros((1, B_GROUP * tq), f32)
    for r in range(B_GROUP):
        out = jnp.where(lane_head == r, _alibi_slope(g * B_GROUP + r), out)
    return out


def _tile_heads(row):
    return jnp.concatenate([row] * B_GROUP, axis=1)


def _overlap_t(n_sel_rows, n_ch):
    j = lax.broadcasted_iota(jnp.int32, (n_sel_rows, n_ch), 0) * SEL_BLOCK
    n = lax.broadcasted_iota(jnp.int32, (n_sel_rows, n_ch), 1) * CMP_STRIDE
    ov = jnp.minimum(n + CMP_BLOCK, j + SEL_BLOCK) - jnp.maximum(n, j)
    return jnp.maximum(ov, 0).astype(f32) * (1.0 / CMP_BLOCK)


def _attn_step(s, dist, slope_row, extra_masks, carry, v_t, v_rows=False):
    m, l, acc = carry
    s = s - slope_row * dist
    s = jnp.where(dist >= 0.0, s, NEG)
    for keep in extra_masks:
        s = jnp.where(keep, s, NEG)
    m_new = jnp.maximum(m, jnp.max(s, axis=0, keepdims=True))
    alpha = jnp.exp(m - m_new)
    p = jnp.where(s > 0.5 * NEG, jnp.exp(s - m_new), 0.0)
    l = l * alpha + jnp.sum(p, axis=0, keepdims=True)
    if v_rows:
        pv = _mm_tn(v_t, p)
    else:
        pv = jnp.dot(v_t, p.astype(bf16), preferred_element_type=f32)
    return m_new, l, acc * alpha + pv


def _attn_step_multi(s, dist, slope_rows, biases, masks, carries, v_t):
    n = range(len(s))
    causal = dist >= 0.0
    s = [s[i] - slope_rows[i] * dist for i in n]
    s = [jnp.where(causal, s[i], NEG) for i in n]
    s = [s[i] if masks[i] is None else jnp.where(masks[i], s[i], NEG) for i in n]
    s = [s[i] if biases[i] is None else
         jnp.concatenate([s[i][:SEL_BLOCK] + biases[i][0], s[i][SEL_BLOCK:] + biases[i][1]], axis=0) for i in n]
    m_new = [jnp.maximum(carries[i][0], jnp.max(s[i], axis=0, keepdims=True)) for i in n]
    alpha = [jnp.exp(carries[i][0] - m_new[i]) for i in n]
    p = [jnp.where(s[i] > 0.5 * NEG, jnp.exp(s[i] - m_new[i]), 0.0) for i in n]
    l = [carries[i][1] * alpha[i] + jnp.sum(p[i], axis=0, keepdims=True) for i in n]
    pv = [jnp.dot(v_t[i], p[i].astype(bf16), preferred_element_type=f32) for i in n]
    return [(m_new[i], l[i], carries[i][2] * alpha[i] + pv[i]) for i in n]


def _attn_init(lanes):
    return (jnp.full((1, lanes), NEG, f32), jnp.zeros((1, lanes), f32), jnp.zeros((B_HEAD_DIM, lanes), f32))


def _attn_finish(carry):
    _, l, acc = carry
    return acc * (1.0 / jnp.maximum(l, 1e-30))


def _compressed_branch(kc, vc_t, q_t, t_row, slope_row, n_ch):
    s = jnp.dot(kc, q_t, preferred_element_type=f32)
    blk_end = (lax.broadcasted_iota(jnp.int32, (n_ch, 1), 0) * CMP_STRIDE + (CMP_BLOCK - 1)).astype(f32)
    dist = t_row - blk_end
    s = s - slope_row * dist
    mask = dist >= 0.0
    s = jnp.where(mask, s, NEG)
    m = jnp.max(s, axis=0, keepdims=True)
    p = jnp.where(mask, jnp.exp(s - m), 0.0)
    p = p * (1.0 / jnp.maximum(jnp.sum(p, axis=0, keepdims=True), 1e-30))
    return jnp.dot(vc_t, p.astype(bf16), preferred_element_type=f32), p


def _compressed_multi(kc, vc_t, q_t, t_row, slope_rows, n_ch):
    n = range(len(kc))
    blk_end = (lax.broadcasted_iota(jnp.int32, (n_ch, 1), 0) * CMP_STRIDE + (CMP_BLOCK - 1)).astype(f32)
    dist = t_row - blk_end
    mask = dist >= 0.0
    s = [jnp.dot(kc[i], q_t[i], preferred_element_type=f32) for i in n]
    s = [jnp.where(mask, s[i] - slope_rows[i] * dist, NEG) for i in n]
    m = [jnp.max(s[i], axis=0, keepdims=True) for i in n]
    p = [jnp.where(mask, jnp.exp(s[i] - m[i]), 0.0) for i in n]
    inv = [1.0 / jnp.maximum(jnp.sum(p[i], axis=0, keepdims=True), 1e-30) for i in n]
    p = [p[i] * inv[i] for i in n]
    o = [jnp.dot(vc_t[i], p[i].astype(bf16), preferred_element_type=f32) for i in n]
    return o, p


def _selection_scores(p, tq, tok_row_i, n_rows, n_sel, n_ch):
    psum = p[:, 0:tq]
    for r in range(1, B_GROUP):
        psum = psum + p[:, r * tq:(r + 1) * tq]
    imp = jnp.dot(_overlap_t(n_rows, n_ch), psum, preferred_element_type=f32, precision=HI)
    j = lax.broadcasted_iota(jnp.int32, (n_rows, 1), 0)
    back = tok_row_i // SEL_BLOCK - j
    forced = (j == 0) | ((back >= 0) & (back < N_LOCAL))
    score = jnp.where(forced, FORCED_SCORE, jnp.where(back >= 0, imp, -1.0))
    return jnp.where(j < n_sel, score, -2.0)


def _kv_pool_partials(x, pw0, pw1):
    rows = x.shape[0]
    n = rows // CMP_STRIDE
    pool = (lax.broadcasted_iota(jnp.int32, (n, rows), 1) // CMP_STRIDE
            == lax.broadcasted_iota(jnp.int32, (n, rows), 0)).astype(f32)
    a = jnp.dot(pool, x * pw0, preferred_element_type=f32, precision=HI)
    b = jnp.dot(pool, x * pw1, preferred_element_type=f32, precision=HI)
    return a, b


def _compress_kv(pooled, wck, wcv, kc_ref, vct_ref):
    for g in range(B_KV_HEADS):
        kc_ref[g] = _mm(pooled[:, g * 64:(g + 1) * 64], wck).astype(bf16)
        vc = _mm(pooled[:, B_KV_WIDTH + g * 64:B_KV_WIDTH + (g + 1) * 64], wcv)
        vct_ref[g] = vc.T.astype(bf16)


def _nsa_prep_kernel(q_ref, kvc_ref, kvs_ref, kvw_ref, pw0_ref, pw1_ref,
                     qt_ref, ks_ref, vst_ref, kw_ref, vwt_ref, a_ref, b_ref):
    qt_ref[...] = (q_ref[...] * ATTN_SCALE).T.astype(bf16)
    for src, k_ref, vt_ref in ((kvs_ref, ks_ref, vst_ref), (kvw_ref, kw_ref, vwt_ref)):
        x = src[...]
        for g in range(B_KV_HEADS):
            k_ref[g] = x[:, g * 64:(g + 1) * 64].astype(bf16)
        vt_ref[...] = x[:, B_KV_WIDTH:].T.astype(bf16)
    a, b = _kv_pool_partials(kvc_ref[...], pw0_ref[...], pw1_ref[...])
    a_ref[...] = a
    b_ref[...] = b


def _nsa_prep(u3, pw0, pw1, tt):
    B, T, _ = u3.shape
    n_ch = tt // CMP_STRIDE
    ublk = lambda w, off: pl.BlockSpec((None, tt, w), lambda b, t: (b, t, off // w))
    const = pl.BlockSpec((tt, 2 * B_KV_WIDTH), lambda b, t: (0, 0))
    return pl.pallas_call(
        _nsa_prep_kernel,
        out_shape=(jax.ShapeDtypeStruct((B, B_WIDTH, T), bf16),
                   jax.ShapeDtypeStruct((B, B_KV_HEADS, T, B_HEAD_DIM), bf16),
                   jax.ShapeDtypeStruct((B, B_KV_WIDTH, T), bf16),
                   jax.ShapeDtypeStruct((B, B_KV_HEADS, T, B_HEAD_DIM), bf16),
                   jax.ShapeDtypeStruct((B, B_KV_WIDTH, T), bf16),
                   jax.ShapeDtypeStruct((B, T // CMP_STRIDE, 2 * B_KV_WIDTH), f32),
                   jax.ShapeDtypeStruct((B, T // CMP_STRIDE, 2 * B_KV_WIDTH), f32)),
        grid=(B, T // tt),
        in_specs=[ublk(B_WIDTH, C_Q), ublk(2 * B_KV_WIDTH, C_KVC), ublk(2 * B_KV_WIDTH, C_KVS),
                  ublk(2 * B_KV_WIDTH, C_KVW), const, const],
        out_specs=(pl.BlockSpec((None, B_WIDTH, tt), lambda b, t: (b, 0, t)),
                   pl.BlockSpec((None, B_KV_HEADS, tt, B_HEAD_DIM), lambda b, t: (b, 0, t, 0)),
                   pl.BlockSpec((None, B_KV_WIDTH, tt), lambda b, t: (b, 0, t)),
                   pl.BlockSpec((None, B_KV_HEADS, tt, B_HEAD_DIM), lambda b, t: (b, 0, t, 0)),
                   pl.BlockSpec((None, B_KV_WIDTH, tt), lambda b, t: (b, 0, t)),
                   pl.BlockSpec((None, n_ch, 2 * B_KV_WIDTH), lambda b, t: (b, t, 0)),
                   pl.BlockSpec((None, n_ch, 2 * B_KV_WIDTH), lambda b, t: (b, t, 0))),
        compiler_params=_cparams(("parallel", "parallel")),
        name="nsa_prep",
    )(u3, u3, u3, u3, pw0, pw1)


def _rank_select(score_ref, n_rows, n_sel, lanes):
    score = score_ref[...].reshape(n_rows, lanes)
    j = lax.broadcasted_iota(jnp.int32, (n_rows, 1), 0)

    def body(i, rank):
        row = score_ref[i]
        beats = jnp.where(row > score, 1.0, jnp.where(row == score, jnp.where(i < j, 1.0, 0.0), 0.0))
        return rank + beats

    rank = lax.fori_loop(0, n_sel, body, jnp.zeros((n_rows, lanes), f32))
    return jnp.where(rank < float(min(N_SELECT, n_sel)), jnp.where(score >= 0.0, 1.0, 0.0), 0.0)


def _nsa_prompt_kernel(qt_ref, ks_ref, vst_ref, kw_ref, vwt_ref, a_ref, b_ref, wck_ref, wcv_ref, gates_ref, gb_ref,
                       o_ref, kc_ref, vct_ref, score_ref, keep_ref, cmp_ref, m_ref, l_ref, acc_ref, ot_ref, *, seq, tq):
    i = pl.program_id(1)
    n_ch = seq // CMP_STRIDE
    n_sel = seq // SEL_BLOCK
    lanes = B_GROUP * tq
    tk = 2 * SEL_BLOCK

    @pl.when(i == 0)
    def _():
        pooled = a_ref[...] + pltpu.roll(b_ref[...], n_ch - 1, 0)
        _compress_kv(pooled, wck_ref[...], wcv_ref[...], kc_ref, vct_ref)

    G = B_KV_HEADS
    gs = range(G)
    tok_i = i * tq + lax.broadcasted_iota(jnp.int32, (1, tq), 1)
    t_row = _tile_heads(tok_i.astype(f32))
    key_off = lax.broadcasted_iota(jnp.int32, (tk, 1), 0).astype(f32)
    slopes = [_slope_row(g, tq) for g in gs]

    def q_t(g):
        return jnp.concatenate([qt_ref[(g * B_GROUP + r) * 64:(g * B_GROUP + r + 1) * 64, :] for r in range(B_GROUP)],
                               axis=1)

    o_cmp, prob = _compressed_multi([kc_ref[g] for g in gs], [vct_ref[g] for g in gs], [q_t(g) for g in gs],
                                    t_row, slopes, n_ch)
    for g in gs:
        cmp_ref[g] = o_cmp[g]
    score = jnp.concatenate([_selection_scores(prob[g], tq, tok_i, n_sel, n_sel, n_ch) for g in gs], axis=1)
    for j in range(n_sel):
        score_ref[j] = score[j:j + 1, :]
    keep = _rank_select(score_ref, n_sel, n_sel, G * tq)
    for j in range(n_sel):
        keep_ref[j] = (keep[j:j + 1, :] - 1.0) * (-NEG)

    for c in range(2 * G):
        m, l, acc = _attn_init(lanes)
        m_ref[c] = m
        l_ref[c] = l
        acc_ref[c] = acc

    def step(kt, with_window):
        off = pl.multiple_of(kt * tk, tk)
        dist = t_row - (key_off + (kt * tk).astype(f32))
        row0 = keep_ref[2 * kt]
        row1 = keep_ref[2 * kt + 1]
        s, biases, masks, v, chains = [], [], [], [], []
        for g in gs:
            s.append(jnp.dot(ks_ref[g, pl.ds(off, tk), :], q_t(g), preferred_element_type=f32))
            biases.append((_tile_heads(row0[:, g * tq:(g + 1) * tq]), _tile_heads(row1[:, g * tq:(g + 1) * tq])))
            masks.append(None)
            v.append(vst_ref[g * 64:(g + 1) * 64, pl.ds(off, tk)])
            chains.append(g)
        if with_window:
            in_window = dist <= float(WINDOW)
            for g in gs:
                s.append(jnp.dot(kw_ref[g, pl.ds(off, tk), :], q_t(g), preferred_element_type=f32))
                biases.append(None)
                masks.append(in_window)
                v.append(vwt_ref[g * 64:(g + 1) * 64, pl.ds(off, tk)])
                chains.append(G + g)
        carries = [(m_ref[c], l_ref[c], acc_ref[c]) for c in chains]
        out = _attn_step_multi(s, dist, [slopes[c % G] for c in chains], biases, masks, carries, v)
        for c, (m, l, acc) in zip(chains, out):
            m_ref[c] = m
            l_ref[c] = l
            acc_ref[c] = acc

    lo = jnp.maximum(i * tq - WINDOW, 0) // tk
    hi = (i * tq) // tk + tq // tk

    def sel_only(kt, carry):
        step(kt, False)
        return carry

    def sel_and_win(kt, carry):
        step(kt, True)
        return carry

    lax.fori_loop(0, lo, sel_only, 0)
    lax.fori_loop(lo, hi, sel_and_win, 0)

    gates_t = jax.nn.sigmoid(gates_ref[...]).T

    def gate_row(branch, g):
        base = branch * B_HEADS + g * B_GROUP
        return jnp.concatenate([gates_t[base + r:base + r + 1, :] for r in range(B_GROUP)], axis=1)

    o_sel = [_attn_finish((m_ref[g], l_ref[g], acc_ref[g])) for g in gs]
    o_win = [_attn_finish((m_ref[G + g], l_ref[G + g], acc_ref[G + g])) for g in gs]
    o_t = [gate_row(0, g) * cmp_ref[g] + gate_row(1, g) * o_sel[g] + gate_row(2, g) * o_win[g] for g in gs]
    for g in gs:
        for r in range(B_GROUP):
            ot_ref[(g * B_GROUP + r) * 64:(g * B_GROUP + r + 1) * 64, :] = o_t[g][:, r * tq:(r + 1) * tq]

    gb = gb_ref[...]
    o_ref[...] = ot_ref[...].T * (gb * jax.nn.sigmoid(gb))


def _nsa_prompt(u3, qt, ks, vst, kw, vwt, a, b, wck, wcv, tq):
    B, T, _ = u3.shape
    n_ch = T // CMP_STRIDE
    n_sel = T // SEL_BLOCK
    lanes = B_GROUP * tq
    per_b3 = lambda s1, s2: pl.BlockSpec((None, s1, s2), lambda b, i: (b, 0, 0))
    per_b4 = pl.BlockSpec((None, B_KV_HEADS, T, B_HEAD_DIM), lambda b, i: (b, 0, 0, 0))
    w_spec = pl.BlockSpec((B_HEAD_DIM, B_HEAD_DIM), lambda b, i: (0, 0))
    return pl.pallas_call(
        functools.partial(_nsa_prompt_kernel, seq=T, tq=tq),
        out_shape=jax.ShapeDtypeStruct((B, T, B_WIDTH), f32),
        grid=(B, T // tq),
        in_specs=[pl.BlockSpec((None, B_WIDTH, tq), lambda b, i: (b, 0, i)),
                  per_b4, per_b3(B_KV_WIDTH, T), per_b4, per_b3(B_KV_WIDTH, T),
                  per_b3(n_ch, 2 * B_KV_WIDTH), per_b3(n_ch, 2 * B_KV_WIDTH), w_spec, w_spec,
                  pl.BlockSpec((None, tq, 128), lambda b, i: (b, i, C_NG // 128)),
                  pl.BlockSpec((None, tq, B_WIDTH), lambda b, i: (b, i, C_GB // B_WIDTH))],
        out_specs=pl.BlockSpec((None, tq, B_WIDTH), lambda b, i: (b, i, 0)),
        scratch_shapes=[pltpu.VMEM((B_KV_HEADS, n_ch, B_HEAD_DIM), bf16),
                        pltpu.VMEM((B_KV_HEADS, B_HEAD_DIM, n_ch), bf16),
                        pltpu.VMEM((n_sel, 1, B_KV_HEADS * tq), f32),
                        pltpu.VMEM((n_sel, 1, B_KV_HEADS * tq), f32),
                        pltpu.VMEM((B_KV_HEADS, B_HEAD_DIM, lanes), f32),
                        pltpu.VMEM((2 * B_KV_HEADS, 1, lanes), f32),
                        pltpu.VMEM((2 * B_KV_HEADS, 1, lanes), f32),
                        pltpu.VMEM((2 * B_KV_HEADS, B_HEAD_DIM, lanes), f32),
                        pltpu.VMEM((B_WIDTH, tq), f32)],
        compiler_params=_cparams(("parallel", "arbitrary")),
        name="nsa_prompt",
    )(qt, ks, vst, kw, vwt, a, b, wck, wcv, u3, u3)


def _pad_rows_transpose(x, rows):
    return jnp.concatenate([x, jnp.zeros((rows - x.shape[0], x.shape[1]), x.dtype)], axis=0).T


def _nsa_sample_kernel(pt_ref, cmp_ref, selp_ref, win_ref, q_ref, news_ref, neww_ref, pw0_ref, pw1_ref, wck_ref,
                       wcv_ref, gates_ref, gb_ref, o_ref,
                       a_s, b_s, kc_ref, vct_ref, qg_ref, score_ref, keep_ref, m_ref, l_ref, acc_ref, ocmp_ref, ot_ref,
                       *, n_pages, t_new, t_pad, tq):
    p = pl.program_id(1)
    past = n_pages * PAGE_SIZE
    n_ch = past // CMP_STRIDE
    n_sel = past // SEL_BLOCK + 1
    n_rows = keep_ref.shape[1]
    lanes = B_GROUP * tq
    tok_i = past + lax.broadcasted_iota(jnp.int32, (1, tq), 1)
    t_row = _tile_heads(tok_i.astype(f32))
    key_off = lax.broadcasted_iota(jnp.int32, (PAGE_SIZE, 1), 0).astype(f32)
    new_off = lax.broadcasted_iota(jnp.int32, (t_pad, 1), 0).astype(f32)

    @pl.when(p == 0)
    def _():
        q_t = (_pad_rows_transpose(q_ref[...], 128) * ATTN_SCALE)[:, :tq]
        for g in range(B_KV_HEADS):
            qg_ref[g] = jnp.concatenate(
                [q_t[(g * B_GROUP + r) * 64:(g * B_GROUP + r + 1) * 64, :] for r in range(B_GROUP)], axis=1).astype(bf16)
            m, l, acc = _attn_init(lanes)
            m_ref[g] = m
            l_ref[g] = l
            acc_ref[g] = acc

    @pl.when(p < n_pages)
    def _():
        a, b = _kv_pool_partials(cmp_ref[...], pw0_ref[...], pw1_ref[...])
        off = pl.multiple_of(p * (PAGE_SIZE // CMP_STRIDE), PAGE_SIZE // CMP_STRIDE)
        a_s[pl.ds(off, PAGE_SIZE // CMP_STRIDE), :] = a
        b_s[pl.ds(off, PAGE_SIZE // CMP_STRIDE), :] = b

    @pl.when(p == n_pages - 1)
    def _():
        pooled = a_s[...] + pltpu.roll(b_s[...], n_ch - 1, 0)
        _compress_kv(pooled, wck_ref[...], wcv_ref[...], kc_ref, vct_ref)
        for g in range(B_KV_HEADS):
            slope_row = _slope_row(g, tq)
            o_cmp, prob = _compressed_branch(kc_ref[g], vct_ref[g], qg_ref[g], t_row, slope_row, n_ch)
            ocmp_ref[g] = o_cmp
            score_ref[...] = _tile_heads(_selection_scores(prob, tq, tok_i, n_rows, n_sel, n_ch)).reshape(n_rows, 1, lanes)
            keep_ref[g] = _rank_select(score_ref, n_rows, n_sel, lanes).reshape(n_rows, 1, lanes)

    def attend(g, k, v, dist, masks, v_rows=False):
        s = jnp.dot(k, qg_ref[g], preferred_element_type=f32)
        m, l, acc = _attn_step(s, dist, _slope_row(g, tq), masks, (m_ref[g], l_ref[g], acc_ref[g]), v, v_rows)
        m_ref[g] = m
        l_ref[g] = l
        acc_ref[g] = acc

    @pl.when(p >= n_pages)
    def _():
        pp = p - n_pages
        x = selp_ref[...]
        dist = t_row - (key_off + (pp * PAGE_SIZE).astype(f32))
        for g in range(B_KV_HEADS):
            keep = jnp.concatenate([jnp.broadcast_to(keep_ref[g, 2 * pp], (SEL_BLOCK, lanes)),
                                    jnp.broadcast_to(keep_ref[g, 2 * pp + 1], (SEL_BLOCK, lanes))], axis=0) > 0.5
            attend(g, x[:, g * 64:(g + 1) * 64].astype(bf16),
                   x[:, B_KV_WIDTH + g * 64:B_KV_WIDTH + (g + 1) * 64].T.astype(bf16), dist, (keep,))

    @pl.when(p == 2 * n_pages - 1)
    def _():
        gates_t = _pad_rows_transpose(jax.nn.sigmoid(gates_ref[...]), 128)
        ot_ref[...] = jnp.zeros(ot_ref.shape, f32)
        xs = news_ref[...]
        xw = neww_ref[...]
        new_dist = t_row - (new_off + float(past))
        n_win = win_ref.shape[0]
        for g in range(B_KV_HEADS):
            ksl = slice(g * 64, (g + 1) * 64)
            vsl = slice(B_KV_WIDTH + g * 64, B_KV_WIDTH + (g + 1) * 64)
            keep = jnp.broadcast_to(keep_ref[g, n_sel - 1], (t_pad, lanes)) > 0.5
            attend(g, xs[:, ksl].astype(bf16), xs[:, vsl], new_dist, (keep,), v_rows=True)
            o_sel = _attn_finish((m_ref[g], l_ref[g], acc_ref[g]))

            m, l, acc = _attn_init(lanes)
            m_ref[g] = m
            l_ref[g] = l
            acc_ref[g] = acc
            for wt in range(n_win // PAGE_SIZE):
                x = win_ref[wt * PAGE_SIZE:(wt + 1) * PAGE_SIZE, :]
                dist = t_row - (key_off + float(past - n_win + wt * PAGE_SIZE))
                attend(g, x[:, ksl].astype(bf16), x[:, vsl].T.astype(bf16), dist, (dist <= float(WINDOW),))
            attend(g, xw[:, ksl].astype(bf16), xw[:, vsl], new_dist, (new_dist <= float(WINDOW),), v_rows=True)
            o_win = _attn_finish((m_ref[g], l_ref[g], acc_ref[g]))

            def gate_row(branch):
                base = branch * B_HEADS + g * B_GROUP
                return jnp.concatenate([gates_t[base + r:base + r + 1, :tq] for r in range(B_GROUP)], axis=1)

            o_t = gate_row(0) * ocmp_ref[g] + gate_row(1) * o_sel + gate_row(2) * o_win
            for r in range(B_GROUP):
                ot_ref[(g * B_GROUP + r) * 64:(g * B_GROUP + r + 1) * 64, 0:tq] = o_t[:, r * tq:(r + 1) * tq]
        gb = gb_ref[...]
        o_ref[...] = ot_ref[...].T[:t_pad, :] * (gb * jax.nn.sigmoid(gb))


def _nsa_sample(u3, cache_cmp, cache_sel, cache_win, page_table, pw0, pw1, wck, wcv, *, t_new, tq):
    DB, t_pad, _ = u3.shape
    n_pages = page_table.shape[1]
    assert t_new < CMP_STRIDE and t_new <= t_pad <= tq and cache_win.shape[1] % PAGE_SIZE == 0
    past = n_pages * PAGE_SIZE
    n_ch = past // CMP_STRIDE
    n_sel = past // SEL_BLOCK + 1
    n_rows = -(-n_sel // 8) * 8
    lanes = B_GROUP * tq
    kvw = 2 * B_KV_WIDTH
    ublk = lambda w, off: pl.BlockSpec((None, t_pad, w), lambda b, p, pt: (b, 0, off // w))
    const = lambda s1, s2: pl.BlockSpec((s1, s2), lambda b, p, pt: (0, 0))
    grid_spec = pltpu.PrefetchScalarGridSpec(
        num_scalar_prefetch=1,
        grid=(DB, 2 * n_pages),
        in_specs=[
            pl.BlockSpec((None, PAGE_SIZE, kvw), lambda b, p, pt: (pt[b, jnp.minimum(p, n_pages - 1)], 0, 0)),
            pl.BlockSpec((None, PAGE_SIZE, kvw), lambda b, p, pt: (pt[b, jnp.maximum(p - n_pages, 0)], 0, 0)),
            pl.BlockSpec((None, cache_win.shape[1], kvw), lambda b, p, pt: (b, 0, 0)),
            ublk(B_WIDTH, C_Q), ublk(kvw, C_KVS), ublk(kvw, C_KVW),
            const(PAGE_SIZE, kvw), const(PAGE_SIZE, kvw), const(B_HEAD_DIM, B_HEAD_DIM), const(B_HEAD_DIM, B_HEAD_DIM),
            ublk(128, C_NG), ublk(B_WIDTH, C_GB)],
        out_specs=pl.BlockSpec((None, t_pad, B_WIDTH), lambda b, p, pt: (b, 0, 0)),
        scratch_shapes=[pltpu.VMEM((n_ch, kvw), f32), pltpu.VMEM((n_ch, kvw), f32),
                        pltpu.VMEM((B_KV_HEADS, n_ch, B_HEAD_DIM), bf16),
                        pltpu.VMEM((B_KV_HEADS, B_HEAD_DIM, n_ch), bf16),
                        pltpu.VMEM((B_KV_HEADS, B_HEAD_DIM, lanes), bf16),
                        pltpu.VMEM((n_rows, 1, lanes), f32),
                        pltpu.VMEM((B_KV_HEADS, n_rows, 1, lanes), f32),
                        pltpu.VMEM((B_KV_HEADS, 1, lanes), f32), pltpu.VMEM((B_KV_HEADS, 1, lanes), f32),
                        pltpu.VMEM((B_KV_HEADS, B_HEAD_DIM, lanes), f32),
                        pltpu.VMEM((B_KV_HEADS, B_HEAD_DIM, lanes), f32),
                        pltpu.VMEM((B_WIDTH, 128), f32)])
    return pl.pallas_call(
        functools.partial(_nsa_sample_kernel, n_pages=n_pages, t_new=t_new, t_pad=t_pad, tq=tq),
        out_shape=jax.ShapeDtypeStruct((DB, t_pad, B_WIDTH), f32),
        grid_spec=grid_spec,
        compiler_params=_cparams(("parallel", "arbitrary")),
        name="nsa_sample",
    )(page_table, cache_cmp, cache_sel, cache_win, u3, u3, u3, pw0, pw1, wck, wcv, u3, u3)


def _pos_weight_tiles(pos_k, pos_v, rows):
    def half(lo):
        t = jnp.concatenate([jnp.tile(pos_k[lo:lo + CMP_STRIDE], (1, B_KV_HEADS)),
                             jnp.tile(pos_v[lo:lo + CMP_STRIDE], (1, B_KV_HEADS))], axis=1)
        return jnp.tile(t, (rows // CMP_STRIDE, 1))
    return half(0), half(CMP_STRIDE)


def _a_cols(u_rows):
    return jnp.concatenate([u_rows[..., :4 * A_WIDTH], u_rows[..., C_LR:C_LR + DECAY_RANK + ICLR_RANK]], axis=-1)


def kernel(x_prompt, x_sample, cache_cmp_kv, cache_sel_kv, cache_win_kv, state_rwkv, state_shift, page_table, norm_in,
           w_in, mu_shift, w0, w_up, a0, a_up, k_k, k_a, r_k, gn_w, gn_b, cmp_pos_k, cmp_pos_v, w_cmp_k, w_cmp_v,
           w_pa, w_pb, w_o, norm_out):
    assert w_in.shape[0] == 1, "one layer"
    B, T, _ = x_prompt.shape
    DB, TS, _ = x_sample.shape
    H, N = A_HEADS, A_HEAD_DIM
    kvs = (2, B_KV_HEADS, B_HEAD_DIM)
    p = dict(mu_shift=mu_shift[0], w0=w0[0], w_up=w_up[0], a0=a0[0], a_up=a_up[0], k_k=k_k[0], k_a=k_a[0],
             r_k=r_k[0].reshape(-1), gn_w=gn_w[0], gn_b=gn_b[0])
    w_packed = _pack_w_in(w_in[0])
    wpa, wpb, wo = w_pa[0].astype(bf16), w_pb[0].astype(bf16), w_o[0].astype(bf16)

    xp = x_prompt.reshape(B * T, D_MODEL)
    up = _proj_in(xp, norm_in[0], w_packed)
    up3 = up.reshape(B, T, U_COLS)
    oa_p, s_p = _rwkv(up3, jnp.zeros((B, 1, 4 * A_WIDTH), f32), jnp.zeros((B, 1, 128), f32),
                      jnp.zeros((B, H, N, N), f32), p, chunk=64, t_valid=64)
    tt = 512
    pw0, pw1 = _pos_weight_tiles(cmp_pos_k[0], cmp_pos_v[0], tt)
    qt, ks, vst, kw, vwt, pa, pb = _nsa_prep(up3, pw0, pw1, tt)
    ob_p = _nsa_prompt(up3, qt, ks, vst, kw, vwt, pa, pb, w_cmp_k[0], w_cmp_v[0], 128)
    y_p = _merge(xp, oa_p.reshape(B * T, A_WIDTH), ob_p.reshape(B * T, B_WIDTH), up, wpa, wpb, wo, norm_out)

    t_pad = 8
    xs = x_sample.reshape(DB * TS, D_MODEL)
    us = _proj_in(xs, norm_in[0], w_packed)
    us3 = us.reshape(DB, TS, U_COLS)
    us3p = jnp.pad(us3, ((0, 0), (0, t_pad - TS), (0, 0)))
    shift0 = state_shift[0]
    oa_s, s_s = _rwkv(us3p, shift0[:, None, :4 * A_WIDTH], shift0[:, None, 4 * A_WIDTH:], state_rwkv[0], p,
                      chunk=t_pad, t_valid=TS)
    n_pool = cache_cmp_kv.shape[1]
    win = cache_win_kv[0]
    ob_s = _nsa_sample(us3p, cache_cmp_kv[0].reshape(n_pool, PAGE_SIZE, 2 * B_KV_WIDTH),
                       cache_sel_kv[0].reshape(n_pool, PAGE_SIZE, 2 * B_KV_WIDTH),
                       win.reshape(DB, win.shape[1], 2 * B_KV_WIDTH), page_table,
                       pw0[:PAGE_SIZE], pw1[:PAGE_SIZE], w_cmp_k[0], w_cmp_v[0], t_new=TS, tq=32)
    y_s = _merge(xs, oa_s[:, :TS].reshape(DB * TS, A_WIDTH), ob_s[:, :TS].reshape(DB * TS, B_WIDTH), us, wpa, wpb, wo,
                 norm_out)

    def kv_out(u3_, col, lead, t):
        return u3_[..., col:col + 2 * B_KV_WIDTH].reshape((1, lead, t) + kvs)

    wk = min(WINDOW, T)
    new_w_s = kv_out(us3, C_KVW, DB, TS)[0]
    s_win = jnp.concatenate([win, new_w_s], axis=1)[:, TS:][None]
    return (y_p.reshape(B, T, D_MODEL), y_s.reshape(DB, TS, D_MODEL),
            kv_out(up3, C_KVC, B, T), kv_out(up3, C_KVS, B, T), kv_out(up3[:, T - wk:], C_KVW, B, wk),
            s_p[None], _a_cols(up3[:, T - 1])[None],
            kv_out(us3, C_KVC, DB, TS), kv_out(us3, C_KVS, DB, TS), s_win,
            s_s[None], _a_cols(us3[:, TS - 1])[None])
```

```python
import functools
import math

import jax
import jax.numpy as jnp
from jax import lax
from jax.experimental import pallas as pl
from jax.experimental.pallas import tpu as pltpu

f32 = jnp.float32
bf16 = jnp.bfloat16

D_MODEL = 2048
PAGE_SIZE = 128
A_HEADS = 16
A_HEAD_DIM = 64
A_WIDTH = A_HEADS * A_HEAD_DIM
DECAY_RANK = 64
ICLR_RANK = 64
A_COLS = 4 * A_WIDTH + DECAY_RANK + ICLR_RANK
GN_EPS = 64e-5
B_HEADS = 16
B_KV_HEADS = 4
B_GROUP = B_HEADS // B_KV_HEADS
B_HEAD_DIM = 64
B_WIDTH = B_HEADS * B_HEAD_DIM
B_KV_WIDTH = B_KV_HEADS * B_HEAD_DIM
B_COLS = 2 * B_WIDTH + 6 * B_KV_WIDTH + 3 * B_HEADS
CMP_BLOCK = 32
CMP_STRIDE = 16
SEL_BLOCK = 64
N_SELECT = 16
N_LOCAL = 2
WINDOW = 512
FORCED_SCORE = 1e4
ATTN_SCALE = B_HEAD_DIM ** -0.5
RMS_EPS = 1e-6
NEG = -1e30

C_RKVG = 0
C_Q = 4096
C_GB = 5120
C_GA_M = 6144
C_GB_M = 8192
C_KVC = 10240
C_KVS = 10752
C_KVW = 11264
C_LR = 11776
C_NG = 11904
U_COLS = 12288

VMEM_LIMIT = 56 * 1024 * 1024
HI = lax.Precision.HIGHEST


def _cparams(sem):
    return pltpu.CompilerParams(dimension_semantics=sem, vmem_limit_bytes=VMEM_LIMIT)


def _pack_w_in(w):
    a, b, m = w[:, :A_COLS], w[:, A_COLS:A_COLS + B_COLS], w[:, A_COLS + B_COLS:]
    z = jnp.zeros((w.shape[0], U_COLS - C_NG - 3 * B_HEADS), w.dtype)
    return jnp.concatenate(
        [a[:, :4 * A_WIDTH], b[:, :2 * B_WIDTH], m, b[:, 2 * B_WIDTH:2 * B_WIDTH + 6 * B_KV_WIDTH],
         a[:, 4 * A_WIDTH:], b[:, 2 * B_WIDTH + 6 * B_KV_WIDTH:], z], axis=1).astype(bf16)


def _proj_in_kernel(x_ref, g_ref, w_ref, o_ref, xn_ref):
    @pl.when(pl.program_id(1) == 0)
    def _():
        x = x_ref[...]
        ms = jnp.mean(x * x, axis=-1, keepdims=True)
        xn_ref[...] = (x * lax.rsqrt(ms + RMS_EPS) * g_ref[...]).astype(bf16)

    o_ref[...] = jnp.dot(xn_ref[...], w_ref[...], preferred_element_type=f32)


def _proj_in(x2d, norm_g, w_packed):
    m = x2d.shape[0]
    tm = min(1024, m)
    tn = 1024
    return pl.pallas_call(
        _proj_in_kernel,
        out_shape=jax.ShapeDtypeStruct((m, U_COLS), f32),
        grid=(m // tm, U_COLS // tn),
        in_specs=[pl.BlockSpec((tm, D_MODEL), lambda i, j: (i, 0)),
                  pl.BlockSpec((1, D_MODEL), lambda i, j: (0, 0)),
                  pl.BlockSpec((D_MODEL, tn), lambda i, j: (0, j))],
        out_specs=pl.BlockSpec((tm, tn), lambda i, j: (i, j)),
        scratch_shapes=[pltpu.VMEM((tm, D_MODEL), bf16)],
        compiler_params=_cparams(("parallel", "arbitrary")),
        name="proj_in",
    )(x2d, norm_g.reshape(1, D_MODEL), w_packed)


def _mm(a, b):
    return jnp.dot(a.astype(bf16), b.astype(bf16), preferred_element_type=f32)


def _mm_nt(a, b):
    return lax.dot_general(a.astype(bf16), b.astype(bf16), (((1,), (1,)), ((), ())), preferred_element_type=f32)


def _mm_tn(a, b):
    return lax.dot_general(a.astype(bf16), b.astype(bf16), (((0,), (0,)), ((), ())), preferred_element_type=f32)


def _rwkv_kernel(um_ref, ulr_ref, pm_ref, plr_ref, s0_ref, mum_ref, mulr_ref, w0_ref, wup_ref, a0_ref,
                 aup_ref, kk_ref, ka_ref, rk_ref, gnw_ref, gnb_ref, o_ref, sout_ref,
                 s_ref, prevm_ref, prevlr_ref, *, chunk, t_valid):
    C = chunk
    H, N = A_HEADS, A_HEAD_DIM
    c = pl.program_id(1)

    @pl.when(c == 0)
    def _():
        s_ref[...] = s0_ref[...]
        prevm_ref[...] = pm_ref[...]
        prevlr_ref[...] = plr_ref[...]

    um = um_ref[...]
    ulr = ulr_ref[...]
    row = lax.broadcasted_iota(jnp.int32, (C, 1), 0)

    def shifted(u, prev):
        return jnp.where(row == 0, prev, pltpu.roll(u, 1, 0))

    usm = um + mum_ref[...] * (shifted(um, prevm_ref[...]) - um)
    uslr = ulr + mulr_ref[...] * (shifted(ulr, prevlr_ref[...]) - ulr)
    prevm_ref[...] = um[C - 1:C, :]
    prevlr_ref[...] = ulr[C - 1:C, :]

    wd = uslr[:, :DECAY_RANK]
    ad = uslr[:, DECAY_RANK:]
    zw = w0_ref[...] + jnp.dot(jnp.tanh(wd), wup_ref[...], preferred_element_type=f32, precision=HI)
    softplus_neg = jnp.maximum(-zw, 0.0) + jnp.log(1.0 + jnp.exp(-jnp.abs(zw)))
    logw = -jnp.exp(-softplus_neg - 0.5)
    a_all = jax.nn.sigmoid(a0_ref[...] + jnp.dot(ad, aup_ref[...], preferred_element_type=f32, precision=HI))
    if t_valid < C:
        valid = (row < t_valid).astype(f32)
        logw = logw * valid
    ti = lax.broadcasted_iota(jnp.int32, (C, C), 0)
    si = lax.broadcasted_iota(jnp.int32, (C, C), 1)
    tril_incl = ti >= si
    tril_strict = ti > si
    cum = jnp.dot(tril_incl.astype(f32), logw, preferred_element_type=f32, precision=HI)
    e_pos = jnp.exp(cum)
    e_neg = jnp.exp(-cum)
    e_prev = jnp.exp(cum - logw)
    e_last = e_pos[C - 1:C, :]
    eye = (ti == si).astype(f32)
    n_sq = int(math.log2(C)) - 1

    hs = range(H)
    sls = [slice(h * N, (h + 1) * N) for h in hs]
    sec = lambda i: [usm[:, i * A_WIDTH + h * N:i * A_WIDTH + (h + 1) * N] for h in hs]
    r, k, v, g = sec(0), sec(1), sec(2), sec(3)
    a = [a_all[:, sl] for sl in sls]
    kk = [k[h] * kk_ref[:, sls[h]] for h in hs]
    ssq = [jnp.sum(kk[h] * kk[h], axis=-1, keepdims=True) for h in hs]
    kk = [kk[h] * lax.rsqrt(jnp.maximum(ssq[h], 1e-24)) for h in hs]
    k2 = [k[h] * (1.0 + (a[h] - 1.0) * ka_ref[:, sls[h]]) for h in hs]
    bonus = [jnp.sum(r[h] * k2[h] * rk_ref[:, sls[h]], axis=-1, keepdims=True) for h in hs]
    kn = [k2[h] * e_neg[:, sls[h]] for h in hs]
    bn = [kk[h] * a[h] * e_neg[:, sls[h]] for h in hs]
    if t_valid < C:
        kn = [x * valid for x in kn]
        bn = [x * valid for x in bn]
    lhs = [jnp.concatenate([kk[h] * e_prev[:, sls[h]], r[h] * e_pos[:, sls[h]]], axis=0) for h in hs]
    s0 = [s_ref[h] for h in hs]
    qk_b = [_mm_nt(lhs[h], bn[h]) for h in hs]
    qk_k = [_mm_nt(lhs[h], kn[h]) for h in hs]
    w0s = [_mm_nt(lhs[h], s0[h]) for h in hs]
    pw = [jnp.where(tril_strict, -qk_b[h][:C], 0.0) for h in hs]
    tinv = [eye + pw[h] for h in hs]
    for _ in range(n_sq):
        pw = [_mm(pw[h], pw[h]) for h in hs]
        tinv = [tinv[h] + _mm(tinv[h], pw[h]) for h in hs]
    rhs_u = [w0s[h][:C] + _mm(jnp.where(tril_strict, qk_k[h][:C], 0.0), v[h]) for h in hs]
    u = [_mm(tinv[h], rhs_u[h]) for h in hs]
    o = [w0s[h][C:] + _mm(jnp.where(tril_incl, qk_k[h][C:], 0.0), v[h])
         - _mm(jnp.where(tril_incl, qk_b[h][C:], 0.0), u[h]) for h in hs]
    for h in hs:
        s_ref[h] = (s0[h] + _mm_tn(v[h], kn[h]) - _mm_tn(u[h], bn[h])) * e_last[:, sls[h]]
    mean = [jnp.mean(o[h], axis=-1, keepdims=True) for h in hs]
    d = [o[h] - mean[h] for h in hs]
    var = [jnp.mean(d[h] * d[h], axis=-1, keepdims=True) for h in hs]
    for h in hs:
        on = d[h] * lax.rsqrt(var[h] + GN_EPS) * gnw_ref[:, sls[h]] + gnb_ref[:, sls[h]] + bonus[h] * v[h]
        o_ref[:, sls[h]] = on * (g[h] * jax.nn.sigmoid(g[h]))

    @pl.when(c == pl.num_programs(1) - 1)
    def _():
        sout_ref[...] = s_ref[...]


def _rwkv(u3, prev_m, prev_lr, s0, p, *, chunk, t_valid):
    B, T, _ = u3.shape
    C = chunk
    H, N = A_HEADS, A_HEAD_DIM
    row = lambda v: v.reshape(1, -1)
    vec = lambda n: pl.BlockSpec((1, n), lambda b, c: (0, 0))
    in_specs = [
        pl.BlockSpec((None, C, 4 * A_WIDTH), lambda b, c: (b, c, 0)),
        pl.BlockSpec((None, C, 128), lambda b, c: (b, c, C_LR // 128)),
        pl.BlockSpec((None, 1, 4 * A_WIDTH), lambda b, c: (b, 0, 0)),
        pl.BlockSpec((None, 1, 128), lambda b, c: (b, 0, 0)),
        pl.BlockSpec((None, H, N, N), lambda b, c: (b, 0, 0, 0)),
        vec(4 * A_WIDTH), vec(128), vec(A_WIDTH),
        pl.BlockSpec((DECAY_RANK, A_WIDTH), lambda b, c: (0, 0)),
        vec(A_WIDTH),
        pl.BlockSpec((ICLR_RANK, A_WIDTH), lambda b, c: (0, 0)),
        vec(A_WIDTH), vec(A_WIDTH), vec(A_WIDTH), vec(A_WIDTH), vec(A_WIDTH),
    ]
    return pl.pallas_call(
        functools.partial(_rwkv_kernel, chunk=C, t_valid=t_valid),
        out_shape=(jax.ShapeDtypeStruct((B, T, A_WIDTH), f32), jax.ShapeDtypeStruct((B, H, N, N), f32)),
        grid=(B, T // C),
        in_specs=in_specs,
        out_specs=(pl.BlockSpec((None, C, A_WIDTH), lambda b, c: (b, c, 0)),
                   pl.BlockSpec((None, H, N, N), lambda b, c: (b, 0, 0, 0))),
        scratch_shapes=[pltpu.VMEM((H, N, N), f32), pltpu.VMEM((1, 4 * A_WIDTH), f32), pltpu.VMEM((1, 128), f32)],
        compiler_params=_cparams(("parallel", "arbitrary")),
        name="rwkv7",
    )(u3, u3, prev_m, prev_lr, s0, row(p["mu_shift"][:4 * A_WIDTH]), row(p["mu_shift"][4 * A_WIDTH:]),
      row(p["w0"]), p["w_up"], row(p["a0"]), p["a_up"], row(p["k_k"]), row(p["k_a"]), row(p["r_k"]),
      row(p["gn_w"]), row(p["gn_b"]))


def _merge_kernel(x_ref, oa_ref, ob_ref, ga_ref, gb_ref, wpa_ref, wpb_ref, wo_ref, gout_ref, y_ref):
    pa = jnp.dot(oa_ref[...].astype(bf16), wpa_ref[...], preferred_element_type=f32)
    pb = jnp.dot(ob_ref[...].astype(bf16), wpb_ref[...], preferred_element_type=f32)
    merged = jax.nn.sigmoid(ga_ref[...]) * pa + jax.nn.sigmoid(gb_ref[...]) * pb
    h = x_ref[...] + jnp.dot(merged.astype(bf16), wo_ref[...], preferred_element_type=f32)
    ms = jnp.mean(h * h, axis=-1, keepdims=True)
    y_ref[...] = h * lax.rsqrt(ms + RMS_EPS) * gout_ref[...]


def _merge(x2d, o_a, o_b, u2d, w_pa, w_pb, w_o, norm_out):
    m = x2d.shape[0]
    tm = min(256, m)
    const = lambda shape: pl.BlockSpec(shape, lambda i: (0, 0), pipeline_mode=pl.Buffered(1))
    return pl.pallas_call(
        _merge_kernel,
        out_shape=jax.ShapeDtypeStruct((m, D_MODEL), f32),
        grid=(m // tm,),
        in_specs=[pl.BlockSpec((tm, D_MODEL), lambda i: (i, 0)),
                  pl.BlockSpec((tm, A_WIDTH), lambda i: (i, 0)),
                  pl.BlockSpec((tm, B_WIDTH), lambda i: (i, 0)),
                  pl.BlockSpec((tm, D_MODEL), lambda i: (i, C_GA_M // D_MODEL)),
                  pl.BlockSpec((tm, D_MODEL), lambda i: (i, C_GB_M // D_MODEL)),
                  const((A_WIDTH, D_MODEL)), const((B_WIDTH, D_MODEL)), const((D_MODEL, D_MODEL)),
                  const((1, D_MODEL))],
        out_specs=pl.BlockSpec((tm, D_MODEL), lambda i: (i, 0)),
        compiler_params=_cparams(("parallel",)),
        name="merge_out",
    )(x2d, o_a, o_b, u2d, u2d, w_pa, w_pb, w_o, norm_out.reshape(1, D_MODEL))


def _alibi_slope(head):
    return 2.0 ** (-8.0 * (head + 1) / B_HEADS)


def _slope_row(g, tq):
    lane_head = lax.broadcasted_iota(jnp.int32, (1, B_GROUP * tq), 1) // tq
    out = jnp.zeros((1, B_GROUP * tq), f32)
    for r in range(B_GROUP):
        out = jnp.where(lane_head == r, _alibi_slope(g * B_GROUP + r), out)
    return out


def _tile_heads(row):
    return jnp.concatenate([row] * B_GROUP, axis=1)


def _overlap_t(n_sel_rows, n_ch):
    j = lax.broadcasted_iota(jnp.int32, (n_sel_rows, n_ch), 0) * SEL_BLOCK
    n = lax.broadcasted_iota(jnp.int32, (n_sel_rows, n_ch), 1) * CMP_STRIDE
    ov = jnp.minimum(n + CMP_BLOCK, j + SEL_BLOCK) - jnp.maximum(n, j)
    return jnp.maximum(ov, 0).astype(f32) * (1.0 / CMP_BLOCK)


def _attn_step(s, dist, slope_row, extra_masks, carry, v_t, v_rows=False):
    m, l, acc = carry
    s = s - slope_row * dist
    s = jnp.where(dist >= 0.0, s, NEG)
    for keep in extra_masks:
        s = jnp.where(keep, s, NEG)
    m_new = jnp.maximum(m, jnp.max(s, axis=0, keepdims=True))
    alpha = jnp.exp(m - m_new)
    p = jnp.where(s > 0.5 * NEG, jnp.exp(s - m_new), 0.0)
    l = l * alpha + jnp.sum(p, axis=0, keepdims=True)
    if v_rows:
        pv = _mm_tn(v_t, p)
    else:
        pv = jnp.dot(v_t, p.astype(bf16), preferred_element_type=f32)
    return m_new, l, acc * alpha + pv


def _attn_step_multi(s, dist, slope_rows, biases, masks, carries, v_t):
    n = range(len(s))
    causal = dist >= 0.0
    s = [s[i] - slope_rows[i] * dist for i in n]
    s = [jnp.where(causal, s[i], NEG) for i in n]
    s = [s[i] if masks[i] is None else jnp.where(masks[i], s[i], NEG) for i in n]
    s = [s[i] if biases[i] is None else
         jnp.concatenate([s[i][:SEL_BLOCK] + biases[i][0], s[i][SEL_BLOCK:] + biases[i][1]], axis=0) for i in n]
    m_new = [jnp.maximum(carries[i][0], jnp.max(s[i], axis=0, keepdims=True)) for i in n]
    alpha = [jnp.exp(carries[i][0] - m_new[i]) for i in n]
    p = [jnp.where(s[i] > 0.5 * NEG, jnp.exp(s[i] - m_new[i]), 0.0) for i in n]
    l = [carries[i][1] * alpha[i] + jnp.sum(p[i], axis=0, keepdims=True) for i in n]
    pv = [jnp.dot(v_t[i], p[i].astype(bf16), preferred_element_type=f32) for i in n]
    return [(m_new[i], l[i], carries[i][2] * alpha[i] + pv[i]) for i in n]


def _attn_init(lanes):
    return (jnp.full((1, lanes), NEG, f32), jnp.zeros((1, lanes), f32), jnp.zeros((B_HEAD_DIM, lanes), f32))


def _attn_finish(carry):
    _, l, acc = carry
    return acc * (1.0 / jnp.maximum(l, 1e-30))


def _compressed_branch(kc, vc_t, q_t, t_row, slope_row, n_ch):
    s = jnp.dot(kc, q_t, preferred_element_type=f32)
    blk_end = (lax.broadcasted_iota(jnp.int32, (n_ch, 1), 0) * CMP_STRIDE + (CMP_BLOCK - 1)).astype(f32)
    dist = t_row - blk_end
    s = s - slope_row * dist
    mask = dist >= 0.0
    s = jnp.where(mask, s, NEG)
    m = jnp.max(s, axis=0, keepdims=True)
    p = jnp.where(mask, jnp.exp(s - m), 0.0)
    p = p * (1.0 / jnp.maximum(jnp.sum(p, axis=0, keepdims=True), 1e-30))
    return jnp.dot(vc_t, p.astype(bf16), preferred_element_type=f32), p


def _compressed_multi(kc, vc_t, q_t, t_row, slope_rows, n_ch):
    n = range(len(kc))
    blk_end = (lax.broadcasted_iota(jnp.int32, (n_ch, 1), 0) * CMP_STRIDE + (CMP_BLOCK - 1)).astype(f32)
    dist = t_row - blk_end
    mask = dist >= 0.0
    s = [jnp.dot(kc[i], q_t[i], preferred_element_type=f32) for i in n]
    s = [jnp.where(mask, s[i] - slope_rows[i] * dist, NEG) for i in n]
    m = [jnp.max(s[i], axis=0, keepdims=True) for i in n]
    p = [jnp.where(mask, jnp.exp(s[i] - m[i]), 0.0) for i in n]
    inv = [1.0 / jnp.maximum(jnp.sum(p[i], axis=0, keepdims=True), 1e-30) for i in n]
    p = [p[i] * inv[i] for i in n]
    o = [jnp.dot(vc_t[i], p[i].astype(bf16), preferred_element_type=f32) for i in n]
    return o, p


def _selection_scores(p, tq, tok_row_i, n_rows, n_sel, n_ch):
    psum = p[:, 0:tq]
    for r in range(1, B_GROUP):
        psum = psum + p[:, r * tq:(r + 1) * tq]
    imp = jnp.dot(_overlap_t(n_rows, n_ch), psum, preferred_element_type=f32, precision=HI)
    j = lax.broadcasted_iota(jnp.int32, (n_rows, 1), 0)
    back = tok_row_i // SEL_BLOCK - j
    forced = (j == 0) | ((back >= 0) & (back < N_LOCAL))
    score = jnp.where(forced, FORCED_SCORE, jnp.where(back >= 0, imp, -1.0))
    return jnp.where(j < n_sel, score, -2.0)


def _kv_pool_partials(x, pw0, pw1):
    rows = x.shape[0]
    n = rows // CMP_STRIDE
    pool = (lax.broadcasted_iota(jnp.int32, (n, rows), 1) // CMP_STRIDE
            == lax.broadcasted_iota(jnp.int32, (n, rows), 0)).astype(f32)
    a = jnp.dot(pool, x * pw0, preferred_element_type=f32, precision=HI)
    b = jnp.dot(pool, x * pw1, preferred_element_type=f32, precision=HI)
    return a, b


def _compress_kv(pooled, wck, wcv, kc_ref, vct_ref):
    for g in range(B_KV_HEADS):
        kc_ref[g] = _mm(pooled[:, g * 64:(g + 1) * 64], wck).astype(bf16)
        vc = _mm(pooled[:, B_KV_WIDTH + g * 64:B_KV_WIDTH + (g + 1) * 64], wcv)
        vct_ref[g] = vc.T.astype(bf16)


def _nsa_prep_kernel(q_ref, kvc_ref, kvs_ref, kvw_ref, pw0_ref, pw1_ref,
                     qt_ref, ks_ref, vst_ref, kw_ref, vwt_ref, a_ref, b_ref):
    qt_ref[...] = (q_ref[...] * ATTN_SCALE).T.astype(bf16)
    for src, k_ref, vt_ref in ((kvs_ref, ks_ref, vst_ref), (kvw_ref, kw_ref, vwt_ref)):
        x = src[...]
        for g in range(B_KV_HEADS):
            k_ref[g] = x[:, g * 64:(g + 1) * 64].astype(bf16)
        vt_ref[...] = x[:, B_KV_WIDTH:].T.astype(bf16)
    a, b = _kv_pool_partials(kvc_ref[...], pw0_ref[...], pw1_ref[...])
    a_ref[...] = a
    b_ref[...] = b


def _nsa_prep(u3, pw0, pw1, tt):
    B, T, _ = u3.shape
    n_ch = tt // CMP_STRIDE
    ublk = lambda w, off: pl.BlockSpec((None, tt, w), lambda b, t: (b, t, off // w))
    const = pl.BlockSpec((tt, 2 * B_KV_WIDTH), lambda b, t: (0, 0))
    return pl.pallas_call(
        _nsa_prep_kernel,
        out_shape=(jax.ShapeDtypeStruct((B, B_WIDTH, T), bf16),
                   jax.ShapeDtypeStruct((B, B_KV_HEADS, T, B_HEAD_DIM), bf16),
                   jax.ShapeDtypeStruct((B, B_KV_WIDTH, T), bf16),
                   jax.ShapeDtypeStruct((B, B_KV_HEADS, T, B_HEAD_DIM), bf16),
                   jax.ShapeDtypeStruct((B, B_KV_WIDTH, T), bf16),
                   jax.ShapeDtypeStruct((B, T // CMP_STRIDE, 2 * B_KV_WIDTH), f32),
                   jax.ShapeDtypeStruct((B, T // CMP_STRIDE, 2 * B_KV_WIDTH), f32)),
        grid=(B, T // tt),
        in_specs=[ublk(B_WIDTH, C_Q), ublk(2 * B_KV_WIDTH, C_KVC), ublk(2 * B_KV_WIDTH, C_KVS),
                  ublk(2 * B_KV_WIDTH, C_KVW), const, const],
        out_specs=(pl.BlockSpec((None, B_WIDTH, tt), lambda b, t: (b, 0, t)),
                   pl.BlockSpec((None, B_KV_HEADS, tt, B_HEAD_DIM), lambda b, t: (b, 0, t, 0)),
                   pl.BlockSpec((None, B_KV_WIDTH, tt), lambda b, t: (b, 0, t)),
                   pl.BlockSpec((None, B_KV_HEADS, tt, B_HEAD_DIM), lambda b, t: (b, 0, t, 0)),
                   pl.BlockSpec((None, B_KV_WIDTH, tt), lambda b, t: (b, 0, t)),
                   pl.BlockSpec((None, n_ch, 2 * B_KV_WIDTH), lambda b, t: (b, t, 0)),
                   pl.BlockSpec((None, n_ch, 2 * B_KV_WIDTH), lambda b, t: (b, t, 0))),
        compiler_params=_cparams(("parallel", "parallel")),
        name="nsa_prep",
    )(u3, u3, u3, u3, pw0, pw1)


def _rank_select(score_ref, n_rows, n_sel, lanes):
    score = score_ref[...].reshape(n_rows, lanes)
    j = lax.broadcasted_iota(jnp.int32, (n_rows, 1), 0)

    def body(i, rank):
        row = score_ref[i]
        beats = jnp.where(row > score, 1.0, jnp.where(row == score, jnp.where(i < j, 1.0, 0.0), 0.0))
        return rank + beats

    rank = lax.fori_loop(0, n_sel, body, jnp.zeros((n_rows, lanes), f32))
    return jnp.where(rank < float(min(N_SELECT, n_sel)), jnp.where(score >= 0.0, 1.0, 0.0), 0.0)


def _nsa_prompt_kernel(qt_ref, ks_ref, vst_ref, kw_ref, vwt_ref, a_ref, b_ref, wck_ref, wcv_ref, gates_ref, gb_ref,
                       o_ref, kc_ref, vct_ref, score_ref, keep_ref, cmp_ref, m_ref, l_ref, acc_ref, ot_ref, *, seq, tq):
    i = pl.program_id(1)
    n_ch = seq // CMP_STRIDE
    n_sel = seq // SEL_BLOCK
    lanes = B_GROUP * tq
    tk = 2 * SEL_BLOCK

    @pl.when(i == 0)
    def _():
        pooled = a_ref[...] + pltpu.roll(b_ref[...], n_ch - 1, 0)
        _compress_kv(pooled, wck_ref[...], wcv_ref[...], kc_ref, vct_ref)

    G = B_KV_HEADS
    gs = range(G)
    tok_i = i * tq + lax.broadcasted_iota(jnp.int32, (1, tq), 1)
    t_row = _tile_heads(tok_i.astype(f32))
    key_off = lax.broadcasted_iota(jnp.int32, (tk, 1), 0).astype(f32)
    slopes = [_slope_row(g, tq) for g in gs]

    def q_t(g):
        return jnp.concatenate([qt_ref[(g * B_GROUP + r) * 64:(g * B_GROUP + r + 1) * 64, :] for r in range(B_GROUP)],
                               axis=1)

    o_cmp, prob = _compressed_multi([kc_ref[g] for g in gs], [vct_ref[g] for g in gs], [q_t(g) for g in gs],
                                    t_row, slopes, n_ch)
    for g in gs:
        cmp_ref[g] = o_cmp[g]
    score = jnp.concatenate([_selection_scores(prob[g], tq, tok_i, n_sel, n_sel, n_ch) for g in gs], axis=1)
    for j in range(n_sel):
        score_ref[j] = score[j:j + 1, :]
    keep = _rank_select(score_ref, n_sel, n_sel, G * tq)
    for j in range(n_sel):
        keep_ref[j] = (keep[j:j + 1, :] - 1.0) * (-NEG)

    for c in range(2 * G):
        m, l, acc = _attn_init(lanes)
        m_ref[c] = m
        l_ref[c] = l
        acc_ref[c] = acc

    def step(kt, with_window):
        off = pl.multiple_of(kt * tk, tk)
        dist = t_row - (key_off + (kt * tk).astype(f32))
        row0 = keep_ref[2 * kt]
        row1 = keep_ref[2 * kt + 1]
        s, biases, masks, v, chains = [], [], [], [], []
        for g in gs:
            s.append(jnp.dot(ks_ref[g, pl.ds(off, tk), :], q_t(g), preferred_element_type=f32))
            biases.append((_tile_heads(row0[:, g * tq:(g + 1) * tq]), _tile_heads(row1[:, g * tq:(g + 1) * tq])))
            masks.append(None)
            v.append(vst_ref[g * 64:(g + 1) * 64, pl.ds(off, tk)])
            chains.append(g)
        if with_window:
            in_window = dist <= float(WINDOW)
            for g in gs:
                s.append(jnp.dot(kw_ref[g, pl.ds(off, tk), :], q_t(g), preferred_element_type=f32))
                biases.append(None)
                masks.append(in_window)
                v.append(vwt_ref[g * 64:(g + 1) * 64, pl.ds(off, tk)])
                chains.append(G + g)
        carries = [(m_ref[c], l_ref[c], acc_ref[c]) for c in chains]
        out = _attn_step_multi(s, dist, [slopes[c % G] for c in chains], biases, masks, carries, v)
        for c, (m, l, acc) in zip(chains, out):
            m_ref[c] = m
            l_ref[c] = l
            acc_ref[c] = acc

    lo = jnp.maximum(i * tq - WINDOW, 0) // tk
    hi = (i * tq) // tk + tq // tk

    def sel_only(kt, carry):
        step(kt, False)
        return carry

    def sel_and_win(kt, carry):
        step(kt, True)
        return carry

    lax.fori_loop(0, lo, sel_only, 0)
    lax.fori_loop(lo, hi, sel_and_win, 0)

    gates_t = jax.nn.sigmoid(gates_ref[...]).T

    def gate_row(branch, g):
        base = branch * B_HEADS + g * B_GROUP
        return jnp.concatenate([gates_t[base + r:base + r + 1, :] for r in range(B_GROUP)], axis=1)

    o_sel = [_attn_finish((m_ref[g], l_ref[g], acc_ref[g])) for g in gs]
    o_win = [_attn_finish((m_ref[G + g], l_ref[G + g], acc_ref[G + g])) for g in gs]
    o_t = [gate_row(0, g) * cmp_ref[g] + gate_row(1, g) * o_sel[g] + gate_row(2, g) * o_win[g] for g in gs]
    for g in gs:
        for r in range(B_GROUP):
            ot_ref[(g * B_GROUP + r) * 64:(g * B_GROUP + r + 1) * 64, :] = o_t[g][:, r * tq:(r + 1) * tq]

    gb = gb_ref[...]
    o_ref[...] = ot_ref[...].T * (gb * jax.nn.sigmoid(gb))


def _nsa_prompt(u3, qt, ks, vst, kw, vwt, a, b, wck, wcv, tq):
    B, T, _ = u3.shape
    n_ch = T // CMP_STRIDE
    n_sel = T // SEL_BLOCK
    lanes = B_GROUP * tq
    per_b3 = lambda s1, s2: pl.BlockSpec((None, s1, s2), lambda b, i: (b, 0, 0))
    per_b4 = pl.BlockSpec((None, B_KV_HEADS, T, B_HEAD_DIM), lambda b, i: (b, 0, 0, 0))
    w_spec = pl.BlockSpec((B_HEAD_DIM, B_HEAD_DIM), lambda b, i: (0, 0))
    return pl.pallas_call(
        functools.partial(_nsa_prompt_kernel, seq=T, tq=tq),
        out_shape=jax.ShapeDtypeStruct((B, T, B_WIDTH), f32),
        grid=(B, T // tq),
        in_specs=[pl.BlockSpec((None, B_WIDTH, tq), lambda b, i: (b, 0, i)),
                  per_b4, per_b3(B_KV_WIDTH, T), per_b4, per_b3(B_KV_WIDTH, T),
                  per_b3(n_ch, 2 * B_KV_WIDTH), per_b3(n_ch, 2 * B_KV_WIDTH), w_spec, w_spec,
                  pl.BlockSpec((None, tq, 128), lambda b, i: (b, i, C_NG // 128)),
                  pl.BlockSpec((None, tq, B_WIDTH), lambda b, i: (b, i, C_GB // B_WIDTH))],
        out_specs=pl.BlockSpec((None, tq, B_WIDTH), lambda b, i: (b, i, 0)),
        scratch_shapes=[pltpu.VMEM((B_KV_HEADS, n_ch, B_HEAD_DIM), bf16),
                        pltpu.VMEM((B_KV_HEADS, B_HEAD_DIM, n_ch), bf16),
                        pltpu.VMEM((n_sel, 1, B_KV_HEADS * tq), f32),
                        pltpu.VMEM((n_sel, 1, B_KV_HEADS * tq), f32),
                        pltpu.VMEM((B_KV_HEADS, B_HEAD_DIM, lanes), f32),
                        pltpu.VMEM((2 * B_KV_HEADS, 1, lanes), f32),
                        pltpu.VMEM((2 * B_KV_HEADS, 1, lanes), f32),
                        pltpu.VMEM((2 * B_KV_HEADS, B_HEAD_DIM, lanes), f32),
                        pltpu.VMEM((B_WIDTH, tq), f32)],
        compiler_params=_cparams(("parallel", "arbitrary")),
        name="nsa_prompt",
    )(qt, ks, vst, kw, vwt, a, b, wck, wcv, u3, u3)


def _softmax_rows_update(sc, m, l, acc, pv_fn):
    m_new = jnp.maximum(m, jnp.max(sc, axis=-1, keepdims=True))
    alpha = jnp.exp(m - m_new)
    p = jnp.where(sc > 0.5 * NEG, jnp.exp(sc - m_new), 0.0)
    return m_new, l * alpha + jnp.sum(p, axis=-1, keepdims=True), acc * alpha + pv_fn(p)


def _nsa_sample_kernel(pt_ref, *refs, n_pages, pages_per_step, t_new, t_pad):
    P = pages_per_step
    cmp_pages, sel_pages = refs[:P], refs[P:2 * P]
    (win_ref, q_ref, news_ref, neww_ref, pw0_ref, pw1_ref, pool_ref, wck_ref, wcv_ref, gates_ref, gb_ref,
     o_ref, at_ref, bt_ref, qg_ref, ocmp_ref, keep_ref, m_ref, l_ref, acc_ref) = refs[2 * P:]
    s = pl.program_id(1)
    S = n_pages // P
    G = B_KV_HEADS
    gs = range(G)
    past = n_pages * PAGE_SIZE
    n_ch = past // CMP_STRIDE
    n_sel = past // SEL_BLOCK + 1
    n_j = keep_ref.shape[1]
    R = B_GROUP * t_pad
    row = lax.broadcasted_iota(jnp.int32, (R, 1), 0)
    t_col = (past + row % t_pad).astype(f32)
    slope_col = []
    for g in gs:
        sc_ = jnp.zeros((R, 1), f32)
        for r in range(B_GROUP):
            sc_ = jnp.where(row // t_pad == r, _alibi_slope(g * B_GROUP + r), sc_)
        slope_col.append(sc_)
    ksl = lambda g: slice(g * 64, (g + 1) * 64)
    vsl = lambda g: slice(B_KV_WIDTH + g * 64, B_KV_WIDTH + (g + 1) * 64)

    @pl.when(s < S)
    def _():
        pw0, pw1 = pw0_ref[...], pw1_ref[...]
        xa = jnp.concatenate([(cmp_pages[k][...] * pw0).astype(bf16) for k in range(P)], axis=1)
        xb = jnp.concatenate([(cmp_pages[k][...] * pw1).astype(bf16) for k in range(P)], axis=1)
        n_step = P * PAGE_SIZE // CMP_STRIDE
        off = pl.multiple_of(s * n_step, n_step)
        at_ref[:, pl.ds(off, n_step)] = jnp.dot(xa, pool_ref[...], preferred_element_type=f32)
        bt_ref[:, pl.ds(off, n_step)] = jnp.dot(xb, pool_ref[...], preferred_element_type=f32)

    @pl.when(s == S - 1)
    def _():
        pooled = at_ref[...] + pltpu.roll(bt_ref[...], n_ch - 1, 1)
        q = q_ref[...] * ATTN_SCALE
        qg = [jnp.concatenate([q[:, (g * B_GROUP + r) * 64:(g * B_GROUP + r + 1) * 64] for r in range(B_GROUP)],
                              axis=0).astype(bf16) for g in gs]
        kct = [_mm_tn(wck_ref[...], pooled[ksl(g), :]).astype(bf16) for g in gs]
        vct = [_mm_tn(wcv_ref[...], pooled[vsl(g), :]).astype(bf16) for g in gs]
        blk_end = (lax.broadcasted_iota(jnp.int32, (1, n_ch), 1) * CMP_STRIDE + (CMP_BLOCK - 1)).astype(f32)
        dist = t_col - blk_end
        mask = dist >= 0.0
        sc = [jnp.dot(qg[g], kct[g], preferred_element_type=f32) for g in gs]
        sc = [jnp.where(mask, sc[g] - slope_col[g] * dist, NEG) for g in gs]
        mx = [jnp.max(sc[g], axis=-1, keepdims=True) for g in gs]
        pr = [jnp.where(mask, jnp.exp(sc[g] - mx[g]), 0.0) for g in gs]
        inv = [1.0 / jnp.maximum(jnp.sum(pr[g], axis=-1, keepdims=True), 1e-30) for g in gs]
        pr = [pr[g] * inv[g] for g in gs]
        for g in gs:
            qg_ref[g] = qg[g]
            ocmp_ref[g] = _mm_nt(pr[g], vct[g])
            m_ref[g] = jnp.full((R, 1), NEG, f32)
            l_ref[g] = jnp.zeros((R, 1), f32)
            acc_ref[g] = jnp.zeros((R, B_HEAD_DIM), f32)
        psum = [pr[g][0:t_pad] for g in gs]
        for r in range(1, B_GROUP):
            psum = [psum[g] + pr[g][r * t_pad:(r + 1) * t_pad] for g in gs]
        psum = jnp.concatenate(psum, axis=0)
        n_i = lax.broadcasted_iota(jnp.int32, (n_ch, n_j), 0) * CMP_STRIDE
        j_i = lax.broadcasted_iota(jnp.int32, (n_ch, n_j), 1) * SEL_BLOCK
        overlap = jnp.maximum(jnp.minimum(n_i + CMP_BLOCK, j_i + SEL_BLOCK) - jnp.maximum(n_i, j_i), 0).astype(f32) \
            * (1.0 / CMP_BLOCK)
        imp = jnp.dot(psum, overlap, preferred_element_type=f32, precision=HI)
        j = lax.broadcasted_iota(jnp.int32, (1, n_j), 1)
        tok = past + lax.broadcasted_iota(jnp.int32, (G * t_pad, 1), 0) % t_pad
        back = tok // SEL_BLOCK - j
        forced = (j == 0) | ((back >= 0) & (back < N_LOCAL))
        score = jnp.where(forced, FORCED_SCORE, jnp.where(back >= 0, imp, -1.0))
        score = jnp.where(j < n_sel, score, -2.0)
        rank = jnp.zeros(score.shape, f32)
        for i in range(n_sel):
            col = score[:, i:i + 1]
            rank = rank + jnp.where(col > score, 1.0, jnp.where(col == score, jnp.where(j > i, 1.0, 0.0), 0.0))
        keep_ref[...] = jnp.where(rank < float(min(N_SELECT, n_sel)), jnp.where(score >= 0.0, 1.0, 0.0), 0.0)

    @pl.when(s >= S)
    def _():
        ss = s - S
        n_blk = P * PAGE_SIZE // SEL_BLOCK
        n_key = P * PAGE_SIZE
        pick = (lax.broadcasted_iota(jnp.int32, (n_j, n_blk), 0)
                == lax.broadcasted_iota(jnp.int32, (n_j, n_blk), 1) + ss * n_blk).astype(bf16)
        keep_blk = jnp.dot(keep_ref[...].astype(bf16), pick, preferred_element_type=f32)
        expand = (lax.broadcasted_iota(jnp.int32, (n_blk, n_key), 0)
                  == lax.broadcasted_iota(jnp.int32, (n_blk, n_key), 1) // SEL_BLOCK).astype(bf16)
        bias = (jnp.dot(keep_blk.astype(bf16), expand, preferred_element_type=f32) - 1.0) * (-NEG)
        kpos = (ss * n_key + lax.broadcasted_iota(jnp.int32, (1, n_key), 1)).astype(f32)
        dist = t_col - kpos
        kt = [jnp.concatenate([sel_pages[k][ksl(g), :].astype(bf16) for k in range(P)], axis=1) for g in gs]
        vt = [jnp.concatenate([sel_pages[k][vsl(g), :].astype(bf16) for k in range(P)], axis=1) for g in gs]
        sc = [jnp.dot(qg_ref[g], kt[g], preferred_element_type=f32) for g in gs]
        sc = [sc[g] - slope_col[g] * dist + jnp.concatenate([bias[g * t_pad:(g + 1) * t_pad]] * B_GROUP, axis=0)
              for g in gs]
        m_new = [jnp.maximum(m_ref[g], jnp.max(sc[g], axis=-1, keepdims=True)) for g in gs]
        alpha = [jnp.exp(m_ref[g] - m_new[g]) for g in gs]
        pr = [jnp.where(sc[g] > 0.5 * NEG, jnp.exp(sc[g] - m_new[g]), 0.0) for g in gs]
        l_new = [l_ref[g] * alpha[g] + jnp.sum(pr[g], axis=-1, keepdims=True) for g in gs]
        pv = [_mm_nt(pr[g], vt[g]) for g in gs]
        for g in gs:
            acc_ref[g] = acc_ref[g] * alpha[g] + pv[g]
            m_ref[g] = m_new[g]
            l_ref[g] = l_new[g]

    @pl.when(s == 2 * S - 1)
    def _():
        xs = news_ref[...]
        xw = neww_ref[...]
        win = win_ref[...]
        n_win = win.shape[1]
        new_dist = t_col - (past + lax.broadcasted_iota(jnp.int32, (1, t_pad), 1)).astype(f32)
        win_dist = t_col - (past - n_win + lax.broadcasted_iota(jnp.int32, (1, n_win), 1)).astype(f32)
        keep_new = keep_ref[:, n_sel - 1:n_sel]
        gates = jax.nn.sigmoid(gates_ref[...])
        gb = gb_ref[...]
        silu_gb = gb * jax.nn.sigmoid(gb)
        qg = [qg_ref[g] for g in gs]
        sc = [_mm_nt(qg[g], xs[:, ksl(g)]) - slope_col[g] * new_dist for g in gs]
        sc = [jnp.where(new_dist >= 0.0, sc[g], NEG)
              + (jnp.concatenate([keep_new[g * t_pad:(g + 1) * t_pad]] * B_GROUP, axis=0) - 1.0) * (-NEG) for g in gs]
        sel = [_softmax_rows_update(sc[g], m_ref[g], l_ref[g], acc_ref[g], lambda p, g=g: _mm(p, xs[:, vsl(g)]))
               for g in gs]
        o_sel = [sel[g][2] * (1.0 / jnp.maximum(sel[g][1], 1e-30)) for g in gs]
        sw = [jnp.dot(qg[g], win[ksl(g), :].astype(bf16), preferred_element_type=f32) - slope_col[g] * win_dist
              for g in gs]
        sw = [jnp.where(win_dist <= float(WINDOW), sw[g], NEG) for g in gs]
        sn = [_mm_nt(qg[g], xw[:, ksl(g)]) - slope_col[g] * new_dist for g in gs]
        sn = [jnp.where(new_dist >= 0.0, jnp.where(new_dist <= float(WINDOW), sn[g], NEG), NEG) for g in gs]
        mw = [jnp.maximum(jnp.max(sw[g], axis=-1, keepdims=True), jnp.max(sn[g], axis=-1, keepdims=True)) for g in gs]
        pw = [jnp.where(sw[g] > 0.5 * NEG, jnp.exp(sw[g] - mw[g]), 0.0) for g in gs]
        pn = [jnp.where(sn[g] > 0.5 * NEG, jnp.exp(sn[g] - mw[g]), 0.0) for g in gs]
        lw = [jnp.sum(pw[g], axis=-1, keepdims=True) + jnp.sum(pn[g], axis=-1, keepdims=True) for g in gs]
        o_win = [(_mm_nt(pw[g], win[vsl(g), :]) + _mm(pn[g], xw[:, vsl(g)])) * (1.0 / jnp.maximum(lw[g], 1e-30))
                 for g in gs]
        for g in gs:
            def gate_col(branch):
                base = branch * B_HEADS + g * B_GROUP
                return jnp.concatenate([gates[:, base + r:base + r + 1] for r in range(B_GROUP)], axis=0)

            o = gate_col(0) * ocmp_ref[g] + gate_col(1) * o_sel[g] + gate_col(2) * o_win[g]
            for r in range(B_GROUP):
                cols = slice((g * B_GROUP + r) * 64, (g * B_GROUP + r + 1) * 64)
                o_ref[:, cols] = o[r * t_pad:(r + 1) * t_pad, :] * silu_gb[:, cols]


def _nsa_sample(u3, cache_cmp_t, cache_sel_t, cache_win_t, page_table, pw0_t, pw1_t, wck, wcv, *, t_new, pages_per_step):
    DB, t_pad, _ = u3.shape
    n_pages = page_table.shape[1]
    P = pages_per_step
    assert t_new < CMP_STRIDE and t_new <= t_pad and n_pages % P == 0 and (P * PAGE_SIZE // CMP_STRIDE) % 128 == 0
    S = n_pages // P
    past = n_pages * PAGE_SIZE
    n_ch = past // CMP_STRIDE
    n_sel = past // SEL_BLOCK + 1
    n_j = -(-n_sel // 128) * 128
    R = B_GROUP * t_pad
    kvw = 2 * B_KV_WIDTH
    pool = (lax.broadcasted_iota(jnp.int32, (P * PAGE_SIZE, P * PAGE_SIZE // CMP_STRIDE), 0) // CMP_STRIDE
            == lax.broadcasted_iota(jnp.int32, (P * PAGE_SIZE, P * PAGE_SIZE // CMP_STRIDE), 1)).astype(bf16)
    ublk = lambda w, off: pl.BlockSpec((None, t_pad, w), lambda b, s, pt: (b, 0, off // w))
    const = lambda s1, s2: pl.BlockSpec((s1, s2), lambda b, s, pt: (0, 0))
    page = lambda fn: pl.BlockSpec((None, kvw, PAGE_SIZE), fn)
    cmp_specs = [page(lambda b, s, pt, k=k: (pt[b, jnp.minimum(s, S - 1) * P + k], 0, 0)) for k in range(P)]
    sel_specs = [page(lambda b, s, pt, k=k: (pt[b, jnp.maximum(s - S, 0) * P + k], 0, 0)) for k in range(P)]
    grid_spec = pltpu.PrefetchScalarGridSpec(
        num_scalar_prefetch=1,
        grid=(DB, 2 * S),
        in_specs=cmp_specs + sel_specs + [
            pl.BlockSpec((None, kvw, cache_win_t.shape[2]), lambda b, s, pt: (b, 0, 0)),
            ublk(B_WIDTH, C_Q), ublk(kvw, C_KVS), ublk(kvw, C_KVW),
            const(kvw, PAGE_SIZE), const(kvw, PAGE_SIZE), const(*pool.shape),
            const(B_HEAD_DIM, B_HEAD_DIM), const(B_HEAD_DIM, B_HEAD_DIM),
            ublk(128, C_NG), ublk(B_WIDTH, C_GB)],
        out_specs=pl.BlockSpec((None, t_pad, B_WIDTH), lambda b, s, pt: (b, 0, 0)),
        scratch_shapes=[pltpu.VMEM((kvw, n_ch), f32), pltpu.VMEM((kvw, n_ch), f32),
                        pltpu.VMEM((B_KV_HEADS, R, B_HEAD_DIM), bf16),
                        pltpu.VMEM((B_KV_HEADS, R, B_HEAD_DIM), f32),
                        pltpu.VMEM((B_KV_HEADS * t_pad, n_j), f32),
                        pltpu.VMEM((B_KV_HEADS, R, 1), f32), pltpu.VMEM((B_KV_HEADS, R, 1), f32),
                        pltpu.VMEM((B_KV_HEADS, R, B_HEAD_DIM), f32)])
    return pl.pallas_call(
        functools.partial(_nsa_sample_kernel, n_pages=n_pages, pages_per_step=P, t_new=t_new, t_pad=t_pad),
        out_shape=jax.ShapeDtypeStruct((DB, t_pad, B_WIDTH), f32),
        grid_spec=grid_spec,
        compiler_params=_cparams(("parallel", "arbitrary")),
        name="nsa_sample",
    )(page_table, *([cache_cmp_t] * P), *([cache_sel_t] * P), cache_win_t, u3, u3, u3, pw0_t, pw1_t, pool, wck, wcv,
      u3, u3)


def _pos_weight_tiles(pos_k, pos_v, rows):
    def half(lo):
        t = jnp.concatenate([jnp.tile(pos_k[lo:lo + CMP_STRIDE], (1, B_KV_HEADS)),
                             jnp.tile(pos_v[lo:lo + CMP_STRIDE], (1, B_KV_HEADS))], axis=1)
        return jnp.tile(t, (rows // CMP_STRIDE, 1))
    return half(0), half(CMP_STRIDE)


def _a_cols(u_rows):
    return jnp.concatenate([u_rows[..., :4 * A_WIDTH], u_rows[..., C_LR:C_LR + DECAY_RANK + ICLR_RANK]], axis=-1)


def kernel(x_prompt, x_sample, cache_cmp_kv, cache_sel_kv, cache_win_kv, state_rwkv, state_shift, page_table, norm_in,
           w_in, mu_shift, w0, w_up, a0, a_up, k_k, k_a, r_k, gn_w, gn_b, cmp_pos_k, cmp_pos_v, w_cmp_k, w_cmp_v,
           w_pa, w_pb, w_o, norm_out):
    assert w_in.shape[0] == 1, "one layer"
    B, T, _ = x_prompt.shape
    DB, TS, _ = x_sample.shape
    H, N = A_HEADS, A_HEAD_DIM
    kvs = (2, B_KV_HEADS, B_HEAD_DIM)
    p = dict(mu_shift=mu_shift[0], w0=w0[0], w_up=w_up[0], a0=a0[0], a_up=a_up[0], k_k=k_k[0], k_a=k_a[0],
             r_k=r_k[0].reshape(-1), gn_w=gn_w[0], gn_b=gn_b[0])
    w_packed = _pack_w_in(w_in[0])
    wpa, wpb, wo = w_pa[0].astype(bf16), w_pb[0].astype(bf16), w_o[0].astype(bf16)

    xp = x_prompt.reshape(B * T, D_MODEL)
    up = _proj_in(xp, norm_in[0], w_packed)
    up3 = up.reshape(B, T, U_COLS)
    oa_p, s_p = _rwkv(up3, jnp.zeros((B, 1, 4 * A_WIDTH), f32), jnp.zeros((B, 1, 128), f32),
                      jnp.zeros((B, H, N, N), f32), p, chunk=64, t_valid=64)
    tt = 512
    pw0, pw1 = _pos_weight_tiles(cmp_pos_k[0], cmp_pos_v[0], tt)
    qt, ks, vst, kw, vwt, pa, pb = _nsa_prep(up3, pw0, pw1, tt)
    ob_p = _nsa_prompt(up3, qt, ks, vst, kw, vwt, pa, pb, w_cmp_k[0], w_cmp_v[0], 128)
    y_p = _merge(xp, oa_p.reshape(B * T, A_WIDTH), ob_p.reshape(B * T, B_WIDTH), up, wpa, wpb, wo, norm_out)

    t_pad = 8
    xs = x_sample.reshape(DB * TS, D_MODEL)
    us = _proj_in(xs, norm_in[0], w_packed)
    us3 = us.reshape(DB, TS, U_COLS)
    us3p = jnp.pad(us3, ((0, 0), (0, t_pad - TS), (0, 0)))
    shift0 = state_shift[0]
    oa_s, s_s = _rwkv(us3p, shift0[:, None, :4 * A_WIDTH], shift0[:, None, 4 * A_WIDTH:], state_rwkv[0], p,
                      chunk=t_pad, t_valid=TS)
    n_pool = cache_cmp_kv.shape[1]
    win = cache_win_kv[0]
    rows_last = lambda c, lead: jnp.transpose(c, (0, 2, 3, 4, 1)).reshape(lead, 2 * B_KV_WIDTH, c.shape[1])
    ob_s = _nsa_sample(us3p, rows_last(cache_cmp_kv[0], n_pool), rows_last(cache_sel_kv[0], n_pool),
                       rows_last(win, DB), page_table, pw0[:PAGE_SIZE].T, pw1[:PAGE_SIZE].T, w_cmp_k[0], w_cmp_v[0],
                       t_new=TS, pages_per_step=min(16, page_table.shape[1]))
    y_s = _merge(xs, oa_s[:, :TS].reshape(DB * TS, A_WIDTH), ob_s[:, :TS].reshape(DB * TS, B_WIDTH), us, wpa, wpb, wo,
                 norm_out)

    def kv_out(u3_, col, lead, t):
        return u3_[..., col:col + 2 * B_KV_WIDTH].reshape((1, lead, t) + kvs)

    wk = min(WINDOW, T)
    new_w_s = kv_out(us3, C_KVW, DB, TS)[0]
    s_win = jnp.concatenate([win, new_w_s], axis=1)[:, TS:][None]
    return (y_p.reshape(B, T, D_MODEL), y_s.reshape(DB, TS, D_MODEL),
            kv_out(up3, C_KVC, B, T), kv_out(up3, C_KVS, B, T), kv_out(up3[:, T - wk:], C_KVW, B, wk),
            s_p[None], _a_cols(up3[:, T - 1])[None],
            kv_out(us3, C_KVC, DB, TS), kv_out(us3, C_KVS, DB, TS), s_win,
            s_s[None], _a_cols(us3[:, TS - 1])[None])
```

```python
import functools
import math

import jax
import jax.numpy as jnp
from jax import lax
from jax.experimental import pallas as pl
from jax.experimental.pallas import tpu as pltpu

f32 = jnp.float32
bf16 = jnp.bfloat16

D_MODEL = 2048
PAGE_SIZE = 128
A_HEADS = 16
A_HEAD_DIM = 64
A_WIDTH = A_HEADS * A_HEAD_DIM
DECAY_RANK = 64
ICLR_RANK = 64
A_COLS = 4 * A_WIDTH + DECAY_RANK + ICLR_RANK
GN_EPS = 64e-5
B_HEADS = 16
B_KV_HEADS = 4
B_GROUP = B_HEADS // B_KV_HEADS
B_HEAD_DIM = 64
B_WIDTH = B_HEADS * B_HEAD_DIM
B_KV_WIDTH = B_KV_HEADS * B_HEAD_DIM
B_COLS = 2 * B_WIDTH + 6 * B_KV_WIDTH + 3 * B_HEADS
CMP_BLOCK = 32
CMP_STRIDE = 16
SEL_BLOCK = 64
N_SELECT = 16
N_LOCAL = 2
WINDOW = 512
FORCED_SCORE = 1e4
ATTN_SCALE = B_HEAD_DIM ** -0.5
RMS_EPS = 1e-6
NEG = -1e30

C_RKVG = 0
C_Q = 4096
C_GB = 5120
C_GA_M = 6144
C_GB_M = 8192
C_KVC = 10240
C_KVS = 10752
C_KVW = 11264
C_LR = 11776
C_NG = 11904
U_COLS = 12288

VMEM_LIMIT = 56 * 1024 * 1024
HI = lax.Precision.HIGHEST


def _cparams(sem):
    return pltpu.CompilerParams(dimension_semantics=sem, vmem_limit_bytes=VMEM_LIMIT)


def _pack_w_in(w):
    a, b, m = w[:, :A_COLS], w[:, A_COLS:A_COLS + B_COLS], w[:, A_COLS + B_COLS:]
    z = jnp.zeros((w.shape[0], U_COLS - C_NG - 3 * B_HEADS), w.dtype)
    return jnp.concatenate(
        [a[:, :4 * A_WIDTH], b[:, :2 * B_WIDTH], m, b[:, 2 * B_WIDTH:2 * B_WIDTH + 6 * B_KV_WIDTH],
         a[:, 4 * A_WIDTH:], b[:, 2 * B_WIDTH + 6 * B_KV_WIDTH:], z], axis=1).astype(bf16)


def _proj_in_kernel(x_ref, g_ref, w_ref, o_ref, xn_ref):
    @pl.when(pl.program_id(1) == 0)
    def _():
        x = x_ref[...]
        ms = jnp.mean(x * x, axis=-1, keepdims=True)
        xn_ref[...] = (x * lax.rsqrt(ms + RMS_EPS) * g_ref[...]).astype(bf16)

    o_ref[...] = jnp.dot(xn_ref[...], w_ref[...], preferred_element_type=f32)


def _proj_in(x2d, norm_g, w_packed):
    m = x2d.shape[0]
    tm = min(1024, m)
    tn = 1024
    return pl.pallas_call(
        _proj_in_kernel,
        out_shape=jax.ShapeDtypeStruct((m, U_COLS), f32),
        grid=(m // tm, U_COLS // tn),
        in_specs=[pl.BlockSpec((tm, D_MODEL), lambda i, j: (i, 0)),
                  pl.BlockSpec((1, D_MODEL), lambda i, j: (0, 0)),
                  pl.BlockSpec((D_MODEL, tn), lambda i, j: (0, j))],
        out_specs=pl.BlockSpec((tm, tn), lambda i, j: (i, j)),
        scratch_shapes=[pltpu.VMEM((tm, D_MODEL), bf16)],
        compiler_params=_cparams(("parallel", "arbitrary")),
        name="proj_in",
    )(x2d, norm_g.reshape(1, D_MODEL), w_packed)


def _mm(a, b):
    return jnp.dot(a.astype(bf16), b.astype(bf16), preferred_element_type=f32)


def _mm_nt(a, b):
    return lax.dot_general(a.astype(bf16), b.astype(bf16), (((1,), (1,)), ((), ())), preferred_element_type=f32)


def _mm_tn(a, b):
    return lax.dot_general(a.astype(bf16), b.astype(bf16), (((0,), (0,)), ((), ())), preferred_element_type=f32)


HEADS_PER_GROUP = 4
N_HEAD_GROUPS = A_HEADS // HEADS_PER_GROUP
GROUP_WIDTH = HEADS_PER_GROUP * A_HEAD_DIM


def _rwkv_wide_kernel(um_ref, ulr_ref, pm_ref, plr_ref, s0_ref, mum_ref, mulr_ref, w0_ref, wup_ref, a0_ref,
                      aup_ref, kk_ref, ka_ref, rk_ref, gnw_ref, gnb_ref, o_ref, sout_ref,
                      s_ref, prevm_ref, prevlr_ref, *, chunk, t_valid):
    C = chunk
    N, HG, NG, GW = A_HEAD_DIM, HEADS_PER_GROUP, N_HEAD_GROUPS, GROUP_WIDTH
    RB = um_ref.shape[0]
    c = pl.program_id(1)

    @pl.when(c == 0)
    def _():
        s_ref[...] = s0_ref[...]
        prevm_ref[...] = pm_ref[...]
        prevlr_ref[...] = plr_ref[...]

    row = lax.broadcasted_iota(jnp.int32, (C, 1), 0)
    tril_b = (lax.broadcasted_iota(jnp.int32, (C, C), 0) >= lax.broadcasted_iota(jnp.int32, (C, C), 1)).astype(bf16)
    t_i = lax.broadcasted_iota(jnp.int32, (C, HG * C), 0)
    s_i = lax.broadcasted_iota(jnp.int32, (C, HG * C), 1) % C
    tril_incl = t_i >= s_i
    tril_strict = t_i > s_i
    eye = (t_i == s_i).astype(f32)
    head_of_dim = lax.broadcasted_iota(jnp.int32, (1, GW), 1) // N
    head_of_tok = lax.broadcasted_iota(jnp.int32, (1, HG * C), 1) // C
    ones_bd = (lax.broadcasted_iota(jnp.int32, (GW, GW), 0) // N
               == lax.broadcasted_iota(jnp.int32, (GW, GW), 1) // N).astype(bf16)
    n_sq = int(math.log2(C)) - 1
    valid = (row < t_valid).astype(f32) if t_valid < C else None
    gcols = [slice(q * GW, (q + 1) * GW) for q in range(NG)]
    qs = range(NG)

    def shifted(u, prev):
        return jnp.where(row == 0, prev, pltpu.roll(u, 1, 0))

    def head_sums(x):
        xb = x.astype(bf16)
        return jnp.concatenate([jnp.dot(xb[:, gc], ones_bd, preferred_element_type=f32) for gc in gcols], axis=1)

    def block_rows(x, head_of_lane):
        return jnp.concatenate([jnp.where(head_of_lane == h, x, 0.0) for h in range(HG)], axis=0).astype(bf16)

    def mm(a, b):
        return jnp.dot(a.astype(bf16), b, preferred_element_type=f32)

    def mm_nt(a, b):
        return lax.dot_general(a.astype(bf16), b, (((1,), (1,)), ((), ())), preferred_element_type=f32)

    def prepare(bi, out):
        um = um_ref[bi]
        ulr = ulr_ref[bi]
        prev_m = prevm_ref[bi]
        prev_lr = prevlr_ref[bi]
        prevm_ref[bi] = um[C - 1:C, :]
        prevlr_ref[bi] = ulr[C - 1:C, :]
        uslr = ulr + mulr_ref[...] * (shifted(ulr, prev_lr) - ulr)
        zw = w0_ref[...] + _mm(jnp.tanh(uslr[:, :DECAY_RANK]), wup_ref[...])
        a = jax.nn.sigmoid(a0_ref[...] + _mm(uslr[:, DECAY_RANK:], aup_ref[...]))
        yield
        secs = []
        for i in range(4):
            cols = slice(i * A_WIDTH, (i + 1) * A_WIDTH)
            secs.append(um[:, cols] + mum_ref[:, cols] * (shifted(um[:, cols], prev_m[:, cols]) - um[:, cols]))
            yield
        r, k, v, g = secs
        softplus_neg = jnp.maximum(-zw, 0.0) + jnp.log(1.0 + jnp.exp(-jnp.abs(zw)))
        logw = -jnp.exp(-softplus_neg - 0.5)
        if valid is not None:
            logw = logw * valid
        yield
        logw_hi = logw.astype(bf16)
        logw_lo = (logw - logw_hi.astype(f32)).astype(bf16)
        cum = (jnp.dot(tril_b, logw_hi, preferred_element_type=f32)
               + jnp.dot(tril_b, logw_lo, preferred_element_type=f32))
        yield
        e_pos = jnp.exp(cum)
        e_neg = jnp.exp(-cum)
        yield
        e_prev = jnp.exp(cum - logw)
        kk = k * kk_ref[...]
        yield
        ssq = head_sums(kk * kk)
        yield
        k2 = k * (1.0 + (a - 1.0) * ka_ref[...])
        bonus = head_sums(r * k2 * rk_ref[...])
        yield
        kk = kk * lax.rsqrt(jnp.maximum(ssq, 1e-24))
        kn = k2 * e_neg
        bn = kk * a * e_neg
        if valid is not None:
            kn = kn * valid
            bn = bn * valid
        yield
        lhs = jnp.concatenate([kk * e_prev, r * e_pos], axis=0).astype(bf16)
        out.update(v=v, g=g, bonus=bonus, kn=kn, bn=bn, lhs=lhs, e_last=e_pos[C - 1:C, :])
        yield

    def solve(bi, pre, out):
        v = [pre["v"][:, gc] for gc in gcols]
        kn = [pre["kn"][:, gc] for gc in gcols]
        bn = [pre["bn"][:, gc] for gc in gcols]
        lhs = [pre["lhs"][:, gc] for gc in gcols]
        s0 = [s_ref[bi, q] for q in qs]
        qk_b = [mm_nt(lhs[q], block_rows(bn[q], head_of_dim)) for q in qs]
        yield
        qk_k = [mm_nt(lhs[q], block_rows(kn[q], head_of_dim)) for q in qs]
        yield
        w0s = [mm_nt(lhs[q], block_rows(s0[q], head_of_dim)) for q in qs]
        yield
        pw = [jnp.where(tril_strict, -qk_b[q][:C], 0.0) for q in qs]
        tinv = [eye + pw[q] for q in qs]
        bd = [block_rows(pw[q], head_of_tok) for q in qs]
        for _ in range(n_sq):
            pw = [mm(pw[q], bd[q]) for q in qs]
            yield
            bd = [block_rows(pw[q], head_of_tok) for q in qs]
            tinv = [tinv[q] + mm(tinv[q], bd[q]) for q in qs]
            yield
        bd_v = [block_rows(v[q], head_of_dim) for q in qs]
        rhs_u = [w0s[q][:C] + mm(jnp.where(tril_strict, qk_k[q][:C], 0.0), bd_v[q]) for q in qs]
        yield
        u = [mm(tinv[q], block_rows(rhs_u[q], head_of_dim)) for q in qs]
        yield
        p_cat = [jnp.concatenate([jnp.where(tril_incl, qk_k[q][C:], 0.0), jnp.where(tril_incl, -qk_b[q][C:], 0.0)],
                                 axis=1) for q in qs]
        out["o"] = [w0s[q][C:] + mm(p_cat[q], jnp.concatenate([bd_v[q], block_rows(u[q], head_of_dim)], axis=0))
                    for q in qs]
        yield
        full = [lax.dot_general(jnp.concatenate([v[q], -u[q]], axis=0).astype(bf16),
                                jnp.concatenate([kn[q], bn[q]], axis=0).astype(bf16),
                                (((0,), (0,)), ((), ())), preferred_element_type=f32) for q in qs]
        yield
        for q in qs:
            upd = jnp.where(head_of_dim == 0, full[q][0:N], 0.0)
            for h in range(1, HG):
                upd = upd + jnp.where(head_of_dim == h, full[q][h * N:(h + 1) * N], 0.0)
            s_ref[bi, q] = (s0[q] + upd) * pre["e_last"][:, gcols[q]]
        yield

    def finish(bi, pre, mid):
        o = jnp.concatenate(mid["o"], axis=1)
        mean = head_sums(o) * (1.0 / N)
        yield
        d = o - mean
        var = head_sums(d * d) * (1.0 / N)
        yield
        g = pre["g"]
        on = d * lax.rsqrt(var + GN_EPS) * gnw_ref[...] + gnb_ref[...] + pre["bonus"] * pre["v"]
        o_ref[bi] = on * (g * jax.nn.sigmoid(g))
        yield

    def interleave(gens):
        gens = list(gens)
        while gens:
            for gen in list(gens):
                if next(gen, "done") == "done":
                    gens.remove(gen)

    pre = [{} for _ in range(RB)]
    mid = [{} for _ in range(RB)]
    halves = [range(0, RB // 2), range(RB // 2, RB)] if RB > 1 else [range(RB)]
    prep = lambda rows_: [prepare(bi, pre[bi]) for bi in rows_]
    solv = lambda rows_: [solve(bi, pre[bi], mid[bi]) for bi in rows_]
    fin = lambda rows_: [finish(bi, pre[bi], mid[bi]) for bi in rows_]
    interleave(prep(halves[0]))
    for i, rows_ in enumerate(halves):
        interleave(solv(rows_) + (prep(halves[i + 1]) if i + 1 < len(halves) else [])
                   + (fin(halves[i - 1]) if i >= 1 else []))
    interleave(fin(halves[-1]))

    @pl.when(c == pl.num_programs(1) - 1)
    def _():
        sout_ref[...] = s_ref[...]


def _state_to_wide(s):
    b = s.shape[0]
    return s.reshape(b, N_HEAD_GROUPS, HEADS_PER_GROUP, A_HEAD_DIM, A_HEAD_DIM).transpose(0, 1, 3, 2, 4).reshape(
        b, N_HEAD_GROUPS, A_HEAD_DIM, GROUP_WIDTH)


def _state_from_wide(s):
    b = s.shape[0]
    return s.reshape(b, N_HEAD_GROUPS, A_HEAD_DIM, HEADS_PER_GROUP, A_HEAD_DIM).transpose(0, 1, 3, 2, 4).reshape(
        b, A_HEADS, A_HEAD_DIM, A_HEAD_DIM)


def _rwkv_wide(u3, prev_m, prev_lr, s0, p, *, chunk, t_valid, rows):
    B, T, _ = u3.shape
    C = chunk
    NG, N, GW = N_HEAD_GROUPS, A_HEAD_DIM, GROUP_WIDTH
    assert B % rows == 0 and T % C == 0
    row = lambda v: v.reshape(1, -1)
    vec = lambda n: pl.BlockSpec((1, n), lambda b, c: (0, 0))
    in_specs = [
        pl.BlockSpec((rows, C, 4 * A_WIDTH), lambda b, c: (b, c, 0)),
        pl.BlockSpec((rows, C, 128), lambda b, c: (b, c, C_LR // 128)),
        pl.BlockSpec((rows, 1, 4 * A_WIDTH), lambda b, c: (b, 0, 0)),
        pl.BlockSpec((rows, 1, 128), lambda b, c: (b, 0, 0)),
        pl.BlockSpec((rows, NG, N, GW), lambda b, c: (b, 0, 0, 0)),
        vec(4 * A_WIDTH), vec(128), vec(A_WIDTH),
        pl.BlockSpec((DECAY_RANK, A_WIDTH), lambda b, c: (0, 0)),
        vec(A_WIDTH),
        pl.BlockSpec((ICLR_RANK, A_WIDTH), lambda b, c: (0, 0)),
        vec(A_WIDTH), vec(A_WIDTH), vec(A_WIDTH), vec(A_WIDTH), vec(A_WIDTH),
    ]
    o_a, s_out = pl.pallas_call(
        functools.partial(_rwkv_wide_kernel, chunk=C, t_valid=t_valid),
        out_shape=(jax.ShapeDtypeStruct((B, T, A_WIDTH), f32), jax.ShapeDtypeStruct((B, NG, N, GW), f32)),
        grid=(B // rows, T // C),
        in_specs=in_specs,
        out_specs=(pl.BlockSpec((rows, C, A_WIDTH), lambda b, c: (b, c, 0)),
                   pl.BlockSpec((rows, NG, N, GW), lambda b, c: (b, 0, 0, 0))),
        scratch_shapes=[pltpu.VMEM((rows, NG, N, GW), f32), pltpu.VMEM((rows, 1, 4 * A_WIDTH), f32),
                        pltpu.VMEM((rows, 1, 128), f32)],
        compiler_params=_cparams(("parallel", "arbitrary")),
        name="rwkv7",
    )(u3, u3, prev_m, prev_lr, _state_to_wide(s0), row(p["mu_shift"][:4 * A_WIDTH]), row(p["mu_shift"][4 * A_WIDTH:]),
      row(p["w0"]), p["w_up"], row(p["a0"]), p["a_up"], row(p["k_k"]), row(p["k_a"]), row(p["r_k"]),
      row(p["gn_w"]), row(p["gn_b"]))
    return o_a, _state_from_wide(s_out)


def _merge_kernel(x_ref, oa_ref, ob_ref, ga_ref, gb_ref, wpa_ref, wpb_ref, wo_ref, gout_ref, y_ref):
    pa = jnp.dot(oa_ref[...].astype(bf16), wpa_ref[...], preferred_element_type=f32)
    pb = jnp.dot(ob_ref[...].astype(bf16), wpb_ref[...], preferred_element_type=f32)
    merged = jax.nn.sigmoid(ga_ref[...]) * pa + jax.nn.sigmoid(gb_ref[...]) * pb
    h = x_ref[...] + jnp.dot(merged.astype(bf16), wo_ref[...], preferred_element_type=f32)
    ms = jnp.mean(h * h, axis=-1, keepdims=True)
    y_ref[...] = h * lax.rsqrt(ms + RMS_EPS) * gout_ref[...]


def _merge(x2d, o_a, o_b, u2d, w_pa, w_pb, w_o, norm_out):
    m = x2d.shape[0]
    tm = min(256, m)
    const = lambda shape: pl.BlockSpec(shape, lambda i: (0, 0), pipeline_mode=pl.Buffered(1))
    return pl.pallas_call(
        _merge_kernel,
        out_shape=jax.ShapeDtypeStruct((m, D_MODEL), f32),
        grid=(m // tm,),
        in_specs=[pl.BlockSpec((tm, D_MODEL), lambda i: (i, 0)),
                  pl.BlockSpec((tm, A_WIDTH), lambda i: (i, 0)),
                  pl.BlockSpec((tm, B_WIDTH), lambda i: (i, 0)),
                  pl.BlockSpec((tm, D_MODEL), lambda i: (i, C_GA_M // D_MODEL)),
                  pl.BlockSpec((tm, D_MODEL), lambda i: (i, C_GB_M // D_MODEL)),
                  const((A_WIDTH, D_MODEL)), const((B_WIDTH, D_MODEL)), const((D_MODEL, D_MODEL)),
                  const((1, D_MODEL))],
        out_specs=pl.BlockSpec((tm, D_MODEL), lambda i: (i, 0)),
        compiler_params=_cparams(("parallel",)),
        name="merge_out",
    )(x2d, o_a, o_b, u2d, u2d, w_pa, w_pb, w_o, norm_out.reshape(1, D_MODEL))


def _alibi_slope(head):
    return 2.0 ** (-8.0 * (head + 1) / B_HEADS)


def _slope_row(g, tq):
    lane_head = lax.broadcasted_iota(jnp.int32, (1, B_GROUP * tq), 1) // tq
    out = jnp.zeros((1, B_GROUP * tq), f32)
    for r in range(B_GROUP):
        out = jnp.where(lane_head == r, _alibi_slope(g * B_GROUP + r), out)
    return out


def _tile_heads(row):
    return jnp.concatenate([row] * B_GROUP, axis=1)


def _overlap_t(n_sel_rows, n_ch):
    j = lax.broadcasted_iota(jnp.int32, (n_sel_rows, n_ch), 0) * SEL_BLOCK
    n = lax.broadcasted_iota(jnp.int32, (n_sel_rows, n_ch), 1) * CMP_STRIDE
    ov = jnp.minimum(n + CMP_BLOCK, j + SEL_BLOCK) - jnp.maximum(n, j)
    return jnp.maximum(ov, 0).astype(f32) * (1.0 / CMP_BLOCK)


M_INIT = 0.1 * NEG


def _attn_step_multi(s, carries, v_t):
    n = range(len(s))
    m_new = [jnp.maximum(carries[i][0], jnp.max(s[i], axis=0, keepdims=True)) for i in n]
    alpha = [jnp.exp(carries[i][0] - m_new[i]) for i in n]
    p = [jnp.exp(s[i] - m_new[i]) for i in n]
    l = [carries[i][1] * alpha[i] + jnp.sum(p[i], axis=0, keepdims=True) for i in n]
    pv = [jnp.dot(v_t[i], p[i].astype(bf16), preferred_element_type=f32) for i in n]
    return [(m_new[i], l[i], carries[i][2] * alpha[i] + pv[i]) for i in n]


def _attn_init(lanes):
    return (jnp.full((1, lanes), M_INIT, f32), jnp.zeros((1, lanes), f32), jnp.zeros((B_HEAD_DIM, lanes), f32))


def _split3(x):
    hi = x.astype(bf16)
    r1 = x - hi.astype(f32)
    mid = r1.astype(bf16)
    lo = (r1 - mid.astype(f32)).astype(bf16)
    return hi, mid, lo


POS_SPLIT = 128
K_AUG = 16


def _key_aug(pos_i):
    one = jnp.ones(pos_i.shape, f32)
    hi = (pos_i // POS_SPLIT).astype(f32)
    lo = (pos_i % POS_SPLIT).astype(f32)
    zero = jnp.zeros((pos_i.shape[0], K_AUG - 9), f32)
    return jnp.concatenate([one, one, one, hi, hi, hi, lo, lo, lo, zero], axis=1).astype(bf16)


def _query_aug(slope_row, t_row):
    rows = _split3(-slope_row * t_row) + _split3(slope_row * float(POS_SPLIT)) + _split3(slope_row)
    zero = jnp.zeros((K_AUG - 9, slope_row.shape[1]), bf16)
    return jnp.concatenate(list(rows) + [zero], axis=0)


def _attn_finish(carry):
    _, l, acc = carry
    return acc * (1.0 / jnp.maximum(l, 1e-30))


def _compressed_multi(kc, vc_t, q_t, t_row, slope_rows, n_ch):
    n = range(len(kc))
    blk_end = (lax.broadcasted_iota(jnp.int32, (n_ch, 1), 0) * CMP_STRIDE + (CMP_BLOCK - 1)).astype(f32)
    dist = t_row - blk_end
    mask = dist >= 0.0
    s = [jnp.dot(kc[i], q_t[i], preferred_element_type=f32) for i in n]
    s = [jnp.where(mask, s[i] - slope_rows[i] * dist, NEG) for i in n]
    m = [jnp.max(s[i], axis=0, keepdims=True) for i in n]
    p = [jnp.where(mask, jnp.exp(s[i] - m[i]), 0.0) for i in n]
    inv = [1.0 / jnp.maximum(jnp.sum(p[i], axis=0, keepdims=True), 1e-30) for i in n]
    p = [p[i] * inv[i] for i in n]
    o = [jnp.dot(vc_t[i], p[i].astype(bf16), preferred_element_type=f32) for i in n]
    return o, p


def _selection_scores(p, tq, tok_row_i, n_rows, n_sel, n_ch):
    psum = p[:, 0:tq]
    for r in range(1, B_GROUP):
        psum = psum + p[:, r * tq:(r + 1) * tq]
    imp = jnp.dot(_overlap_t(n_rows, n_ch), psum, preferred_element_type=f32, precision=HI)
    j = lax.broadcasted_iota(jnp.int32, (n_rows, 1), 0)
    back = tok_row_i // SEL_BLOCK - j
    forced = (j == 0) | ((back >= 0) & (back < N_LOCAL))
    score = jnp.where(forced, FORCED_SCORE, jnp.where(back >= 0, imp, -1.0))
    return jnp.where(j < n_sel, score, -2.0)


def _kv_pool_partials(x, pw0, pw1):
    rows = x.shape[0]
    n = rows // CMP_STRIDE
    pool = (lax.broadcasted_iota(jnp.int32, (n, rows), 1) // CMP_STRIDE
            == lax.broadcasted_iota(jnp.int32, (n, rows), 0)).astype(f32)
    a = jnp.dot(pool, x * pw0, preferred_element_type=f32, precision=HI)
    b = jnp.dot(pool, x * pw1, preferred_element_type=f32, precision=HI)
    return a, b


def _compress_kv(pooled, wck, wcv, kc_ref, vct_ref):
    for g in range(B_KV_HEADS):
        kc_ref[g] = _mm(pooled[:, g * 64:(g + 1) * 64], wck).astype(bf16)
        vc = _mm(pooled[:, B_KV_WIDTH + g * 64:B_KV_WIDTH + (g + 1) * 64], wcv)
        vct_ref[g] = vc.T.astype(bf16)


def _nsa_prep_kernel(q_ref, kvc_ref, kvs_ref, kvw_ref, pw0_ref, pw1_ref,
                     qt_ref, ks_ref, vst_ref, kw_ref, vwt_ref, a_ref, b_ref):
    qt_ref[...] = (q_ref[...] * ATTN_SCALE).T.astype(bf16)
    tt = q_ref.shape[0]
    aug = _key_aug(pl.program_id(1) * tt + lax.broadcasted_iota(jnp.int32, (tt, 1), 0))
    for src, k_ref, vt_ref in ((kvs_ref, ks_ref, vst_ref), (kvw_ref, kw_ref, vwt_ref)):
        x = src[...]
        for g in range(B_KV_HEADS):
            k_ref[g] = jnp.concatenate([x[:, g * 64:(g + 1) * 64].astype(bf16), aug], axis=1)
        vt_ref[...] = x[:, B_KV_WIDTH:].T.astype(bf16)
    a, b = _kv_pool_partials(kvc_ref[...], pw0_ref[...], pw1_ref[...])
    a_ref[...] = a
    b_ref[...] = b


def _nsa_prep(u3, pw0, pw1, tt):
    B, T, _ = u3.shape
    n_ch = tt // CMP_STRIDE
    ublk = lambda w, off: pl.BlockSpec((None, tt, w), lambda b, t: (b, t, off // w))
    const = pl.BlockSpec((tt, 2 * B_KV_WIDTH), lambda b, t: (0, 0))
    return pl.pallas_call(
        _nsa_prep_kernel,
        out_shape=(jax.ShapeDtypeStruct((B, B_WIDTH, T), bf16),
                   jax.ShapeDtypeStruct((B, B_KV_HEADS, T, B_HEAD_DIM + K_AUG), bf16),
                   jax.ShapeDtypeStruct((B, B_KV_WIDTH, T), bf16),
                   jax.ShapeDtypeStruct((B, B_KV_HEADS, T, B_HEAD_DIM + K_AUG), bf16),
                   jax.ShapeDtypeStruct((B, B_KV_WIDTH, T), bf16),
                   jax.ShapeDtypeStruct((B, T // CMP_STRIDE, 2 * B_KV_WIDTH), f32),
                   jax.ShapeDtypeStruct((B, T // CMP_STRIDE, 2 * B_KV_WIDTH), f32)),
        grid=(B, T // tt),
        in_specs=[ublk(B_WIDTH, C_Q), ublk(2 * B_KV_WIDTH, C_KVC), ublk(2 * B_KV_WIDTH, C_KVS),
                  ublk(2 * B_KV_WIDTH, C_KVW), const, const],
        out_specs=(pl.BlockSpec((None, B_WIDTH, tt), lambda b, t: (b, 0, t)),
                   pl.BlockSpec((None, B_KV_HEADS, tt, B_HEAD_DIM + K_AUG), lambda b, t: (b, 0, t, 0)),
                   pl.BlockSpec((None, B_KV_WIDTH, tt), lambda b, t: (b, 0, t)),
                   pl.BlockSpec((None, B_KV_HEADS, tt, B_HEAD_DIM + K_AUG), lambda b, t: (b, 0, t, 0)),
                   pl.BlockSpec((None, B_KV_WIDTH, tt), lambda b, t: (b, 0, t)),
                   pl.BlockSpec((None, n_ch, 2 * B_KV_WIDTH), lambda b, t: (b, t, 0)),
                   pl.BlockSpec((None, n_ch, 2 * B_KV_WIDTH), lambda b, t: (b, t, 0))),
        compiler_params=_cparams(("parallel", "parallel")),
        name="nsa_prep",
    )(u3, u3, u3, u3, pw0, pw1)


def _rank_select(score_ref, n_rows, n_sel, lanes):
    score = score_ref[...].reshape(n_rows, lanes)
    j = lax.broadcasted_iota(jnp.int32, (n_rows, 1), 0)

    def body(i, rank):
        row = score_ref[i]
        beats = jnp.where(row > score, 1.0, jnp.where(row == score, jnp.where(i < j, 1.0, 0.0), 0.0))
        return rank + beats

    rank = lax.fori_loop(0, n_sel, body, jnp.zeros((n_rows, lanes), f32))
    return jnp.where(rank < float(min(N_SELECT, n_sel)), jnp.where(score >= 0.0, 1.0, 0.0), 0.0)


def _nsa_prompt_kernel(qt_ref, ks_ref, vst_ref, kw_ref, vwt_ref, a_ref, b_ref, wck_ref, wcv_ref, gates_ref, gb_ref,
                       o_ref, kc_ref, vct_ref, score_ref, keep_ref, cmp_ref, qaug_ref, m_ref, l_ref, acc_ref, ot_ref,
                       *, seq, tq):
    i = pl.program_id(1)
    n_ch = seq // CMP_STRIDE
    n_sel = seq // SEL_BLOCK
    lanes = B_GROUP * tq
    tk = 2 * SEL_BLOCK

    @pl.when(i == 0)
    def _():
        pooled = a_ref[...] + pltpu.roll(b_ref[...], n_ch - 1, 0)
        _compress_kv(pooled, wck_ref[...], wcv_ref[...], kc_ref, vct_ref)

    G = B_KV_HEADS
    gs = range(G)
    tok_i = i * tq + lax.broadcasted_iota(jnp.int32, (1, tq), 1)
    t_row = _tile_heads(tok_i.astype(f32))
    key_off = lax.broadcasted_iota(jnp.int32, (tk, 1), 0).astype(f32)
    slopes = [_slope_row(g, tq) for g in gs]

    def q_t(g):
        return jnp.concatenate([qt_ref[(g * B_GROUP + r) * 64:(g * B_GROUP + r + 1) * 64, :] for r in range(B_GROUP)],
                               axis=1)

    o_cmp, prob = _compressed_multi([kc_ref[g] for g in gs], [vct_ref[g] for g in gs], [q_t(g) for g in gs],
                                    t_row, slopes, n_ch)
    for g in gs:
        cmp_ref[g] = o_cmp[g]
    score = jnp.concatenate([_selection_scores(prob[g], tq, tok_i, n_sel, n_sel, n_ch) for g in gs], axis=1)
    for j in range(n_sel):
        score_ref[j] = score[j:j + 1, :]
    keep = _rank_select(score_ref, n_sel, n_sel, G * tq)
    for j in range(n_sel):
        keep_ref[j] = (keep[j:j + 1, :] - 1.0) * (-NEG)

    for g in gs:
        qaug_ref[g] = jnp.concatenate([q_t(g), _query_aug(slopes[g], t_row)], axis=0)
    for c in range(2 * G):
        m, l, acc = _attn_init(lanes)
        m_ref[c] = m
        l_ref[c] = l
        acc_ref[c] = acc

    def step(kt, with_window):
        off = pl.multiple_of(kt * tk, tk)
        dist = t_row - (key_off + (kt * tk).astype(f32))
        causal = jnp.where(dist >= 0.0, 0.0, NEG)
        row0 = keep_ref[2 * kt]
        row1 = keep_ref[2 * kt + 1]
        s, v, chains = [], [], []
        for g in gs:
            sg = jnp.dot(ks_ref[g, pl.ds(off, tk), :], qaug_ref[g], preferred_element_type=f32) + causal
            s.append(jnp.concatenate([sg[:SEL_BLOCK] + _tile_heads(row0[:, g * tq:(g + 1) * tq]),
                                      sg[SEL_BLOCK:] + _tile_heads(row1[:, g * tq:(g + 1) * tq])], axis=0))
            v.append(vst_ref[g * 64:(g + 1) * 64, pl.ds(off, tk)])
            chains.append(g)
        if with_window:
            band = jnp.where(dist <= float(WINDOW), causal, NEG)
            for g in gs:
                s.append(jnp.dot(kw_ref[g, pl.ds(off, tk), :], qaug_ref[g], preferred_element_type=f32) + band)
                v.append(vwt_ref[g * 64:(g + 1) * 64, pl.ds(off, tk)])
                chains.append(G + g)
        out = _attn_step_multi(s, [(m_ref[c], l_ref[c], acc_ref[c]) for c in chains], v)
        for c, (m, l, acc) in zip(chains, out):
            m_ref[c] = m
            l_ref[c] = l
            acc_ref[c] = acc

    lo = jnp.maximum(i * tq - WINDOW, 0) // tk
    hi = (i * tq) // tk + tq // tk

    def sel_only(kt, carry):
        step(kt, False)
        return carry

    def sel_and_win(kt, carry):
        step(kt, True)
        return carry

    lax.fori_loop(0, lo, sel_only, 0)
    lax.fori_loop(lo, hi, sel_and_win, 0)

    gates_t = jax.nn.sigmoid(gates_ref[...]).T

    def gate_row(branch, g):
        base = branch * B_HEADS + g * B_GROUP
        return jnp.concatenate([gates_t[base + r:base + r + 1, :] for r in range(B_GROUP)], axis=1)

    o_sel = [_attn_finish((m_ref[g], l_ref[g], acc_ref[g])) for g in gs]
    o_win = [_attn_finish((m_ref[G + g], l_ref[G + g], acc_ref[G + g])) for g in gs]
    o_t = [gate_row(0, g) * cmp_ref[g] + gate_row(1, g) * o_sel[g] + gate_row(2, g) * o_win[g] for g in gs]
    for g in gs:
        for r in range(B_GROUP):
            ot_ref[(g * B_GROUP + r) * 64:(g * B_GROUP + r + 1) * 64, :] = o_t[g][:, r * tq:(r + 1) * tq]

    gb = gb_ref[...]
    o_ref[...] = ot_ref[...].T * (gb * jax.nn.sigmoid(gb))


def _nsa_prompt(u3, qt, ks, vst, kw, vwt, a, b, wck, wcv, tq):
    B, T, _ = u3.shape
    n_ch = T // CMP_STRIDE
    n_sel = T // SEL_BLOCK
    lanes = B_GROUP * tq
    per_b3 = lambda s1, s2: pl.BlockSpec((None, s1, s2), lambda b, i: (b, 0, 0))
    per_b4 = pl.BlockSpec((None, B_KV_HEADS, T, B_HEAD_DIM + K_AUG), lambda b, i: (b, 0, 0, 0))
    w_spec = pl.BlockSpec((B_HEAD_DIM, B_HEAD_DIM), lambda b, i: (0, 0))
    return pl.pallas_call(
        functools.partial(_nsa_prompt_kernel, seq=T, tq=tq),
        out_shape=jax.ShapeDtypeStruct((B, T, B_WIDTH), f32),
        grid=(B, T // tq),
        in_specs=[pl.BlockSpec((None, B_WIDTH, tq), lambda b, i: (b, 0, i)),
                  per_b4, per_b3(B_KV_WIDTH, T), per_b4, per_b3(B_KV_WIDTH, T),
                  per_b3(n_ch, 2 * B_KV_WIDTH), per_b3(n_ch, 2 * B_KV_WIDTH), w_spec, w_spec,
                  pl.BlockSpec((None, tq, 128), lambda b, i: (b, i, C_NG // 128)),
                  pl.BlockSpec((None, tq, B_WIDTH), lambda b, i: (b, i, C_GB // B_WIDTH))],
        out_specs=pl.BlockSpec((None, tq, B_WIDTH), lambda b, i: (b, i, 0)),
        scratch_shapes=[pltpu.VMEM((B_KV_HEADS, n_ch, B_HEAD_DIM), bf16),
                        pltpu.VMEM((B_KV_HEADS, B_HEAD_DIM, n_ch), bf16),
                        pltpu.VMEM((n_sel, 1, B_KV_HEADS * tq), f32),
                        pltpu.VMEM((n_sel, 1, B_KV_HEADS * tq), f32),
                        pltpu.VMEM((B_KV_HEADS, B_HEAD_DIM, lanes), f32),
                        pltpu.VMEM((B_KV_HEADS, B_HEAD_DIM + K_AUG, lanes), bf16),
                        pltpu.VMEM((2 * B_KV_HEADS, 1, lanes), f32),
                        pltpu.VMEM((2 * B_KV_HEADS, 1, lanes), f32),
                        pltpu.VMEM((2 * B_KV_HEADS, B_HEAD_DIM, lanes), f32),
                        pltpu.VMEM((B_WIDTH, tq), f32)],
        compiler_params=_cparams(("parallel", "arbitrary")),
        name="nsa_prompt",
    )(qt, ks, vst, kw, vwt, a, b, wck, wcv, u3, u3)


def _softmax_rows_update(sc, m, l, acc, pv_fn):
    m_new = jnp.maximum(m, jnp.max(sc, axis=-1, keepdims=True))
    alpha = jnp.exp(m - m_new)
    p = jnp.where(sc > 0.5 * NEG, jnp.exp(sc - m_new), 0.0)
    return m_new, l * alpha + jnp.sum(p, axis=-1, keepdims=True), acc * alpha + pv_fn(p)


def _nsa_sample_kernel(pt_ref, *refs, n_pages, pages_per_step, t_new, t_pad):
    P = pages_per_step
    cmp_pages, sel_pages = refs[:P], refs[P:2 * P]
    (win_ref, q_ref, news_ref, neww_ref, pw0_ref, pw1_ref, pool_ref, wck_ref, wcv_ref, gates_ref, gb_ref,
     o_ref, at_ref, bt_ref, qg_ref, ocmp_ref, keep_ref, m_ref, l_ref, acc_ref) = refs[2 * P:]
    s = pl.program_id(1)
    S = n_pages // P
    G = B_KV_HEADS
    gs = range(G)
    past = n_pages * PAGE_SIZE
    n_ch = past // CMP_STRIDE
    n_sel = past // SEL_BLOCK + 1
    n_j = keep_ref.shape[1]
    R = B_GROUP * t_pad
    row = lax.broadcasted_iota(jnp.int32, (R, 1), 0)
    t_col = (past + row % t_pad).astype(f32)
    slope_col = []
    for g in gs:
        sc_ = jnp.zeros((R, 1), f32)
        for r in range(B_GROUP):
            sc_ = jnp.where(row // t_pad == r, _alibi_slope(g * B_GROUP + r), sc_)
        slope_col.append(sc_)
    ksl = lambda g: slice(g * 64, (g + 1) * 64)
    vsl = lambda g: slice(B_KV_WIDTH + g * 64, B_KV_WIDTH + (g + 1) * 64)

    @pl.when(s < S)
    def _():
        pw0, pw1 = pw0_ref[...], pw1_ref[...]
        xa = jnp.concatenate([(cmp_pages[k][...] * pw0).astype(bf16) for k in range(P)], axis=1)
        xb = jnp.concatenate([(cmp_pages[k][...] * pw1).astype(bf16) for k in range(P)], axis=1)
        n_step = P * PAGE_SIZE // CMP_STRIDE
        off = pl.multiple_of(s * n_step, n_step)
        at_ref[:, pl.ds(off, n_step)] = jnp.dot(xa, pool_ref[...], preferred_element_type=f32)
        bt_ref[:, pl.ds(off, n_step)] = jnp.dot(xb, pool_ref[...], preferred_element_type=f32)

    @pl.when(s == S - 1)
    def _():
        pooled = at_ref[...] + pltpu.roll(bt_ref[...], n_ch - 1, 1)
        q = q_ref[...] * ATTN_SCALE
        qg = [jnp.concatenate([q[:, (g * B_GROUP + r) * 64:(g * B_GROUP + r + 1) * 64] for r in range(B_GROUP)],
                              axis=0).astype(bf16) for g in gs]
        kct = [_mm_tn(wck_ref[...], pooled[ksl(g), :]).astype(bf16) for g in gs]
        vct = [_mm_tn(wcv_ref[...], pooled[vsl(g), :]).astype(bf16) for g in gs]
        blk_end = (lax.broadcasted_iota(jnp.int32, (1, n_ch), 1) * CMP_STRIDE + (CMP_BLOCK - 1)).astype(f32)
        dist = t_col - blk_end
        mask = dist >= 0.0
        sc = [jnp.dot(qg[g], kct[g], preferred_element_type=f32) for g in gs]
        sc = [jnp.where(mask, sc[g] - slope_col[g] * dist, NEG) for g in gs]
        mx = [jnp.max(sc[g], axis=-1, keepdims=True) for g in gs]
        pr = [jnp.where(mask, jnp.exp(sc[g] - mx[g]), 0.0) for g in gs]
        inv = [1.0 / jnp.maximum(jnp.sum(pr[g], axis=-1, keepdims=True), 1e-30) for g in gs]
        pr = [pr[g] * inv[g] for g in gs]
        for g in gs:
            qg_ref[g] = qg[g]
            ocmp_ref[g] = _mm_nt(pr[g], vct[g])
            m_ref[g] = jnp.full((R, 1), NEG, f32)
            l_ref[g] = jnp.zeros((R, 1), f32)
            acc_ref[g] = jnp.zeros((R, B_HEAD_DIM), f32)
        psum = [pr[g][0:t_pad] for g in gs]
        for r in range(1, B_GROUP):
            psum = [psum[g] + pr[g][r * t_pad:(r + 1) * t_pad] for g in gs]
        psum = jnp.concatenate(psum, axis=0)
        n_i = lax.broadcasted_iota(jnp.int32, (n_ch, n_j), 0) * CMP_STRIDE
        j_i = lax.broadcasted_iota(jnp.int32, (n_ch, n_j), 1) * SEL_BLOCK
        overlap = jnp.maximum(jnp.minimum(n_i + CMP_BLOCK, j_i + SEL_BLOCK) - jnp.maximum(n_i, j_i), 0).astype(f32) \
            * (1.0 / CMP_BLOCK)
        imp = jnp.dot(psum, overlap, preferred_element_type=f32, precision=HI)
        j = lax.broadcasted_iota(jnp.int32, (1, n_j), 1)
        tok = past + lax.broadcasted_iota(jnp.int32, (G * t_pad, 1), 0) % t_pad
        back = tok // SEL_BLOCK - j
        forced = (j == 0) | ((back >= 0) & (back < N_LOCAL))
        score = jnp.where(forced, FORCED_SCORE, jnp.where(back >= 0, imp, -1.0))
        score = jnp.where(j < n_sel, score, -2.0)
        rank = jnp.zeros(score.shape, f32)
        for i in range(n_sel):
            col = score[:, i:i + 1]
            rank = rank + jnp.where(col > score, 1.0, jnp.where(col == score, jnp.where(j > i, 1.0, 0.0), 0.0))
        keep_ref[...] = jnp.where(rank < float(min(N_SELECT, n_sel)), jnp.where(score >= 0.0, 1.0, 0.0), 0.0)

    @pl.when(s >= S)
    def _():
        ss = s - S
        n_blk = P * PAGE_SIZE // SEL_BLOCK
        n_key = P * PAGE_SIZE
        pick = (lax.broadcasted_iota(jnp.int32, (n_j, n_blk), 0)
                == lax.broadcasted_iota(jnp.int32, (n_j, n_blk), 1) + ss * n_blk).astype(bf16)
        keep_blk = jnp.dot(keep_ref[...].astype(bf16), pick, preferred_element_type=f32)
        expand = (lax.broadcasted_iota(jnp.int32, (n_blk, n_key), 0)
                  == lax.broadcasted_iota(jnp.int32, (n_blk, n_key), 1) // SEL_BLOCK).astype(bf16)
        bias = (jnp.dot(keep_blk.astype(bf16), expand, preferred_element_type=f32) - 1.0) * (-NEG)
        kpos = (ss * n_key + lax.broadcasted_iota(jnp.int32, (1, n_key), 1)).astype(f32)
        dist = t_col - kpos
        kt = [jnp.concatenate([sel_pages[k][ksl(g), :].astype(bf16) for k in range(P)], axis=1) for g in gs]
        vt = [jnp.concatenate([sel_pages[k][vsl(g), :].astype(bf16) for k in range(P)], axis=1) for g in gs]
        sc = [jnp.dot(qg_ref[g], kt[g], preferred_element_type=f32) for g in gs]
        sc = [sc[g] - slope_col[g] * dist + jnp.concatenate([bias[g * t_pad:(g + 1) * t_pad]] * B_GROUP, axis=0)
              for g in gs]
        m_new = [jnp.maximum(m_ref[g], jnp.max(sc[g], axis=-1, keepdims=True)) for g in gs]
        alpha = [jnp.exp(m_ref[g] - m_new[g]) for g in gs]
        pr = [jnp.where(sc[g] > 0.5 * NEG, jnp.exp(sc[g] - m_new[g]), 0.0) for g in gs]
        l_new = [l_ref[g] * alpha[g] + jnp.sum(pr[g], axis=-1, keepdims=True) for g in gs]
        pv = [_mm_nt(pr[g], vt[g]) for g in gs]
        for g in gs:
            acc_ref[g] = acc_ref[g] * alpha[g] + pv[g]
            m_ref[g] = m_new[g]
            l_ref[g] = l_new[g]

    @pl.when(s == 2 * S - 1)
    def _():
        xs = news_ref[...]
        xw = neww_ref[...]
        win = win_ref[...]
        n_win = win.shape[1]
        new_dist = t_col - (past + lax.broadcasted_iota(jnp.int32, (1, t_pad), 1)).astype(f32)
        win_dist = t_col - (past - n_win + lax.broadcasted_iota(jnp.int32, (1, n_win), 1)).astype(f32)
        keep_new = keep_ref[:, n_sel - 1:n_sel]
        gates = jax.nn.sigmoid(gates_ref[...])
        gb = gb_ref[...]
        silu_gb = gb * jax.nn.sigmoid(gb)
        qg = [qg_ref[g] for g in gs]
        sc = [_mm_nt(qg[g], xs[:, ksl(g)]) - slope_col[g] * new_dist for g in gs]
        sc = [jnp.where(new_dist >= 0.0, sc[g], NEG)
              + (jnp.concatenate([keep_new[g * t_pad:(g + 1) * t_pad]] * B_GROUP, axis=0) - 1.0) * (-NEG) for g in gs]
        sel = [_softmax_rows_update(sc[g], m_ref[g], l_ref[g], acc_ref[g], lambda p, g=g: _mm(p, xs[:, vsl(g)]))
               for g in gs]
        o_sel = [sel[g][2] * (1.0 / jnp.maximum(sel[g][1], 1e-30)) for g in gs]
        sw = [jnp.dot(qg[g], win[ksl(g), :].astype(bf16), preferred_element_type=f32) - slope_col[g] * win_dist
              for g in gs]
        sw = [jnp.where(win_dist <= float(WINDOW), sw[g], NEG) for g in gs]
        sn = [_mm_nt(qg[g], xw[:, ksl(g)]) - slope_col[g] * new_dist for g in gs]
        sn = [jnp.where(new_dist >= 0.0, jnp.where(new_dist <= float(WINDOW), sn[g], NEG), NEG) for g in gs]
        mw = [jnp.maximum(jnp.max(sw[g], axis=-1, keepdims=True), jnp.max(sn[g], axis=-1, keepdims=True)) for g in gs]
        pw = [jnp.where(sw[g] > 0.5 * NEG, jnp.exp(sw[g] - mw[g]), 0.0) for g in gs]
        pn = [jnp.where(sn[g] > 0.5 * NEG, jnp.exp(sn[g] - mw[g]), 0.0) for g in gs]
        lw = [jnp.sum(pw[g], axis=-1, keepdims=True) + jnp.sum(pn[g], axis=-1, keepdims=True) for g in gs]
        o_win = [(_mm_nt(pw[g], win[vsl(g), :]) + _mm(pn[g], xw[:, vsl(g)])) * (1.0 / jnp.maximum(lw[g], 1e-30))
                 for g in gs]
        for g in gs:
            def gate_col(branch):
                base = branch * B_HEADS + g * B_GROUP
                return jnp.concatenate([gates[:, base + r:base + r + 1] for r in range(B_GROUP)], axis=0)

            o = gate_col(0) * ocmp_ref[g] + gate_col(1) * o_sel[g] + gate_col(2) * o_win[g]
            for r in range(B_GROUP):
                cols = slice((g * B_GROUP + r) * 64, (g * B_GROUP + r + 1) * 64)
                o_ref[:, cols] = o[r * t_pad:(r + 1) * t_pad, :] * silu_gb[:, cols]


def _nsa_sample(u3, cache_cmp_t, cache_sel_t, cache_win_t, page_table, pw0_t, pw1_t, wck, wcv, *, t_new, pages_per_step):
    DB, t_pad, _ = u3.shape
    n_pages = page_table.shape[1]
    P = pages_per_step
    assert t_new < CMP_STRIDE and t_new <= t_pad and n_pages % P == 0 and (P * PAGE_SIZE // CMP_STRIDE) % 128 == 0
    S = n_pages // P
    past = n_pages * PAGE_SIZE
    n_ch = past // CMP_STRIDE
    n_sel = past // SEL_BLOCK + 1
    n_j = -(-n_sel // 128) * 128
    R = B_GROUP * t_pad
    kvw = 2 * B_KV_WIDTH
    pool = (lax.broadcasted_iota(jnp.int32, (P * PAGE_SIZE, P * PAGE_SIZE // CMP_STRIDE), 0) // CMP_STRIDE
            == lax.broadcasted_iota(jnp.int32, (P * PAGE_SIZE, P * PAGE_SIZE // CMP_STRIDE), 1)).astype(bf16)
    ublk = lambda w, off: pl.BlockSpec((None, t_pad, w), lambda b, s, pt: (b, 0, off // w))
    const = lambda s1, s2: pl.BlockSpec((s1, s2), lambda b, s, pt: (0, 0))
    page = lambda fn: pl.BlockSpec((None, kvw, PAGE_SIZE), fn)
    cmp_specs = [page(lambda b, s, pt, k=k: (pt[b, jnp.minimum(s, S - 1) * P + k], 0, 0)) for k in range(P)]
    sel_specs = [page(lambda b, s, pt, k=k: (pt[b, jnp.maximum(s - S, 0) * P + k], 0, 0)) for k in range(P)]
    grid_spec = pltpu.PrefetchScalarGridSpec(
        num_scalar_prefetch=1,
        grid=(DB, 2 * S),
        in_specs=cmp_specs + sel_specs + [
            pl.BlockSpec((None, kvw, cache_win_t.shape[2]), lambda b, s, pt: (b, 0, 0)),
            ublk(B_WIDTH, C_Q), ublk(kvw, C_KVS), ublk(kvw, C_KVW),
            const(kvw, PAGE_SIZE), const(kvw, PAGE_SIZE), const(*pool.shape),
            const(B_HEAD_DIM, B_HEAD_DIM), const(B_HEAD_DIM, B_HEAD_DIM),
            ublk(128, C_NG), ublk(B_WIDTH, C_GB)],
        out_specs=pl.BlockSpec((None, t_pad, B_WIDTH), lambda b, s, pt: (b, 0, 0)),
        scratch_shapes=[pltpu.VMEM((kvw, n_ch), f32), pltpu.VMEM((kvw, n_ch), f32),
                        pltpu.VMEM((B_KV_HEADS, R, B_HEAD_DIM), bf16),
                        pltpu.VMEM((B_KV_HEADS, R, B_HEAD_DIM), f32),
                        pltpu.VMEM((B_KV_HEADS * t_pad, n_j), f32),
                        pltpu.VMEM((B_KV_HEADS, R, 1), f32), pltpu.VMEM((B_KV_HEADS, R, 1), f32),
                        pltpu.VMEM((B_KV_HEADS, R, B_HEAD_DIM), f32)])
    return pl.pallas_call(
        functools.partial(_nsa_sample_kernel, n_pages=n_pages, pages_per_step=P, t_new=t_new, t_pad=t_pad),
        out_shape=jax.ShapeDtypeStruct((DB, t_pad, B_WIDTH), f32),
        grid_spec=grid_spec,
        compiler_params=_cparams(("parallel", "arbitrary")),
        name="nsa_sample",
    )(page_table, *([cache_cmp_t] * P), *([cache_sel_t] * P), cache_win_t, u3, u3, u3, pw0_t, pw1_t, pool, wck, wcv,
      u3, u3)


def _pos_weight_tiles(pos_k, pos_v, rows):
    def half(lo):
        t = jnp.concatenate([jnp.tile(pos_k[lo:lo + CMP_STRIDE], (1, B_KV_HEADS)),
                             jnp.tile(pos_v[lo:lo + CMP_STRIDE], (1, B_KV_HEADS))], axis=1)
        return jnp.tile(t, (rows // CMP_STRIDE, 1))
    return half(0), half(CMP_STRIDE)


def _a_cols(u_rows):
    return jnp.concatenate([u_rows[..., :4 * A_WIDTH], u_rows[..., C_LR:C_LR + DECAY_RANK + ICLR_RANK]], axis=-1)


def kernel(x_prompt, x_sample, cache_cmp_kv, cache_sel_kv, cache_win_kv, state_rwkv, state_shift, page_table, norm_in,
           w_in, mu_shift, w0, w_up, a0, a_up, k_k, k_a, r_k, gn_w, gn_b, cmp_pos_k, cmp_pos_v, w_cmp_k, w_cmp_v,
           w_pa, w_pb, w_o, norm_out):
    assert w_in.shape[0] == 1, "one layer"
    B, T, _ = x_prompt.shape
    DB, TS, _ = x_sample.shape
    H, N = A_HEADS, A_HEAD_DIM
    kvs = (2, B_KV_HEADS, B_HEAD_DIM)
    p = dict(mu_shift=mu_shift[0], w0=w0[0], w_up=w_up[0], a0=a0[0], a_up=a_up[0], k_k=k_k[0], k_a=k_a[0],
             r_k=r_k[0].reshape(-1), gn_w=gn_w[0], gn_b=gn_b[0])
    w_packed = _pack_w_in(w_in[0])
    wpa, wpb, wo = w_pa[0].astype(bf16), w_pb[0].astype(bf16), w_o[0].astype(bf16)

    xp = x_prompt.reshape(B * T, D_MODEL)
    up = _proj_in(xp, norm_in[0], w_packed)
    up3 = up.reshape(B, T, U_COLS)
    oa_p, s_p = _rwkv_wide(up3, jnp.zeros((B, 1, 4 * A_WIDTH), f32), jnp.zeros((B, 1, 128), f32),
                      jnp.zeros((B, H, N, N), f32), p, chunk=64, t_valid=64, rows=math.gcd(B, 4))
    tt = 512
    pw0, pw1 = _pos_weight_tiles(cmp_pos_k[0], cmp_pos_v[0], tt)
    qt, ks, vst, kw, vwt, pa, pb = _nsa_prep(up3, pw0, pw1, tt)
    ob_p = _nsa_prompt(up3, qt, ks, vst, kw, vwt, pa, pb, w_cmp_k[0], w_cmp_v[0], 128)
    y_p = _merge(xp, oa_p.reshape(B * T, A_WIDTH), ob_p.reshape(B * T, B_WIDTH), up, wpa, wpb, wo, norm_out)

    t_pad = 8
    xs = x_sample.reshape(DB * TS, D_MODEL)
    us = _proj_in(xs, norm_in[0], w_packed)
    us3 = us.reshape(DB, TS, U_COLS)
    us3p = jnp.pad(us3, ((0, 0), (0, t_pad - TS), (0, 0)))
    shift0 = state_shift[0]
    oa_s, s_s = _rwkv_wide(us3p, shift0[:, None, :4 * A_WIDTH], shift0[:, None, 4 * A_WIDTH:], state_rwkv[0], p,
                      chunk=t_pad, t_valid=TS, rows=2 if DB % 2 == 0 else 1)
    n_pool = cache_cmp_kv.shape[1]
    win = cache_win_kv[0]
    rows_last = lambda c, lead: jnp.transpose(c, (0, 2, 3, 4, 1)).reshape(lead, 2 * B_KV_WIDTH, c.shape[1])
    ob_s = _nsa_sample(us3p, rows_last(cache_cmp_kv[0], n_pool), rows_last(cache_sel_kv[0], n_pool),
                       rows_last(win, DB), page_table, pw0[:PAGE_SIZE].T, pw1[:PAGE_SIZE].T, w_cmp_k[0], w_cmp_v[0],
                       t_new=TS, pages_per_step=min(16, page_table.shape[1]))
    y_s = _merge(xs, oa_s[:, :TS].reshape(DB * TS, A_WIDTH), ob_s[:, :TS].reshape(DB * TS, B_WIDTH), us, wpa, wpb, wo,
                 norm_out)

    def kv_out(u3_, col, lead, t):
        return u3_[..., col:col + 2 * B_KV_WIDTH].reshape((1, lead, t) + kvs)

    wk = min(WINDOW, T)
    new_w_s = kv_out(us3, C_KVW, DB, TS)[0]
    s_win = jnp.concatenate([win, new_w_s], axis=1)[:, TS:][None]
    return (y_p.reshape(B, T, D_MODEL), y_s.reshape(DB, TS, D_MODEL),
            kv_out(up3, C_KVC, B, T), kv_out(up3, C_KVS, B, T), kv_out(up3[:, T - wk:], C_KVW, B, wk),
            s_p[None], _a_cols(up3[:, T - 1])[None],
            kv_out(us3, C_KVC, DB, TS), kv_out(us3, C_KVS, DB, TS), s_win,
            s_s[None], _a_cols(us3[:, TS - 1])[None])
```

```python
import functools
import math

import jax
import jax.numpy as jnp
from jax import lax
from jax.experimental import pallas as pl
from jax.experimental.pallas import tpu as pltpu

f32 = jnp.float32
bf16 = jnp.bfloat16

D_MODEL = 2048
PAGE_SIZE = 128
A_HEADS = 16
A_HEAD_DIM = 64
A_WIDTH = A_HEADS * A_HEAD_DIM
DECAY_RANK = 64
ICLR_RANK = 64
A_COLS = 4 * A_WIDTH + DECAY_RANK + ICLR_RANK
GN_EPS = 64e-5
B_HEADS = 16
B_KV_HEADS = 4
B_GROUP = B_HEADS // B_KV_HEADS
B_HEAD_DIM = 64
B_WIDTH = B_HEADS * B_HEAD_DIM
B_KV_WIDTH = B_KV_HEADS * B_HEAD_DIM
B_COLS = 2 * B_WIDTH + 6 * B_KV_WIDTH + 3 * B_HEADS
CMP_BLOCK = 32
CMP_STRIDE = 16
SEL_BLOCK = 64
N_SELECT = 16
N_LOCAL = 2
WINDOW = 512
FORCED_SCORE = 1e4
ATTN_SCALE = B_HEAD_DIM ** -0.5
RMS_EPS = 1e-6
NEG = -1e30
LOG2E = math.log2(math.e)

C_RKVG = 0
C_Q = 4096
C_GB = 5120
C_GA_M = 6144
C_GB_M = 8192
C_KVC = 10240
C_KVS = 10752
C_KVW = 11264
C_LR = 11776
C_NG = 11904
U_COLS = 12288

VMEM_LIMIT = 56 * 1024 * 1024
HI = lax.Precision.HIGHEST


def _cparams(sem):
    return pltpu.CompilerParams(dimension_semantics=sem, vmem_limit_bytes=VMEM_LIMIT)


def _pack_w_in(w):
    a, b, m = w[:, :A_COLS], w[:, A_COLS:A_COLS + B_COLS], w[:, A_COLS + B_COLS:]
    z = jnp.zeros((w.shape[0], U_COLS - C_NG - 3 * B_HEADS), w.dtype)
    return jnp.concatenate(
        [a[:, :4 * A_WIDTH], b[:, :2 * B_WIDTH], m, b[:, 2 * B_WIDTH:2 * B_WIDTH + 6 * B_KV_WIDTH],
         a[:, 4 * A_WIDTH:], b[:, 2 * B_WIDTH + 6 * B_KV_WIDTH:], z], axis=1).astype(bf16)


def _proj_in_kernel(x_ref, g_ref, w_ref, o_ref, xn_ref):
    @pl.when(pl.program_id(1) == 0)
    def _():
        x = x_ref[...]
        ms = jnp.mean(x * x, axis=-1, keepdims=True)
        xn_ref[...] = (x * lax.rsqrt(ms + RMS_EPS) * g_ref[...]).astype(bf16)

    o_ref[...] = jnp.dot(xn_ref[...], w_ref[...], preferred_element_type=f32)


def _proj_in(x2d, norm_g, w_packed):
    m = x2d.shape[0]
    tm = min(1024, m)
    tn = 1024
    return pl.pallas_call(
        _proj_in_kernel,
        out_shape=jax.ShapeDtypeStruct((m, U_COLS), f32),
        grid=(m // tm, U_COLS // tn),
        in_specs=[pl.BlockSpec((tm, D_MODEL), lambda i, j: (i, 0)),
                  pl.BlockSpec((1, D_MODEL), lambda i, j: (0, 0)),
                  pl.BlockSpec((D_MODEL, tn), lambda i, j: (0, j))],
        out_specs=pl.BlockSpec((tm, tn), lambda i, j: (i, j)),
        scratch_shapes=[pltpu.VMEM((tm, D_MODEL), bf16)],
        compiler_params=_cparams(("parallel", "arbitrary")),
        name="proj_in",
    )(x2d, norm_g.reshape(1, D_MODEL), w_packed)


def _mm(a, b):
    return jnp.dot(a.astype(bf16), b.astype(bf16), preferred_element_type=f32)


def _mm_nt(a, b):
    return lax.dot_general(a.astype(bf16), b.astype(bf16), (((1,), (1,)), ((), ())), preferred_element_type=f32)


def _mm_tn(a, b):
    return lax.dot_general(a.astype(bf16), b.astype(bf16), (((0,), (0,)), ((), ())), preferred_element_type=f32)


HEADS_PER_GROUP = 4
N_HEAD_GROUPS = A_HEADS // HEADS_PER_GROUP
GROUP_WIDTH = HEADS_PER_GROUP * A_HEAD_DIM


def _rwkv_wide_kernel(um_ref, ulr_ref, pm_ref, plr_ref, s0_ref, mum_ref, mulr_ref, w0_ref, wup_ref, a0_ref,
                      aup_ref, kk_ref, ka_ref, rk_ref, gnw_ref, gnb_ref, o_ref, sout_ref,
                      s_ref, prevm_ref, prevlr_ref, *, chunk, t_valid):
    C = chunk
    N, HG, NG, GW = A_HEAD_DIM, HEADS_PER_GROUP, N_HEAD_GROUPS, GROUP_WIDTH
    RB = um_ref.shape[0]
    c = pl.program_id(1)

    @pl.when(c == 0)
    def _():
        s_ref[...] = s0_ref[...]
        prevm_ref[...] = pm_ref[...]
        prevlr_ref[...] = plr_ref[...]

    row = lax.broadcasted_iota(jnp.int32, (C, 1), 0)
    tril_b = (lax.broadcasted_iota(jnp.int32, (C, C), 0) >= lax.broadcasted_iota(jnp.int32, (C, C), 1)).astype(bf16)
    t_i = lax.broadcasted_iota(jnp.int32, (C, HG * C), 0)
    s_i = lax.broadcasted_iota(jnp.int32, (C, HG * C), 1) % C
    tril_incl = t_i >= s_i
    tril_strict = t_i > s_i
    eye = (t_i == s_i).astype(f32)
    head_of_dim = lax.broadcasted_iota(jnp.int32, (1, GW), 1) // N
    head_of_tok = lax.broadcasted_iota(jnp.int32, (1, HG * C), 1) // C
    ones_bd = (lax.broadcasted_iota(jnp.int32, (GW, GW), 0) // N
               == lax.broadcasted_iota(jnp.int32, (GW, GW), 1) // N).astype(bf16)
    n_sq = int(math.log2(C)) - 1
    valid = (row < t_valid).astype(f32) if t_valid < C else None
    gcols = [slice(q * GW, (q + 1) * GW) for q in range(NG)]
    qs = range(NG)

    def shifted(u, prev):
        return jnp.where(row == 0, prev, pltpu.roll(u, 1, 0))

    def head_sums(*xs):
        stacked = jnp.concatenate([x.astype(bf16)[:, gc] for x in xs for gc in gcols], axis=0)
        sums = jnp.dot(stacked, ones_bd, preferred_element_type=f32)
        return [jnp.concatenate([sums[(i * NG + q) * C:(i * NG + q + 1) * C] for q in qs], axis=1)
                for i in range(len(xs))]

    def block_rows(x, head_of_lane):
        return jnp.concatenate([jnp.where(head_of_lane == h, x, 0.0) for h in range(HG)], axis=0).astype(bf16)

    def mm(a, b):
        return jnp.dot(a.astype(bf16), b, preferred_element_type=f32)

    def mm_nt(a, b):
        return lax.dot_general(a.astype(bf16), b, (((1,), (1,)), ((), ())), preferred_element_type=f32)

    def prepare(bi, out):
        um = um_ref[bi]
        ulr = ulr_ref[bi]
        prev_m = prevm_ref[bi]
        prev_lr = prevlr_ref[bi]
        prevm_ref[bi] = um[C - 1:C, :]
        prevlr_ref[bi] = ulr[C - 1:C, :]
        uslr = ulr + mulr_ref[...] * (shifted(ulr, prev_lr) - ulr)
        zw = w0_ref[...] + _mm(jnp.tanh(uslr[:, :DECAY_RANK]), wup_ref[...])
        a = jax.nn.sigmoid(a0_ref[...] + _mm(uslr[:, DECAY_RANK:], aup_ref[...]))
        yield
        secs = []
        for i in range(4):
            cols = slice(i * A_WIDTH, (i + 1) * A_WIDTH)
            secs.append(um[:, cols] + mum_ref[:, cols] * (shifted(um[:, cols], prev_m[:, cols]) - um[:, cols]))
            yield
        r, k, v, g = secs
        logw = -math.exp(-0.5) * jax.nn.sigmoid(zw)
        if valid is not None:
            logw = logw * valid
        yield
        logw_hi = logw.astype(bf16)
        logw_lo = (logw - logw_hi.astype(f32)).astype(bf16)
        cum = (jnp.dot(tril_b, logw_hi, preferred_element_type=f32)
               + jnp.dot(tril_b, logw_lo, preferred_element_type=f32))
        yield
        e_pos = jnp.exp(cum)
        e_neg = jnp.exp(-cum)
        yield
        e_prev = jnp.exp(cum - logw)
        kk = k * kk_ref[...]
        yield
        k2 = k * (1.0 + (a - 1.0) * ka_ref[...])
        yield
        ssq, bonus = head_sums(kk * kk, r * k2 * rk_ref[...])
        yield
        kk = kk * lax.rsqrt(jnp.maximum(ssq, 1e-24))
        kn = k2 * e_neg
        bn = kk * a * e_neg
        if valid is not None:
            kn = kn * valid
            bn = bn * valid
        yield
        lhs = jnp.concatenate([kk * e_prev, r * e_pos], axis=0).astype(bf16)
        out.update(v=v, g=g, bonus=bonus, kn=kn, bn=bn, lhs=lhs, e_last=e_pos[C - 1:C, :])
        yield

    def solve(bi, pre, out):
        v = [pre["v"][:, gc] for gc in gcols]
        kn = [pre["kn"][:, gc] for gc in gcols]
        bn = [pre["bn"][:, gc] for gc in gcols]
        lhs = [pre["lhs"][:, gc] for gc in gcols]
        s0 = [s_ref[bi, q] for q in qs]
        qk_b = [mm_nt(lhs[q], block_rows(bn[q], head_of_dim)) for q in qs]
        yield
        qk_k = [mm_nt(lhs[q], block_rows(kn[q], head_of_dim)) for q in qs]
        yield
        w0s = [mm_nt(lhs[q], block_rows(s0[q], head_of_dim)) for q in qs]
        yield
        pw = [jnp.where(tril_strict, -qk_b[q][:C], 0.0) for q in qs]
        tinv = [eye + pw[q] for q in qs]
        bd = [block_rows(pw[q], head_of_tok) for q in qs]
        for _ in range(n_sq):
            pw = [mm(pw[q], bd[q]) for q in qs]
            yield
            bd = [block_rows(pw[q], head_of_tok) for q in qs]
            tinv = [tinv[q] + mm(tinv[q], bd[q]) for q in qs]
            yield
        bd_v = [block_rows(v[q], head_of_dim) for q in qs]
        rhs_u = [w0s[q][:C] + mm(jnp.where(tril_strict, qk_k[q][:C], 0.0), bd_v[q]) for q in qs]
        yield
        u = [mm(tinv[q], block_rows(rhs_u[q], head_of_dim)) for q in qs]
        yield
        p_cat = [jnp.concatenate([jnp.where(tril_incl, qk_k[q][C:], 0.0), jnp.where(tril_incl, -qk_b[q][C:], 0.0)],
                                 axis=1) for q in qs]
        out["o"] = [w0s[q][C:] + mm(p_cat[q], jnp.concatenate([bd_v[q], block_rows(u[q], head_of_dim)], axis=0))
                    for q in qs]
        yield
        full = [lax.dot_general(jnp.concatenate([v[q], -u[q]], axis=0).astype(bf16),
                                jnp.concatenate([kn[q], bn[q]], axis=0).astype(bf16),
                                (((0,), (0,)), ((), ())), preferred_element_type=f32) for q in qs]
        yield
        for q in qs:
            upd = jnp.where(head_of_dim == 0, full[q][0:N], 0.0)
            for h in range(1, HG):
                upd = upd + jnp.where(head_of_dim == h, full[q][h * N:(h + 1) * N], 0.0)
            s_ref[bi, q] = (s0[q] + upd) * pre["e_last"][:, gcols[q]]
        yield

    def finish(bi, pre, mid):
        o = jnp.concatenate(mid["o"], axis=1)
        mean = head_sums(o)[0] * (1.0 / N)
        yield
        d = o - mean
        var = head_sums(d * d)[0] * (1.0 / N)
        yield
        g = pre["g"]
        on = d * lax.rsqrt(var + GN_EPS) * gnw_ref[...] + gnb_ref[...] + pre["bonus"] * pre["v"]
        o_ref[bi] = on * (g * jax.nn.sigmoid(g))
        yield

    def interleave(gens):
        gens = list(gens)
        while gens:
            for gen in list(gens):
                if next(gen, "done") == "done":
                    gens.remove(gen)

    pre = [{} for _ in range(RB)]
    mid = [{} for _ in range(RB)]
    halves = [range(0, RB // 2), range(RB // 2, RB)] if RB > 1 else [range(RB)]
    prep = lambda rows_: [prepare(bi, pre[bi]) for bi in rows_]
    solv = lambda rows_: [solve(bi, pre[bi], mid[bi]) for bi in rows_]
    fin = lambda rows_: [finish(bi, pre[bi], mid[bi]) for bi in rows_]
    interleave(prep(halves[0]))
    for i, rows_ in enumerate(halves):
        interleave(solv(rows_) + (prep(halves[i + 1]) if i + 1 < len(halves) else [])
                   + (fin(halves[i - 1]) if i >= 1 else []))
    interleave(fin(halves[-1]))

    @pl.when(c == pl.num_programs(1) - 1)
    def _():
        sout_ref[...] = s_ref[...]


def _state_to_wide(s):
    b = s.shape[0]
    return s.reshape(b, N_HEAD_GROUPS, HEADS_PER_GROUP, A_HEAD_DIM, A_HEAD_DIM).transpose(0, 1, 3, 2, 4).reshape(
        b, N_HEAD_GROUPS, A_HEAD_DIM, GROUP_WIDTH)


def _state_from_wide(s):
    b = s.shape[0]
    return s.reshape(b, N_HEAD_GROUPS, A_HEAD_DIM, HEADS_PER_GROUP, A_HEAD_DIM).transpose(0, 1, 3, 2, 4).reshape(
        b, A_HEADS, A_HEAD_DIM, A_HEAD_DIM)


def _rwkv_wide(u3, prev_m, prev_lr, s0, p, *, chunk, t_valid, rows):
    B, T, _ = u3.shape
    C = chunk
    NG, N, GW = N_HEAD_GROUPS, A_HEAD_DIM, GROUP_WIDTH
    assert B % rows == 0 and T % C == 0
    row = lambda v: v.reshape(1, -1)
    vec = lambda n: pl.BlockSpec((1, n), lambda b, c: (0, 0))
    in_specs = [
        pl.BlockSpec((rows, C, 4 * A_WIDTH), lambda b, c: (b, c, 0)),
        pl.BlockSpec((rows, C, 128), lambda b, c: (b, c, C_LR // 128)),
        pl.BlockSpec((rows, 1, 4 * A_WIDTH), lambda b, c: (b, 0, 0)),
        pl.BlockSpec((rows, 1, 128), lambda b, c: (b, 0, 0)),
        pl.BlockSpec((rows, NG, N, GW), lambda b, c: (b, 0, 0, 0)),
        vec(4 * A_WIDTH), vec(128), vec(A_WIDTH),
        pl.BlockSpec((DECAY_RANK, A_WIDTH), lambda b, c: (0, 0)),
        vec(A_WIDTH),
        pl.BlockSpec((ICLR_RANK, A_WIDTH), lambda b, c: (0, 0)),
        vec(A_WIDTH), vec(A_WIDTH), vec(A_WIDTH), vec(A_WIDTH), vec(A_WIDTH),
    ]
    o_a, s_out = pl.pallas_call(
        functools.partial(_rwkv_wide_kernel, chunk=C, t_valid=t_valid),
        out_shape=(jax.ShapeDtypeStruct((B, T, A_WIDTH), f32), jax.ShapeDtypeStruct((B, NG, N, GW), f32)),
        grid=(B // rows, T // C),
        in_specs=in_specs,
        out_specs=(pl.BlockSpec((rows, C, A_WIDTH), lambda b, c: (b, c, 0)),
                   pl.BlockSpec((rows, NG, N, GW), lambda b, c: (b, 0, 0, 0))),
        scratch_shapes=[pltpu.VMEM((rows, NG, N, GW), f32), pltpu.VMEM((rows, 1, 4 * A_WIDTH), f32),
                        pltpu.VMEM((rows, 1, 128), f32)],
        compiler_params=_cparams(("parallel", "arbitrary")),
        name="rwkv7",
    )(u3, u3, prev_m, prev_lr, _state_to_wide(s0), row(p["mu_shift"][:4 * A_WIDTH]), row(p["mu_shift"][4 * A_WIDTH:]),
      row(p["w0"]), p["w_up"], row(p["a0"]), p["a_up"], row(p["k_k"]), row(p["k_a"]), row(p["r_k"]),
      row(p["gn_w"]), row(p["gn_b"]))
    return o_a, _state_from_wide(s_out)


def _merge_kernel(x_ref, oa_ref, ob_ref, ga_ref, gb_ref, wpa_ref, wpb_ref, wo_ref, gout_ref, y_ref):
    pa = jnp.dot(oa_ref[...].astype(bf16), wpa_ref[...], preferred_element_type=f32)
    pb = jnp.dot(ob_ref[...].astype(bf16), wpb_ref[...], preferred_element_type=f32)
    merged = jax.nn.sigmoid(ga_ref[...]) * pa + jax.nn.sigmoid(gb_ref[...]) * pb
    h = x_ref[...] + jnp.dot(merged.astype(bf16), wo_ref[...], preferred_element_type=f32)
    ms = jnp.mean(h * h, axis=-1, keepdims=True)
    y_ref[...] = h * lax.rsqrt(ms + RMS_EPS) * gout_ref[...]


def _merge(x2d, o_a, o_b, u2d, w_pa, w_pb, w_o, norm_out):
    m = x2d.shape[0]
    tm = min(256, m)
    const = lambda shape: pl.BlockSpec(shape, lambda i: (0, 0), pipeline_mode=pl.Buffered(1))
    return pl.pallas_call(
        _merge_kernel,
        out_shape=jax.ShapeDtypeStruct((m, D_MODEL), f32),
        grid=(m // tm,),
        in_specs=[pl.BlockSpec((tm, D_MODEL), lambda i: (i, 0)),
                  pl.BlockSpec((tm, A_WIDTH), lambda i: (i, 0)),
                  pl.BlockSpec((tm, B_WIDTH), lambda i: (i, 0)),
                  pl.BlockSpec((tm, D_MODEL), lambda i: (i, C_GA_M // D_MODEL)),
                  pl.BlockSpec((tm, D_MODEL), lambda i: (i, C_GB_M // D_MODEL)),
                  const((A_WIDTH, D_MODEL)), const((B_WIDTH, D_MODEL)), const((D_MODEL, D_MODEL)),
                  const((1, D_MODEL))],
        out_specs=pl.BlockSpec((tm, D_MODEL), lambda i: (i, 0)),
        compiler_params=_cparams(("parallel",)),
        name="merge_out",
    )(x2d, o_a, o_b, u2d, u2d, w_pa, w_pb, w_o, norm_out.reshape(1, D_MODEL))


def _alibi_slope(head):
    return 2.0 ** (-8.0 * (head + 1) / B_HEADS)


def _slope_row(g, tq):
    lane_head = lax.broadcasted_iota(jnp.int32, (1, B_GROUP * tq), 1) // tq
    out = jnp.zeros((1, B_GROUP * tq), f32)
    for r in range(B_GROUP):
        out = jnp.where(lane_head == r, _alibi_slope(g * B_GROUP + r), out)
    return out


def _tile_heads(row):
    return jnp.concatenate([row] * B_GROUP, axis=1)


def _overlap_t(n_sel_rows, n_ch):
    j = lax.broadcasted_iota(jnp.int32, (n_sel_rows, n_ch), 0) * SEL_BLOCK
    n = lax.broadcasted_iota(jnp.int32, (n_sel_rows, n_ch), 1) * CMP_STRIDE
    ov = jnp.minimum(n + CMP_BLOCK, j + SEL_BLOCK) - jnp.maximum(n, j)
    return jnp.maximum(ov, 0).astype(f32) * (1.0 / CMP_BLOCK)


M_INIT = 0.1 * NEG


def _attn_step_multi(s, carries, v_t):
    n = range(len(s))
    m_new = [jnp.maximum(carries[i][0], jnp.max(s[i], axis=0, keepdims=True)) for i in n]
    alpha = [jnp.exp2(carries[i][0] - m_new[i]) for i in n]
    p = [jnp.exp2(s[i] - m_new[i]) for i in n]
    l = [carries[i][1] * alpha[i] + jnp.sum(p[i], axis=0, keepdims=True) for i in n]
    pv = [jnp.dot(v_t[i], p[i].astype(bf16), preferred_element_type=f32) for i in n]
    return [(m_new[i], l[i], carries[i][2] * alpha[i] + pv[i]) for i in n]


def _attn_init(lanes):
    return (jnp.full((1, lanes), M_INIT, f32), jnp.zeros((1, lanes), f32), jnp.zeros((B_HEAD_DIM, lanes), f32))


def _split3(x):
    hi = x.astype(bf16)
    r1 = x - hi.astype(f32)
    mid = r1.astype(bf16)
    lo = (r1 - mid.astype(f32)).astype(bf16)
    return hi, mid, lo


POS_SPLIT = 128
K_AUG = 16


def _key_aug(pos_i):
    one = jnp.ones(pos_i.shape, f32)
    hi = (pos_i // POS_SPLIT).astype(f32)
    lo = (pos_i % POS_SPLIT).astype(f32)
    zero = jnp.zeros((pos_i.shape[0], K_AUG - 9), f32)
    return jnp.concatenate([one, one, one, hi, hi, hi, lo, lo, lo, zero], axis=1).astype(bf16)


def _query_aug(slope_row, t_row):
    rows = _split3(-slope_row * t_row) + _split3(slope_row * float(POS_SPLIT)) + _split3(slope_row)
    zero = jnp.zeros((K_AUG - 9, slope_row.shape[1]), bf16)
    return jnp.concatenate(list(rows) + [zero], axis=0)


def _attn_finish(carry):
    _, l, acc = carry
    return acc * (1.0 / jnp.maximum(l, 1e-30))


def _compressed_multi(kc, vc_t, q_t, t_row, slope_rows, n_ch):
    n = range(len(kc))
    blk_end = (lax.broadcasted_iota(jnp.int32, (n_ch, 1), 0) * CMP_STRIDE + (CMP_BLOCK - 1)).astype(f32)
    dist = t_row - blk_end
    mask = dist >= 0.0
    s = [jnp.dot(kc[i], q_t[i], preferred_element_type=f32) for i in n]
    s = [jnp.where(mask, s[i] - slope_rows[i] * dist, NEG) for i in n]
    m = [jnp.max(s[i], axis=0, keepdims=True) for i in n]
    p = [jnp.where(mask, jnp.exp2(s[i] - m[i]), 0.0) for i in n]
    inv = [1.0 / jnp.maximum(jnp.sum(p[i], axis=0, keepdims=True), 1e-30) for i in n]
    p = [p[i] * inv[i] for i in n]
    o = [jnp.dot(vc_t[i], p[i].astype(bf16), preferred_element_type=f32) for i in n]
    return o, p


def _selection_scores(p, tq, tok_row_i, n_rows, n_sel, n_ch):
    psum = p[:, 0:tq]
    for r in range(1, B_GROUP):
        psum = psum + p[:, r * tq:(r + 1) * tq]
    imp = jnp.dot(_overlap_t(n_rows, n_ch), psum, preferred_element_type=f32, precision=HI)
    j = lax.broadcasted_iota(jnp.int32, (n_rows, 1), 0)
    back = tok_row_i // SEL_BLOCK - j
    forced = (j == 0) | ((back >= 0) & (back < N_LOCAL))
    score = jnp.where(forced, FORCED_SCORE, jnp.where(back >= 0, imp, -1.0))
    return jnp.where(j < n_sel, score, -2.0)


def _kv_pool_partials(x, pw0, pw1):
    rows = x.shape[0]
    n = rows // CMP_STRIDE
    pool = (lax.broadcasted_iota(jnp.int32, (n, rows), 1) // CMP_STRIDE
            == lax.broadcasted_iota(jnp.int32, (n, rows), 0)).astype(f32)
    a = jnp.dot(pool, x * pw0, preferred_element_type=f32, precision=HI)
    b = jnp.dot(pool, x * pw1, preferred_element_type=f32, precision=HI)
    return a, b


def _compress_kv(pooled, wck, wcv, kc_ref, vct_ref):
    for g in range(B_KV_HEADS):
        kc_ref[g] = _mm(pooled[:, g * 64:(g + 1) * 64], wck).astype(bf16)
        vc = _mm(pooled[:, B_KV_WIDTH + g * 64:B_KV_WIDTH + (g + 1) * 64], wcv)
        vct_ref[g] = vc.T.astype(bf16)


def _nsa_prep_kernel(q_ref, kvc_ref, kvs_ref, kvw_ref, pw0_ref, pw1_ref,
                     qt_ref, ks_ref, vst_ref, kw_ref, vwt_ref, a_ref, b_ref):
    qt_ref[...] = (q_ref[...] * (ATTN_SCALE * LOG2E)).T.astype(bf16)
    tt = q_ref.shape[0]
    aug = _key_aug(pl.program_id(1) * tt + lax.broadcasted_iota(jnp.int32, (tt, 1), 0))
    for src, k_ref, vt_ref in ((kvs_ref, ks_ref, vst_ref), (kvw_ref, kw_ref, vwt_ref)):
        x = src[...]
        for g in range(B_KV_HEADS):
            k_ref[g] = jnp.concatenate([x[:, g * 64:(g + 1) * 64].astype(bf16), aug], axis=1)
        vt_ref[...] = x[:, B_KV_WIDTH:].T.astype(bf16)
    a, b = _kv_pool_partials(kvc_ref[...], pw0_ref[...], pw1_ref[...])
    a_ref[...] = a
    b_ref[...] = b


def _nsa_prep(u3, pw0, pw1, tt):
    B, T, _ = u3.shape
    n_ch = tt // CMP_STRIDE
    ublk = lambda w, off: pl.BlockSpec((None, tt, w), lambda b, t: (b, t, off // w))
    const = pl.BlockSpec((tt, 2 * B_KV_WIDTH), lambda b, t: (0, 0))
    return pl.pallas_call(
        _nsa_prep_kernel,
        out_shape=(jax.ShapeDtypeStruct((B, B_WIDTH, T), bf16),
                   jax.ShapeDtypeStruct((B, B_KV_HEADS, T, B_HEAD_DIM + K_AUG), bf16),
                   jax.ShapeDtypeStruct((B, B_KV_WIDTH, T), bf16),
                   jax.ShapeDtypeStruct((B, B_KV_HEADS, T, B_HEAD_DIM + K_AUG), bf16),
                   jax.ShapeDtypeStruct((B, B_KV_WIDTH, T), bf16),
                   jax.ShapeDtypeStruct((B, T // CMP_STRIDE, 2 * B_KV_WIDTH), f32),
                   jax.ShapeDtypeStruct((B, T // CMP_STRIDE, 2 * B_KV_WIDTH), f32)),
        grid=(B, T // tt),
        in_specs=[ublk(B_WIDTH, C_Q), ublk(2 * B_KV_WIDTH, C_KVC), ublk(2 * B_KV_WIDTH, C_KVS),
                  ublk(2 * B_KV_WIDTH, C_KVW), const, const],
        out_specs=(pl.BlockSpec((None, B_WIDTH, tt), lambda b, t: (b, 0, t)),
                   pl.BlockSpec((None, B_KV_HEADS, tt, B_HEAD_DIM + K_AUG), lambda b, t: (b, 0, t, 0)),
                   pl.BlockSpec((None, B_KV_WIDTH, tt), lambda b, t: (b, 0, t)),
                   pl.BlockSpec((None, B_KV_HEADS, tt, B_HEAD_DIM + K_AUG), lambda b, t: (b, 0, t, 0)),
                   pl.BlockSpec((None, B_KV_WIDTH, tt), lambda b, t: (b, 0, t)),
                   pl.BlockSpec((None, n_ch, 2 * B_KV_WIDTH), lambda b, t: (b, t, 0)),
                   pl.BlockSpec((None, n_ch, 2 * B_KV_WIDTH), lambda b, t: (b, t, 0))),
        compiler_params=_cparams(("parallel", "parallel")),
        name="nsa_prep",
    )(u3, u3, u3, u3, pw0, pw1)


def _rank_select(score, n_sel):
    sub = 8
    assert n_sel % sub == 0
    tiles = [score[r * sub:(r + 1) * sub] for r in range(n_sel // sub)]
    ranks = [jnp.zeros(t.shape, f32) for t in tiles]
    j_in_tile = lax.broadcasted_iota(jnp.int32, (sub, 1), 0)
    for i in range(n_sel):
        row = tiles[i // sub][i % sub:i % sub + 1]
        for r, t in enumerate(tiles):
            if r * sub > i:
                beats = jnp.where(row >= t, 1.0, 0.0)
            elif r * sub + sub - 1 <= i:
                beats = jnp.where(row > t, 1.0, 0.0)
            else:
                beats = jnp.where(row > t, 1.0, jnp.where(row == t, jnp.where(j_in_tile + r * sub > i, 1.0, 0.0), 0.0))
            ranks[r] = ranks[r] + beats
    rank = jnp.concatenate(ranks, axis=0)
    return jnp.where(rank < float(min(N_SELECT, n_sel)), jnp.where(score >= 0.0, 1.0, 0.0), 0.0)


def _nsa_prompt_kernel(qt_ref, ks_ref, vst_ref, kw_ref, vwt_ref, a_ref, b_ref, wck_ref, wcv_ref, gates_ref, gb_ref,
                       o_ref, kc_ref, vct_ref, keep_ref, cmp_ref, qaug_ref, m_ref, l_ref, acc_ref, ot_ref,
                       *, seq, tq):
    i = pl.program_id(1)
    n_ch = seq // CMP_STRIDE
    n_sel = seq // SEL_BLOCK
    lanes = B_GROUP * tq
    tk = 2 * SEL_BLOCK

    @pl.when(i == 0)
    def _():
        pooled = a_ref[...] + pltpu.roll(b_ref[...], n_ch - 1, 0)
        _compress_kv(pooled, wck_ref[...], wcv_ref[...], kc_ref, vct_ref)

    G = B_KV_HEADS
    gs = range(G)
    tok_i = i * tq + lax.broadcasted_iota(jnp.int32, (1, tq), 1)
    t_row = _tile_heads(tok_i.astype(f32))
    key_off = lax.broadcasted_iota(jnp.int32, (tk, 1), 0).astype(f32)
    slopes = [_slope_row(g, tq) * LOG2E for g in gs]

    def q_t(g):
        return jnp.concatenate([qt_ref[(g * B_GROUP + r) * 64:(g * B_GROUP + r + 1) * 64, :] for r in range(B_GROUP)],
                               axis=1)

    o_cmp, prob = _compressed_multi([kc_ref[g] for g in gs], [vct_ref[g] for g in gs], [q_t(g) for g in gs],
                                    t_row, slopes, n_ch)
    for g in gs:
        cmp_ref[g] = o_cmp[g]
    score = jnp.concatenate([_selection_scores(prob[g], tq, tok_i, n_sel, n_sel, n_ch) for g in gs], axis=1)
    keep = _rank_select(score, n_sel)
    for j in range(n_sel):
        keep_ref[j] = (keep[j:j + 1, :] - 1.0) * (-NEG)

    for g in gs:
        qaug_ref[g] = jnp.concatenate([q_t(g), _query_aug(slopes[g], t_row)], axis=0)
    for c in range(2 * G):
        m, l, acc = _attn_init(lanes)
        m_ref[c] = m
        l_ref[c] = l
        acc_ref[c] = acc

    def step(kt, with_window, masked):
        off = pl.multiple_of(kt * tk, tk)
        row0 = keep_ref[2 * kt]
        row1 = keep_ref[2 * kt + 1]
        if masked:
            dist = t_row - (key_off + (kt * tk).astype(f32))
            causal = jnp.where(dist >= 0.0, 0.0, NEG)
            band = jnp.where(dist <= float(WINDOW), causal, NEG)
        s, v, chains = [], [], []
        for g in gs:
            sg = jnp.dot(ks_ref[g, pl.ds(off, tk), :], qaug_ref[g], preferred_element_type=f32)
            if masked:
                sg = sg + causal
            s.append(jnp.concatenate([sg[:SEL_BLOCK] + _tile_heads(row0[:, g * tq:(g + 1) * tq]),
                                      sg[SEL_BLOCK:] + _tile_heads(row1[:, g * tq:(g + 1) * tq])], axis=0))
            v.append(vst_ref[g * 64:(g + 1) * 64, pl.ds(off, tk)])
            chains.append(g)
        if with_window:
            for g in gs:
                sg = jnp.dot(kw_ref[g, pl.ds(off, tk), :], qaug_ref[g], preferred_element_type=f32)
                s.append(sg + band if masked else sg)
                v.append(vwt_ref[g * 64:(g + 1) * 64, pl.ds(off, tk)])
                chains.append(G + g)
        out = _attn_step_multi(s, [(m_ref[c], l_ref[c], acc_ref[c]) for c in chains], v)
        for c, (m, l, acc) in zip(chains, out):
            m_ref[c] = m
            l_ref[c] = l
            acc_ref[c] = acc

    assert tq == tk and WINDOW % tk == 0
    last = i
    lo = jnp.maximum(i - WINDOW // tk, 0)

    def plain_loop(with_window):
        def body(kt, carry):
            step(kt, with_window, False)
            return carry
        return body

    lax.fori_loop(0, lo, plain_loop(False), 0)
    step(lo, True, True)
    lax.fori_loop(lo + 1, last, plain_loop(True), 0)

    @pl.when(last > lo)
    def _():
        step(last, True, True)

    gates_t = jax.nn.sigmoid(gates_ref[...]).T

    def gate_row(branch, g):
        base = branch * B_HEADS + g * B_GROUP
        return jnp.concatenate([gates_t[base + r:base + r + 1, :] for r in range(B_GROUP)], axis=1)

    o_sel = [_attn_finish((m_ref[g], l_ref[g], acc_ref[g])) for g in gs]
    o_win = [_attn_finish((m_ref[G + g], l_ref[G + g], acc_ref[G + g])) for g in gs]
    o_t = [gate_row(0, g) * cmp_ref[g] + gate_row(1, g) * o_sel[g] + gate_row(2, g) * o_win[g] for g in gs]
    for g in gs:
        for r in range(B_GROUP):
            ot_ref[(g * B_GROUP + r) * 64:(g * B_GROUP + r + 1) * 64, :] = o_t[g][:, r * tq:(r + 1) * tq]

    gb = gb_ref[...]
    o_ref[...] = ot_ref[...].T * (gb * jax.nn.sigmoid(gb))


def _nsa_prompt(u3, qt, ks, vst, kw, vwt, a, b, wck, wcv, tq):
    B, T, _ = u3.shape
    n_ch = T // CMP_STRIDE
    n_sel = T // SEL_BLOCK
    lanes = B_GROUP * tq
    per_b3 = lambda s1, s2: pl.BlockSpec((None, s1, s2), lambda b, i: (b, 0, 0))
    per_b4 = pl.BlockSpec((None, B_KV_HEADS, T, B_HEAD_DIM + K_AUG), lambda b, i: (b, 0, 0, 0))
    w_spec = pl.BlockSpec((B_HEAD_DIM, B_HEAD_DIM), lambda b, i: (0, 0))
    return pl.pallas_call(
        functools.partial(_nsa_prompt_kernel, seq=T, tq=tq),
        out_shape=jax.ShapeDtypeStruct((B, T, B_WIDTH), f32),
        grid=(B, T // tq),
        in_specs=[pl.BlockSpec((None, B_WIDTH, tq), lambda b, i: (b, 0, i)),
                  per_b4, per_b3(B_KV_WIDTH, T), per_b4, per_b3(B_KV_WIDTH, T),
                  per_b3(n_ch, 2 * B_KV_WIDTH), per_b3(n_ch, 2 * B_KV_WIDTH), w_spec, w_spec,
                  pl.BlockSpec((None, tq, 128), lambda b, i: (b, i, C_NG // 128)),
                  pl.BlockSpec((None, tq, B_WIDTH), lambda b, i: (b, i, C_GB // B_WIDTH))],
        out_specs=pl.BlockSpec((None, tq, B_WIDTH), lambda b, i: (b, i, 0)),
        scratch_shapes=[pltpu.VMEM((B_KV_HEADS, n_ch, B_HEAD_DIM), bf16),
                        pltpu.VMEM((B_KV_HEADS, B_HEAD_DIM, n_ch), bf16),
                        pltpu.VMEM((n_sel, 1, B_KV_HEADS * tq), f32),
                        pltpu.VMEM((B_KV_HEADS, B_HEAD_DIM, lanes), f32),
                        pltpu.VMEM((B_KV_HEADS, B_HEAD_DIM + K_AUG, lanes), bf16),
                        pltpu.VMEM((2 * B_KV_HEADS, 1, lanes), f32),
                        pltpu.VMEM((2 * B_KV_HEADS, 1, lanes), f32),
                        pltpu.VMEM((2 * B_KV_HEADS, B_HEAD_DIM, lanes), f32),
                        pltpu.VMEM((B_WIDTH, tq), f32)],
        compiler_params=_cparams(("parallel", "arbitrary")),
        name="nsa_prompt",
    )(qt, ks, vst, kw, vwt, a, b, wck, wcv, u3, u3)


def _softmax_rows_update(sc, m, l, acc, pv_fn):
    m_new = jnp.maximum(m, jnp.max(sc, axis=-1, keepdims=True))
    alpha = jnp.exp(m - m_new)
    p = jnp.where(sc > 0.5 * NEG, jnp.exp(sc - m_new), 0.0)
    return m_new, l * alpha + jnp.sum(p, axis=-1, keepdims=True), acc * alpha + pv_fn(p)


def _nsa_sample_kernel(pt_ref, *refs, n_pages, pages_per_step, t_new, t_pad):
    P = pages_per_step
    cmp_pages, sel_pages = refs[:P], refs[P:2 * P]
    (win_ref, q_ref, news_ref, neww_ref, pw0_ref, pw1_ref, pool_ref, wck_ref, wcv_ref, gates_ref, gb_ref,
     o_ref, at_ref, bt_ref, qg_ref, ocmp_ref, keep_ref, m_ref, l_ref, acc_ref) = refs[2 * P:]
    s = pl.program_id(1)
    S = n_pages // P
    G = B_KV_HEADS
    gs = range(G)
    past = n_pages * PAGE_SIZE
    n_ch = past // CMP_STRIDE
    n_sel = past // SEL_BLOCK + 1
    n_j = keep_ref.shape[1]
    R = B_GROUP * t_pad
    row = lax.broadcasted_iota(jnp.int32, (R, 1), 0)
    t_col = (past + row % t_pad).astype(f32)
    slope_col = []
    for g in gs:
        sc_ = jnp.zeros((R, 1), f32)
        for r in range(B_GROUP):
            sc_ = jnp.where(row // t_pad == r, _alibi_slope(g * B_GROUP + r), sc_)
        slope_col.append(sc_)
    ksl = lambda g: slice(g * 64, (g + 1) * 64)
    vsl = lambda g: slice(B_KV_WIDTH + g * 64, B_KV_WIDTH + (g + 1) * 64)

    @pl.when(s < S)
    def _():
        pw0, pw1 = pw0_ref[...], pw1_ref[...]
        xa = jnp.concatenate([(cmp_pages[k][...] * pw0).astype(bf16) for k in range(P)], axis=1)
        xb = jnp.concatenate([(cmp_pages[k][...] * pw1).astype(bf16) for k in range(P)], axis=1)
        n_step = P * PAGE_SIZE // CMP_STRIDE
        off = pl.multiple_of(s * n_step, n_step)
        at_ref[:, pl.ds(off, n_step)] = jnp.dot(xa, pool_ref[...], preferred_element_type=f32)
        bt_ref[:, pl.ds(off, n_step)] = jnp.dot(xb, pool_ref[...], preferred_element_type=f32)

    @pl.when(s == S - 1)
    def _():
        pooled = at_ref[...] + pltpu.roll(bt_ref[...], n_ch - 1, 1)
        q = q_ref[...] * ATTN_SCALE
        qg = [jnp.concatenate([q[:, (g * B_GROUP + r) * 64:(g * B_GROUP + r + 1) * 64] for r in range(B_GROUP)],
                              axis=0).astype(bf16) for g in gs]
        kct = [_mm_tn(wck_ref[...], pooled[ksl(g), :]).astype(bf16) for g in gs]
        vct = [_mm_tn(wcv_ref[...], pooled[vsl(g), :]).astype(bf16) for g in gs]
        blk_end = (lax.broadcasted_iota(jnp.int32, (1, n_ch), 1) * CMP_STRIDE + (CMP_BLOCK - 1)).astype(f32)
        dist = t_col - blk_end
        mask = dist >= 0.0
        sc = [jnp.dot(qg[g], kct[g], preferred_element_type=f32) for g in gs]
        sc = [jnp.where(mask, sc[g] - slope_col[g] * dist, NEG) for g in gs]
        mx = [jnp.max(sc[g], axis=-1, keepdims=True) for g in gs]
        pr = [jnp.where(mask, jnp.exp(sc[g] - mx[g]), 0.0) for g in gs]
        inv = [1.0 / jnp.maximum(jnp.sum(pr[g], axis=-1, keepdims=True), 1e-30) for g in gs]
        pr = [pr[g] * inv[g] for g in gs]
        for g in gs:
            qg_ref[g] = qg[g]
            ocmp_ref[g] = _mm_nt(pr[g], vct[g])
            m_ref[g] = jnp.full((R, 1), NEG, f32)
            l_ref[g] = jnp.zeros((R, 1), f32)
            acc_ref[g] = jnp.zeros((R, B_HEAD_DIM), f32)
        psum = [pr[g][0:t_pad] for g in gs]
        for r in range(1, B_GROUP):
            psum = [psum[g] + pr[g][r * t_pad:(r + 1) * t_pad] for g in gs]
        psum = jnp.concatenate(psum, axis=0)
        n_i = lax.broadcasted_iota(jnp.int32, (n_ch, n_j), 0) * CMP_STRIDE
        j_i = lax.broadcasted_iota(jnp.int32, (n_ch, n_j), 1) * SEL_BLOCK
        overlap = jnp.maximum(jnp.minimum(n_i + CMP_BLOCK, j_i + SEL_BLOCK) - jnp.maximum(n_i, j_i), 0).astype(f32) \
            * (1.0 / CMP_BLOCK)
        imp = jnp.dot(psum, overlap, preferred_element_type=f32, precision=HI)
        j = lax.broadcasted_iota(jnp.int32, (1, n_j), 1)
        tok = past + lax.broadcasted_iota(jnp.int32, (G * t_pad, 1), 0) % t_pad
        back = tok // SEL_BLOCK - j
        forced = (j == 0) | ((back >= 0) & (back < N_LOCAL))
        score = jnp.where(forced, FORCED_SCORE, jnp.where(back >= 0, imp, -1.0))
        score = jnp.where(j < n_sel, score, -2.0)
        rank = jnp.zeros(score.shape, f32)
        for i in range(n_sel):
            col = score[:, i:i + 1]
            rank = rank + jnp.where(col > score, 1.0, jnp.where(col == score, jnp.where(j > i, 1.0, 0.0), 0.0))
        keep_ref[...] = jnp.where(rank < float(min(N_SELECT, n_sel)), jnp.where(score >= 0.0, 1.0, 0.0), 0.0)

    @pl.when(s >= S)
    def _():
        ss = s - S
        n_blk = P * PAGE_SIZE // SEL_BLOCK
        n_key = P * PAGE_SIZE
        pick = (lax.broadcasted_iota(jnp.int32, (n_j, n_blk), 0)
                == lax.broadcasted_iota(jnp.int32, (n_j, n_blk), 1) + ss * n_blk).astype(bf16)
        keep_blk = jnp.dot(keep_ref[...].astype(bf16), pick, preferred_element_type=f32)
        expand = (lax.broadcasted_iota(jnp.int32, (n_blk, n_key), 0)
                  == lax.broadcasted_iota(jnp.int32, (n_blk, n_key), 1) // SEL_BLOCK).astype(bf16)
        bias = (jnp.dot(keep_blk.astype(bf16), expand, preferred_element_type=f32) - 1.0) * (-NEG)
        kpos = (ss * n_key + lax.broadcasted_iota(jnp.int32, (1, n_key), 1)).astype(f32)
        dist = t_col - kpos
        kt = [jnp.concatenate([sel_pages[k][ksl(g), :].astype(bf16) for k in range(P)], axis=1) for g in gs]
        vt = [jnp.concatenate([sel_pages[k][vsl(g), :].astype(bf16) for k in range(P)], axis=1) for g in gs]
        sc = [jnp.dot(qg_ref[g], kt[g], preferred_element_type=f32) for g in gs]
        sc = [sc[g] - slope_col[g] * dist + jnp.concatenate([bias[g * t_pad:(g + 1) * t_pad]] * B_GROUP, axis=0)
              for g in gs]
        m_new = [jnp.maximum(m_ref[g], jnp.max(sc[g], axis=-1, keepdims=True)) for g in gs]
        alpha = [jnp.exp(m_ref[g] - m_new[g]) for g in gs]
        pr = [jnp.where(sc[g] > 0.5 * NEG, jnp.exp(sc[g] - m_new[g]), 0.0) for g in gs]
        l_new = [l_ref[g] * alpha[g] + jnp.sum(pr[g], axis=-1, keepdims=True) for g in gs]
        pv = [_mm_nt(pr[g], vt[g]) for g in gs]
        for g in gs:
            acc_ref[g] = acc_ref[g] * alpha[g] + pv[g]
            m_ref[g] = m_new[g]
            l_ref[g] = l_new[g]

    @pl.when(s == 2 * S - 1)
    def _():
        xs = news_ref[...]
        xw = neww_ref[...]
        win = win_ref[...]
        n_win = win.shape[1]
        new_dist = t_col - (past + lax.broadcasted_iota(jnp.int32, (1, t_pad), 1)).astype(f32)
        win_dist = t_col - (past - n_win + lax.broadcasted_iota(jnp.int32, (1, n_win), 1)).astype(f32)
        keep_new = keep_ref[:, n_sel - 1:n_sel]
        gates = jax.nn.sigmoid(gates_ref[...])
        gb = gb_ref[...]
        silu_gb = gb * jax.nn.sigmoid(gb)
        qg = [qg_ref[g] for g in gs]
        sc = [_mm_nt(qg[g], xs[:, ksl(g)]) - slope_col[g] * new_dist for g in gs]
        sc = [jnp.where(new_dist >= 0.0, sc[g], NEG)
              + (jnp.concatenate([keep_new[g * t_pad:(g + 1) * t_pad]] * B_GROUP, axis=0) - 1.0) * (-NEG) for g in gs]
        sel = [_softmax_rows_update(sc[g], m_ref[g], l_ref[g], acc_ref[g], lambda p, g=g: _mm(p, xs[:, vsl(g)]))
               for g in gs]
        o_sel = [sel[g][2] * (1.0 / jnp.maximum(sel[g][1], 1e-30)) for g in gs]
        sw = [jnp.dot(qg[g], win[ksl(g), :].astype(bf16), preferred_element_type=f32) - slope_col[g] * win_dist
              for g in gs]
        sw = [jnp.where(win_dist <= float(WINDOW), sw[g], NEG) for g in gs]
        sn = [_mm_nt(qg[g], xw[:, ksl(g)]) - slope_col[g] * new_dist for g in gs]
        sn = [jnp.where(new_dist >= 0.0, jnp.where(new_dist <= float(WINDOW), sn[g], NEG), NEG) for g in gs]
        mw = [jnp.maximum(jnp.max(sw[g], axis=-1, keepdims=True), jnp.max(sn[g], axis=-1, keepdims=True)) for g in gs]
        pw = [jnp.where(sw[g] > 0.5 * NEG, jnp.exp(sw[g] - mw[g]), 0.0) for g in gs]
        pn = [jnp.where(sn[g] > 0.5 * NEG, jnp.exp(sn[g] - mw[g]), 0.0) for g in gs]
        lw = [jnp.sum(pw[g], axis=-1, keepdims=True) + jnp.sum(pn[g], axis=-1, keepdims=True) for g in gs]
        o_win = [(_mm_nt(pw[g], win[vsl(g), :]) + _mm(pn[g], xw[:, vsl(g)])) * (1.0 / jnp.maximum(lw[g], 1e-30))
                 for g in gs]
        for g in gs:
            def gate_col(branch):
                base = branch * B_HEADS + g * B_GROUP
                return jnp.concatenate([gates[:, base + r:base + r + 1] for r in range(B_GROUP)], axis=0)

            o = gate_col(0) * ocmp_ref[g] + gate_col(1) * o_sel[g] + gate_col(2) * o_win[g]
            for r in range(B_GROUP):
                cols = slice((g * B_GROUP + r) * 64, (g * B_GROUP + r + 1) * 64)
                o_ref[:, cols] = o[r * t_pad:(r + 1) * t_pad, :] * silu_gb[:, cols]


def _nsa_sample(u3, cache_cmp_t, cache_sel_t, cache_win_t, page_table, pw0_t, pw1_t, wck, wcv, *, t_new, pages_per_step):
    DB, t_pad, _ = u3.shape
    n_pages = page_table.shape[1]
    P = pages_per_step
    assert t_new < CMP_STRIDE and t_new <= t_pad and n_pages % P == 0 and (P * PAGE_SIZE // CMP_STRIDE) % 128 == 0
    S = n_pages // P
    past = n_pages * PAGE_SIZE
    n_ch = past // CMP_STRIDE
    n_sel = past // SEL_BLOCK + 1
    n_j = -(-n_sel // 128) * 128
    R = B_GROUP * t_pad
    kvw = 2 * B_KV_WIDTH
    pool = (lax.broadcasted_iota(jnp.int32, (P * PAGE_SIZE, P * PAGE_SIZE // CMP_STRIDE), 0) // CMP_STRIDE
            == lax.broadcasted_iota(jnp.int32, (P * PAGE_SIZE, P * PAGE_SIZE // CMP_STRIDE), 1)).astype(bf16)
    ublk = lambda w, off: pl.BlockSpec((None, t_pad, w), lambda b, s, pt: (b, 0, off // w))
    const = lambda s1, s2: pl.BlockSpec((s1, s2), lambda b, s, pt: (0, 0))
    page = lambda fn: pl.BlockSpec((None, kvw, PAGE_SIZE), fn)
    cmp_specs = [page(lambda b, s, pt, k=k: (pt[b, jnp.minimum(s, S - 1) * P + k], 0, 0)) for k in range(P)]
    sel_specs = [page(lambda b, s, pt, k=k: (pt[b, jnp.maximum(s - S, 0) * P + k], 0, 0)) for k in range(P)]
    grid_spec = pltpu.PrefetchScalarGridSpec(
        num_scalar_prefetch=1,
        grid=(DB, 2 * S),
        in_specs=cmp_specs + sel_specs + [
            pl.BlockSpec((None, kvw, cache_win_t.shape[2]), lambda b, s, pt: (b, 0, 0)),
            ublk(B_WIDTH, C_Q), ublk(kvw, C_KVS), ublk(kvw, C_KVW),
            const(kvw, PAGE_SIZE), const(kvw, PAGE_SIZE), const(*pool.shape),
            const(B_HEAD_DIM, B_HEAD_DIM), const(B_HEAD_DIM, B_HEAD_DIM),
            ublk(128, C_NG), ublk(B_WIDTH, C_GB)],
        out_specs=pl.BlockSpec((None, t_pad, B_WIDTH), lambda b, s, pt: (b, 0, 0)),
        scratch_shapes=[pltpu.VMEM((kvw, n_ch), f32), pltpu.VMEM((kvw, n_ch), f32),
                        pltpu.VMEM((B_KV_HEADS, R, B_HEAD_DIM), bf16),
                        pltpu.VMEM((B_KV_HEADS, R, B_HEAD_DIM), f32),
                        pltpu.VMEM((B_KV_HEADS * t_pad, n_j), f32),
                        pltpu.VMEM((B_KV_HEADS, R, 1), f32), pltpu.VMEM((B_KV_HEADS, R, 1), f32),
                        pltpu.VMEM((B_KV_HEADS, R, B_HEAD_DIM), f32)])
    return pl.pallas_call(
        functools.partial(_nsa_sample_kernel, n_pages=n_pages, pages_per_step=P, t_new=t_new, t_pad=t_pad),
        out_shape=jax.ShapeDtypeStruct((DB, t_pad, B_WIDTH), f32),
        grid_spec=grid_spec,
        compiler_params=_cparams(("parallel", "arbitrary")),
        name="nsa_sample",
    )(page_table, *([cache_cmp_t] * P), *([cache_sel_t] * P), cache_win_t, u3, u3, u3, pw0_t, pw1_t, pool, wck, wcv,
      u3, u3)


def _pos_weight_tiles(pos_k, pos_v, rows):
    def half(lo):
        t = jnp.concatenate([jnp.tile(pos_k[lo:lo + CMP_STRIDE], (1, B_KV_HEADS)),
                             jnp.tile(pos_v[lo:lo + CMP_STRIDE], (1, B_KV_HEADS))], axis=1)
        return jnp.tile(t, (rows // CMP_STRIDE, 1))
    return half(0), half(CMP_STRIDE)


def _a_cols(u_rows):
    return jnp.concatenate([u_rows[..., :4 * A_WIDTH], u_rows[..., C_LR:C_LR + DECAY_RANK + ICLR_RANK]], axis=-1)


def kernel(x_prompt, x_sample, cache_cmp_kv, cache_sel_kv, cache_win_kv, state_rwkv, state_shift, page_table, norm_in,
           w_in, mu_shift, w0, w_up, a0, a_up, k_k, k_a, r_k, gn_w, gn_b, cmp_pos_k, cmp_pos_v, w_cmp_k, w_cmp_v,
           w_pa, w_pb, w_o, norm_out):
    assert w_in.shape[0] == 1, "one layer"
    B, T, _ = x_prompt.shape
    DB, TS, _ = x_sample.shape
    H, N = A_HEADS, A_HEAD_DIM
    kvs = (2, B_KV_HEADS, B_HEAD_DIM)
    p = dict(mu_shift=mu_shift[0], w0=w0[0], w_up=w_up[0], a0=a0[0], a_up=a_up[0], k_k=k_k[0], k_a=k_a[0],
             r_k=r_k[0].reshape(-1), gn_w=gn_w[0], gn_b=gn_b[0])
    w_packed = _pack_w_in(w_in[0])
    wpa, wpb, wo = w_pa[0].astype(bf16), w_pb[0].astype(bf16), w_o[0].astype(bf16)

    xp = x_prompt.reshape(B * T, D_MODEL)
    up = _proj_in(xp, norm_in[0], w_packed)
    up3 = up.reshape(B, T, U_COLS)
    oa_p, s_p = _rwkv_wide(up3, jnp.zeros((B, 1, 4 * A_WIDTH), f32), jnp.zeros((B, 1, 128), f32),
                      jnp.zeros((B, H, N, N), f32), p, chunk=64, t_valid=64, rows=math.gcd(B, 4))
    tt = 512
    pw0, pw1 = _pos_weight_tiles(cmp_pos_k[0], cmp_pos_v[0], tt)
    qt, ks, vst, kw, vwt, pa, pb = _nsa_prep(up3, pw0, pw1, tt)
    ob_p = _nsa_prompt(up3, qt, ks, vst, kw, vwt, pa, pb, w_cmp_k[0], w_cmp_v[0], 128)
    y_p = _merge(xp, oa_p.reshape(B * T, A_WIDTH), ob_p.reshape(B * T, B_WIDTH), up, wpa, wpb, wo, norm_out)

    t_pad = 8
    xs = x_sample.reshape(DB * TS, D_MODEL)
    us = _proj_in(xs, norm_in[0], w_packed)
    us3 = us.reshape(DB, TS, U_COLS)
    us3p = jnp.pad(us3, ((0, 0), (0, t_pad - TS), (0, 0)))
    shift0 = state_shift[0]
    oa_s, s_s = _rwkv_wide(us3p, shift0[:, None, :4 * A_WIDTH], shift0[:, None, 4 * A_WIDTH:], state_rwkv[0], p,
                      chunk=t_pad, t_valid=TS, rows=2 if DB % 2 == 0 else 1)
    n_pool = cache_cmp_kv.shape[1]
    win = cache_win_kv[0]
    rows_last = lambda c, lead: jnp.transpose(c, (0, 2, 3, 4, 1)).reshape(lead, 2 * B_KV_WIDTH, c.shape[1])
    ob_s = _nsa_sample(us3p, rows_last(cache_cmp_kv[0], n_pool), rows_last(cache_sel_kv[0], n_pool),
                       rows_last(win, DB), page_table, pw0[:PAGE_SIZE].T, pw1[:PAGE_SIZE].T, w_cmp_k[0], w_cmp_v[0],
                       t_new=TS, pages_per_step=min(16, page_table.shape[1]))
    y_s = _merge(xs, oa_s[:, :TS].reshape(DB * TS, A_WIDTH), ob_s[:, :TS].reshape(DB * TS, B_WIDTH), us, wpa, wpb, wo,
                 norm_out)

    def kv_out(u3_, col, lead, t):
        return u3_[..., col:col + 2 * B_KV_WIDTH].reshape((1, lead, t) + kvs)

    wk = min(WINDOW, T)
    new_w_s = kv_out(us3, C_KVW, DB, TS)[0]
    s_win = jnp.concatenate([win, new_w_s], axis=1)[:, TS:][None]
    return (y_p.reshape(B, T, D_MODEL), y_s.reshape(DB, TS, D_MODEL),
            kv_out(up3, C_KVC, B, T), kv_out(up3, C_KVS, B, T), kv_out(up3[:, T - wk:], C_KVW, B, wk),
            s_p[None], _a_cols(up3[:, T - 1])[None],
            kv_out(us3, C_KVC, DB, TS), kv_out(us3, C_KVS, DB, TS), s_win,
            s_s[None], _a_cols(us3[:, TS - 1])[None])
```

```python
import functools
import math

import jax
import jax.numpy as jnp
from jax import lax
from jax.experimental import pallas as pl
from jax.experimental.pallas import tpu as pltpu

f32 = jnp.float32
bf16 = jnp.bfloat16

D_MODEL = 2048
PAGE_SIZE = 128
A_HEADS = 16
A_HEAD_DIM = 64
A_WIDTH = A_HEADS * A_HEAD_DIM
DECAY_RANK = 64
ICLR_RANK = 64
A_COLS = 4 * A_WIDTH + DECAY_RANK + ICLR_RANK
GN_EPS = 64e-5
B_HEADS = 16
B_KV_HEADS = 4
B_GROUP = B_HEADS // B_KV_HEADS
B_HEAD_DIM = 64
B_WIDTH = B_HEADS * B_HEAD_DIM
B_KV_WIDTH = B_KV_HEADS * B_HEAD_DIM
B_COLS = 2 * B_WIDTH + 6 * B_KV_WIDTH + 3 * B_HEADS
CMP_BLOCK = 32
CMP_STRIDE = 16
SEL_BLOCK = 64
N_SELECT = 16
N_LOCAL = 2
WINDOW = 512
FORCED_SCORE = 1e4
ATTN_SCALE = B_HEAD_DIM ** -0.5
RMS_EPS = 1e-6
NEG = -1e30
LOG2E = math.log2(math.e)

C_RKVG = 0
C_Q = 4096
C_GB = 5120
C_GA_M = 6144
C_GB_M = 8192
C_KVC = 10240
C_KVS = 10752
C_KVW = 11264
C_LR = 11776
C_NG = 11904
U_COLS = 12288

VMEM_LIMIT = 56 * 1024 * 1024
HI = lax.Precision.HIGHEST


def _cparams(sem):
    return pltpu.CompilerParams(dimension_semantics=sem, vmem_limit_bytes=VMEM_LIMIT)


def _pack_w_in(w):
    wt = w.T
    a, b, m = wt[:A_COLS], wt[A_COLS:A_COLS + B_COLS], wt[A_COLS + B_COLS:]
    z = jnp.zeros((U_COLS - C_NG - 3 * B_HEADS, w.shape[0]), w.dtype)
    return jnp.concatenate(
        [a[:4 * A_WIDTH], b[:2 * B_WIDTH], m, b[2 * B_WIDTH:2 * B_WIDTH + 6 * B_KV_WIDTH],
         a[4 * A_WIDTH:], b[2 * B_WIDTH + 6 * B_KV_WIDTH:], z], axis=0).astype(bf16)


def _proj_in_kernel(x_ref, g_ref, w_ref, o_ref, xn_ref):
    @pl.when(pl.program_id(1) == 0)
    def _():
        x = x_ref[...]
        ms = jnp.mean(x * x, axis=-1, keepdims=True)
        xn_ref[...] = (x * lax.rsqrt(ms + RMS_EPS) * g_ref[...]).astype(bf16)

    o_ref[...] = lax.dot_general(xn_ref[...], w_ref[...], (((1,), (1,)), ((), ())), preferred_element_type=f32)


def _proj_in(x2d, norm_g, w_packed_t):
    m = x2d.shape[0]
    tm = min(1024, m)
    tn = 1024
    return pl.pallas_call(
        _proj_in_kernel,
        out_shape=jax.ShapeDtypeStruct((m, U_COLS), f32),
        grid=(m // tm, U_COLS // tn),
        in_specs=[pl.BlockSpec((tm, D_MODEL), lambda i, j: (i, 0)),
                  pl.BlockSpec((1, D_MODEL), lambda i, j: (0, 0)),
                  pl.BlockSpec((tn, D_MODEL), lambda i, j: (j, 0))],
        out_specs=pl.BlockSpec((tm, tn), lambda i, j: (i, j)),
        scratch_shapes=[pltpu.VMEM((tm, D_MODEL), bf16)],
        compiler_params=_cparams(("parallel", "arbitrary")),
        name="proj_in",
    )(x2d, norm_g.reshape(1, D_MODEL), w_packed_t)


def _mm(a, b):
    return jnp.dot(a.astype(bf16), b.astype(bf16), preferred_element_type=f32)


def _mm_nt(a, b):
    return lax.dot_general(a.astype(bf16), b.astype(bf16), (((1,), (1,)), ((), ())), preferred_element_type=f32)


def _mm_tn(a, b):
    return lax.dot_general(a.astype(bf16), b.astype(bf16), (((0,), (0,)), ((), ())), preferred_element_type=f32)


HEADS_PER_GROUP = 4
N_HEAD_GROUPS = A_HEADS // HEADS_PER_GROUP
GROUP_WIDTH = HEADS_PER_GROUP * A_HEAD_DIM


def _rwkv_wide_kernel(um_ref, ulr_ref, pm_ref, plr_ref, s0_ref, mum_ref, mulr_ref, w0_ref, wup_ref, a0_ref,
                      aup_ref, kk_ref, ka_ref, rk_ref, gnw_ref, gnb_ref, o_ref, sout_ref,
                      s_ref, prevm_ref, prevlr_ref, *, chunk, t_valid):
    C = chunk
    N, HG, NG, GW = A_HEAD_DIM, HEADS_PER_GROUP, N_HEAD_GROUPS, GROUP_WIDTH
    RB = um_ref.shape[0]
    c = pl.program_id(1)

    @pl.when(c == 0)
    def _():
        for bi in range(RB):
            for q in range(NG):
                s_ref[bi, q] = jnp.concatenate([s0_ref[bi, q * HG + h] for h in range(HG)], axis=1)
        prevm_ref[...] = pm_ref[...]
        prevlr_ref[...] = plr_ref[...]

    row = lax.broadcasted_iota(jnp.int32, (C, 1), 0)
    tril_b = (lax.broadcasted_iota(jnp.int32, (C, C), 0) >= lax.broadcasted_iota(jnp.int32, (C, C), 1)).astype(bf16)
    t_i = lax.broadcasted_iota(jnp.int32, (C, HG * C), 0)
    s_i = lax.broadcasted_iota(jnp.int32, (C, HG * C), 1) % C
    tril_incl = t_i >= s_i
    tril_strict = t_i > s_i
    eye = (t_i == s_i).astype(f32)
    head_of_dim = lax.broadcasted_iota(jnp.int32, (1, GW), 1) // N
    head_of_tok = lax.broadcasted_iota(jnp.int32, (1, HG * C), 1) // C
    ones_bd = (lax.broadcasted_iota(jnp.int32, (GW, GW), 0) // N
               == lax.broadcasted_iota(jnp.int32, (GW, GW), 1) // N).astype(bf16)
    n_sq = int(math.log2(C)) - 1
    valid = (row < t_valid).astype(f32) if t_valid < C else None
    gcols = [slice(q * GW, (q + 1) * GW) for q in range(NG)]
    qs = range(NG)

    def shifted(u, prev):
        return jnp.where(row == 0, prev, pltpu.roll(u, 1, 0))

    def head_sums(*xs):
        stacked = jnp.concatenate([x.astype(bf16)[:, gc] for x in xs for gc in gcols], axis=0)
        sums = jnp.dot(stacked, ones_bd, preferred_element_type=f32)
        return [jnp.concatenate([sums[(i * NG + q) * C:(i * NG + q + 1) * C] for q in qs], axis=1)
                for i in range(len(xs))]

    def block_rows(x, head_of_lane):
        return jnp.concatenate([jnp.where(head_of_lane == h, x, 0.0) for h in range(HG)], axis=0).astype(bf16)

    def mm(a, b):
        return jnp.dot(a.astype(bf16), b, preferred_element_type=f32)

    def mm_nt(a, b):
        return lax.dot_general(a.astype(bf16), b, (((1,), (1,)), ((), ())), preferred_element_type=f32)

    def prepare(bi, out):
        um = um_ref[bi]
        ulr = ulr_ref[bi]
        prev_m = prevm_ref[bi]
        prev_lr = prevlr_ref[bi]
        prevm_ref[bi] = um[C - 1:C, :]
        prevlr_ref[bi] = ulr[C - 1:C, :]
        uslr = ulr + mulr_ref[...] * (shifted(ulr, prev_lr) - ulr)
        zw = w0_ref[...] + _mm(jnp.tanh(uslr[:, :DECAY_RANK]), wup_ref[...])
        a = jax.nn.sigmoid(a0_ref[...] + _mm(uslr[:, DECAY_RANK:], aup_ref[...]))
        yield
        secs = []
        for i in range(4):
            cols = slice(i * A_WIDTH, (i + 1) * A_WIDTH)
            secs.append(um[:, cols] + mum_ref[:, cols] * (shifted(um[:, cols], prev_m[:, cols]) - um[:, cols]))
            yield
        r, k, v, g = secs
        logw = -math.exp(-0.5) * jax.nn.sigmoid(zw)
        if valid is not None:
            logw = logw * valid
        yield
        logw_hi = logw.astype(bf16)
        logw_lo = (logw - logw_hi.astype(f32)).astype(bf16)
        cum = (jnp.dot(tril_b, logw_hi, preferred_element_type=f32)
               + jnp.dot(tril_b, logw_lo, preferred_element_type=f32))
        yield
        e_pos = jnp.exp(cum)
        e_neg = jnp.exp(-cum)
        yield
        e_prev = jnp.exp(cum - logw)
        kk = k * kk_ref[...]
        yield
        k2 = k * (1.0 + (a - 1.0) * ka_ref[...])
        yield
        ssq, bonus = head_sums(kk * kk, r * k2 * rk_ref[...])
        yield
        kk = kk * lax.rsqrt(jnp.maximum(ssq, 1e-24))
        kn = k2 * e_neg
        bn = kk * a * e_neg
        if valid is not None:
            kn = kn * valid
            bn = bn * valid
        yield
        lhs = jnp.concatenate([kk * e_prev, r * e_pos], axis=0).astype(bf16)
        out.update(v=v, g=g, bonus=bonus, kn=kn, bn=bn, lhs=lhs, e_last=e_pos[C - 1:C, :])
        yield

    def solve(bi, pre, out):
        v = [pre["v"][:, gc] for gc in gcols]
        kn = [pre["kn"][:, gc] for gc in gcols]
        bn = [pre["bn"][:, gc] for gc in gcols]
        lhs = [pre["lhs"][:, gc] for gc in gcols]
        s0 = [s_ref[bi, q] for q in qs]
        qk_b = [mm_nt(lhs[q], block_rows(bn[q], head_of_dim)) for q in qs]
        yield
        qk_k = [mm_nt(lhs[q], block_rows(kn[q], head_of_dim)) for q in qs]
        yield
        w0s = [mm_nt(lhs[q], block_rows(s0[q], head_of_dim)) for q in qs]
        yield
        pw = [jnp.where(tril_strict, -qk_b[q][:C], 0.0) for q in qs]
        tinv = [eye + pw[q] for q in qs]
        bd = [block_rows(pw[q], head_of_tok) for q in qs]
        for _ in range(n_sq):
            pw = [mm(pw[q], bd[q]) for q in qs]
            yield
            bd = [block_rows(pw[q], head_of_tok) for q in qs]
            tinv = [tinv[q] + mm(tinv[q], bd[q]) for q in qs]
            yield
        bd_v = [block_rows(v[q], head_of_dim) for q in qs]
        rhs_u = [w0s[q][:C] + mm(jnp.where(tril_strict, qk_k[q][:C], 0.0), bd_v[q]) for q in qs]
        yield
        u = [mm(tinv[q], block_rows(rhs_u[q], head_of_dim)) for q in qs]
        yield
        p_cat = [jnp.concatenate([jnp.where(tril_incl, qk_k[q][C:], 0.0), jnp.where(tril_incl, -qk_b[q][C:], 0.0)],
                                 axis=1) for q in qs]
        out["o"] = [w0s[q][C:] + mm(p_cat[q], jnp.concatenate([bd_v[q], block_rows(u[q], head_of_dim)], axis=0))
                    for q in qs]
        yield
        full = [lax.dot_general(jnp.concatenate([v[q], -u[q]], axis=0).astype(bf16),
                                jnp.concatenate([kn[q], bn[q]], axis=0).astype(bf16),
                                (((0,), (0,)), ((), ())), preferred_element_type=f32) for q in qs]
        yield
        for q in qs:
            upd = jnp.where(head_of_dim == 0, full[q][0:N], 0.0)
            for h in range(1, HG):
                upd = upd + jnp.where(head_of_dim == h, full[q][h * N:(h + 1) * N], 0.0)
            s_ref[bi, q] = (s0[q] + upd) * pre["e_last"][:, gcols[q]]
        yield

    def finish(bi, pre, mid):
        o = jnp.concatenate(mid["o"], axis=1)
        mean = head_sums(o)[0] * (1.0 / N)
        yield
        d = o - mean
        var = head_sums(d * d)[0] * (1.0 / N)
        yield
        g = pre["g"]
        on = d * lax.rsqrt(var + GN_EPS) * gnw_ref[...] + gnb_ref[...] + pre["bonus"] * pre["v"]
        o_ref[bi] = on * (g * jax.nn.sigmoid(g))
        yield

    def interleave(gens):
        gens = list(gens)
        while gens:
            for gen in list(gens):
                if next(gen, "done") == "done":
                    gens.remove(gen)

    pre = [{} for _ in range(RB)]
    mid = [{} for _ in range(RB)]
    halves = [range(0, RB // 2), range(RB // 2, RB)] if RB > 1 else [range(RB)]
    prep = lambda rows_: [prepare(bi, pre[bi]) for bi in rows_]
    solv = lambda rows_: [solve(bi, pre[bi], mid[bi]) for bi in rows_]
    fin = lambda rows_: [finish(bi, pre[bi], mid[bi]) for bi in rows_]
    interleave(prep(halves[0]))
    for i, rows_ in enumerate(halves):
        interleave(solv(rows_) + (prep(halves[i + 1]) if i + 1 < len(halves) else [])
                   + (fin(halves[i - 1]) if i >= 1 else []))
    interleave(fin(halves[-1]))

    @pl.when(c == pl.num_programs(1) - 1)
    def _():
        for bi in range(RB):
            for q in range(NG):
                s_q = s_ref[bi, q]
                for h in range(HG):
                    sout_ref[bi, q * HG + h] = s_q[:, h * N:(h + 1) * N]


def _rwkv_wide(u3, prev_m, prev_lr, s0, p, *, chunk, t_valid, rows):
    B, T, _ = u3.shape
    C = chunk
    NG, N, GW = N_HEAD_GROUPS, A_HEAD_DIM, GROUP_WIDTH
    assert B % rows == 0 and T % C == 0
    row = lambda v: v.reshape(1, -1)
    vec = lambda n: pl.BlockSpec((1, n), lambda b, c: (0, 0))
    in_specs = [
        pl.BlockSpec((rows, C, 4 * A_WIDTH), lambda b, c: (b, c, 0)),
        pl.BlockSpec((rows, C, 128), lambda b, c: (b, c, C_LR // 128)),
        pl.BlockSpec((rows, 1, 4 * A_WIDTH), lambda b, c: (b, 0, 0)),
        pl.BlockSpec((rows, 1, 128), lambda b, c: (b, 0, 0)),
        pl.BlockSpec((rows, A_HEADS, N, N), lambda b, c: (b, 0, 0, 0)),
        vec(4 * A_WIDTH), vec(128), vec(A_WIDTH),
        pl.BlockSpec((DECAY_RANK, A_WIDTH), lambda b, c: (0, 0)),
        vec(A_WIDTH),
        pl.BlockSpec((ICLR_RANK, A_WIDTH), lambda b, c: (0, 0)),
        vec(A_WIDTH), vec(A_WIDTH), vec(A_WIDTH), vec(A_WIDTH), vec(A_WIDTH),
    ]
    o_a, s_out = pl.pallas_call(
        functools.partial(_rwkv_wide_kernel, chunk=C, t_valid=t_valid),
        out_shape=(jax.ShapeDtypeStruct((B, T, A_WIDTH), f32), jax.ShapeDtypeStruct((B, A_HEADS, N, N), f32)),
        grid=(B // rows, T // C),
        in_specs=in_specs,
        out_specs=(pl.BlockSpec((rows, C, A_WIDTH), lambda b, c: (b, c, 0)),
                   pl.BlockSpec((rows, A_HEADS, N, N), lambda b, c: (b, 0, 0, 0))),
        scratch_shapes=[pltpu.VMEM((rows, NG, N, GW), f32), pltpu.VMEM((rows, 1, 4 * A_WIDTH), f32),
                        pltpu.VMEM((rows, 1, 128), f32)],
        compiler_params=_cparams(("parallel", "arbitrary")),
        name="rwkv7",
    )(u3, u3, prev_m, prev_lr, s0, row(p["mu_shift"][:4 * A_WIDTH]), row(p["mu_shift"][4 * A_WIDTH:]),
      row(p["w0"]), p["w_up"], row(p["a0"]), p["a_up"], row(p["k_k"]), row(p["k_a"]), row(p["r_k"]),
      row(p["gn_w"]), row(p["gn_b"]))
    return o_a, s_out


def _merge_kernel(x_ref, oa_ref, ob_ref, ga_ref, gb_ref, wpa_ref, wpb_ref, wo_ref, gout_ref, y_ref):
    pa = jnp.dot(oa_ref[...].astype(bf16), wpa_ref[...], preferred_element_type=f32)
    pb = jnp.dot(ob_ref[...].astype(bf16), wpb_ref[...], preferred_element_type=f32)
    merged = jax.nn.sigmoid(ga_ref[...]) * pa + jax.nn.sigmoid(gb_ref[...]) * pb
    h = x_ref[...] + jnp.dot(merged.astype(bf16), wo_ref[...], preferred_element_type=f32)
    ms = jnp.mean(h * h, axis=-1, keepdims=True)
    y_ref[...] = h * lax.rsqrt(ms + RMS_EPS) * gout_ref[...]


def _merge(x2d, o_a, o_b, u2d, w_pa, w_pb, w_o, norm_out):
    m = x2d.shape[0]
    tm = min(256, m)
    const = lambda shape: pl.BlockSpec(shape, lambda i: (0, 0), pipeline_mode=pl.Buffered(1))
    return pl.pallas_call(
        _merge_kernel,
        out_shape=jax.ShapeDtypeStruct((m, D_MODEL), f32),
        grid=(m // tm,),
        in_specs=[pl.BlockSpec((tm, D_MODEL), lambda i: (i, 0)),
                  pl.BlockSpec((tm, A_WIDTH), lambda i: (i, 0)),
                  pl.BlockSpec((tm, B_WIDTH), lambda i: (i, 0)),
                  pl.BlockSpec((tm, D_MODEL), lambda i: (i, C_GA_M // D_MODEL)),
                  pl.BlockSpec((tm, D_MODEL), lambda i: (i, C_GB_M // D_MODEL)),
                  const((A_WIDTH, D_MODEL)), const((B_WIDTH, D_MODEL)), const((D_MODEL, D_MODEL)),
                  const((1, D_MODEL))],
        out_specs=pl.BlockSpec((tm, D_MODEL), lambda i: (i, 0)),
        compiler_params=_cparams(("parallel",)),
        name="merge_out",
    )(x2d, o_a, o_b, u2d, u2d, w_pa, w_pb, w_o, norm_out.reshape(1, D_MODEL))


def _alibi_slope(head):
    return 2.0 ** (-8.0 * (head + 1) / B_HEADS)


def _slope_row(g, tq):
    lane_head = lax.broadcasted_iota(jnp.int32, (1, B_GROUP * tq), 1) // tq
    out = jnp.zeros((1, B_GROUP * tq), f32)
    for r in range(B_GROUP):
        out = jnp.where(lane_head == r, _alibi_slope(g * B_GROUP + r), out)
    return out


def _tile_heads(row):
    return jnp.concatenate([row] * B_GROUP, axis=1)


def _overlap_t(n_sel_rows, n_ch):
    j = lax.broadcasted_iota(jnp.int32, (n_sel_rows, n_ch), 0) * SEL_BLOCK
    n = lax.broadcasted_iota(jnp.int32, (n_sel_rows, n_ch), 1) * CMP_STRIDE
    ov = jnp.minimum(n + CMP_BLOCK, j + SEL_BLOCK) - jnp.maximum(n, j)
    return jnp.maximum(ov, 0).astype(f32) * (1.0 / CMP_BLOCK)


M_INIT = 0.1 * NEG


def _attn_step_multi(s, carries, v_t):
    n = range(len(s))
    m_new = [jnp.maximum(carries[i][0], jnp.max(s[i], axis=0, keepdims=True)) for i in n]
    alpha = [jnp.exp2(carries[i][0] - m_new[i]) for i in n]
    p = [jnp.exp2(s[i] - m_new[i]) for i in n]
    l = [carries[i][1] * alpha[i] + jnp.sum(p[i], axis=0, keepdims=True) for i in n]
    pv = [jnp.dot(v_t[i], p[i].astype(bf16), preferred_element_type=f32) for i in n]
    return [(m_new[i], l[i], carries[i][2] * alpha[i] + pv[i]) for i in n]


def _attn_init(lanes):
    return (jnp.full((1, lanes), M_INIT, f32), jnp.zeros((1, lanes), f32), jnp.zeros((B_HEAD_DIM, lanes), f32))


def _split3(x):
    hi = x.astype(bf16)
    r1 = x - hi.astype(f32)
    mid = r1.astype(bf16)
    lo = (r1 - mid.astype(f32)).astype(bf16)
    return hi, mid, lo


POS_SPLIT = 128
K_AUG = 16


def _key_aug(pos_i):
    one = jnp.ones(pos_i.shape, f32)
    hi = (pos_i // POS_SPLIT).astype(f32)
    lo = (pos_i % POS_SPLIT).astype(f32)
    zero = jnp.zeros((pos_i.shape[0], K_AUG - 9), f32)
    return jnp.concatenate([one, one, one, hi, hi, hi, lo, lo, lo, zero], axis=1).astype(bf16)


def _query_aug(slope_row, t_row):
    rows = _split3(-slope_row * t_row) + _split3(slope_row * float(POS_SPLIT)) + _split3(slope_row)
    zero = jnp.zeros((K_AUG - 9, slope_row.shape[1]), bf16)
    return jnp.concatenate(list(rows) + [zero], axis=0)


def _attn_finish(carry):
    _, l, acc = carry
    return acc * (1.0 / jnp.maximum(l, 1e-30))


def _compressed_multi(kc, vc_t, q_t, t_row, slope_rows, n_ch):
    n = range(len(kc))
    blk_end = (lax.broadcasted_iota(jnp.int32, (n_ch, 1), 0) * CMP_STRIDE + (CMP_BLOCK - 1)).astype(f32)
    dist = t_row - blk_end
    mask = dist >= 0.0
    s = [jnp.dot(kc[i], q_t[i], preferred_element_type=f32) for i in n]
    s = [jnp.where(mask, s[i] - slope_rows[i] * dist, NEG) for i in n]
    m = [jnp.max(s[i], axis=0, keepdims=True) for i in n]
    p = [jnp.where(mask, jnp.exp2(s[i] - m[i]), 0.0) for i in n]
    inv = [1.0 / jnp.maximum(jnp.sum(p[i], axis=0, keepdims=True), 1e-30) for i in n]
    p = [p[i] * inv[i] for i in n]
    o = [jnp.dot(vc_t[i], p[i].astype(bf16), preferred_element_type=f32) for i in n]
    return o, p


def _selection_scores(p, tq, tok_row_i, n_rows, n_sel, n_ch):
    psum = p[:, 0:tq]
    for r in range(1, B_GROUP):
        psum = psum + p[:, r * tq:(r + 1) * tq]
    imp = jnp.dot(_overlap_t(n_rows, n_ch), psum, preferred_element_type=f32, precision=HI)
    j = lax.broadcasted_iota(jnp.int32, (n_rows, 1), 0)
    back = tok_row_i // SEL_BLOCK - j
    forced = (j == 0) | ((back >= 0) & (back < N_LOCAL))
    score = jnp.where(forced, FORCED_SCORE, jnp.where(back >= 0, imp, -1.0))
    return jnp.where(j < n_sel, score, -2.0)


def _kv_pool_partials(x, pw0, pw1):
    rows = x.shape[0]
    n = rows // CMP_STRIDE
    pool = (lax.broadcasted_iota(jnp.int32, (n, rows), 1) // CMP_STRIDE
            == lax.broadcasted_iota(jnp.int32, (n, rows), 0)).astype(f32)
    a = jnp.dot(pool, x * pw0, preferred_element_type=f32, precision=HI)
    b = jnp.dot(pool, x * pw1, preferred_element_type=f32, precision=HI)
    return a, b


def _compress_kv(pooled, wck, wcv, kc_ref, vct_ref):
    for g in range(B_KV_HEADS):
        kc_ref[g] = _mm(pooled[:, g * 64:(g + 1) * 64], wck).astype(bf16)
        vc = _mm(pooled[:, B_KV_WIDTH + g * 64:B_KV_WIDTH + (g + 1) * 64], wcv)
        vct_ref[g] = vc.T.astype(bf16)


def _nsa_prep_kernel(q_ref, kvc_ref, kvs_ref, kvw_ref, pw0_ref, pw1_ref,
                     qt_ref, ks_ref, vst_ref, kw_ref, vwt_ref, a_ref, b_ref):
    qt_ref[...] = (q_ref[...] * (ATTN_SCALE * LOG2E)).T.astype(bf16)
    tt = q_ref.shape[0]
    aug = _key_aug(pl.program_id(1) * tt + lax.broadcasted_iota(jnp.int32, (tt, 1), 0))
    for src, k_ref, vt_ref in ((kvs_ref, ks_ref, vst_ref), (kvw_ref, kw_ref, vwt_ref)):
        x = src[...]
        for g in range(B_KV_HEADS):
            k_ref[g] = jnp.concatenate([x[:, g * 64:(g + 1) * 64].astype(bf16), aug], axis=1)
        vt_ref[...] = x[:, B_KV_WIDTH:].T.astype(bf16)
    a, b = _kv_pool_partials(kvc_ref[...], pw0_ref[...], pw1_ref[...])
    a_ref[...] = a
    b_ref[...] = b


def _nsa_prep(u3, pw0, pw1, tt):
    B, T, _ = u3.shape
    n_ch = tt // CMP_STRIDE
    ublk = lambda w, off: pl.BlockSpec((None, tt, w), lambda b, t: (b, t, off // w))
    const = pl.BlockSpec((tt, 2 * B_KV_WIDTH), lambda b, t: (0, 0))
    return pl.pallas_call(
        _nsa_prep_kernel,
        out_shape=(jax.ShapeDtypeStruct((B, B_WIDTH, T), bf16),
                   jax.ShapeDtypeStruct((B, B_KV_HEADS, T, B_HEAD_DIM + K_AUG), bf16),
                   jax.ShapeDtypeStruct((B, B_KV_WIDTH, T), bf16),
                   jax.ShapeDtypeStruct((B, B_KV_HEADS, T, B_HEAD_DIM + K_AUG), bf16),
                   jax.ShapeDtypeStruct((B, B_KV_WIDTH, T), bf16),
                   jax.ShapeDtypeStruct((B, T // CMP_STRIDE, 2 * B_KV_WIDTH), f32),
                   jax.ShapeDtypeStruct((B, T // CMP_STRIDE, 2 * B_KV_WIDTH), f32)),
        grid=(B, T // tt),
        in_specs=[ublk(B_WIDTH, C_Q), ublk(2 * B_KV_WIDTH, C_KVC), ublk(2 * B_KV_WIDTH, C_KVS),
                  ublk(2 * B_KV_WIDTH, C_KVW), const, const],
        out_specs=(pl.BlockSpec((None, B_WIDTH, tt), lambda b, t: (b, 0, t)),
                   pl.BlockSpec((None, B_KV_HEADS, tt, B_HEAD_DIM + K_AUG), lambda b, t: (b, 0, t, 0)),
                   pl.BlockSpec((None, B_KV_WIDTH, tt), lambda b, t: (b, 0, t)),
                   pl.BlockSpec((None, B_KV_HEADS, tt, B_HEAD_DIM + K_AUG), lambda b, t: (b, 0, t, 0)),
                   pl.BlockSpec((None, B_KV_WIDTH, tt), lambda b, t: (b, 0, t)),
                   pl.BlockSpec((None, n_ch, 2 * B_KV_WIDTH), lambda b, t: (b, t, 0)),
                   pl.BlockSpec((None, n_ch, 2 * B_KV_WIDTH), lambda b, t: (b, t, 0))),
        compiler_params=_cparams(("parallel", "parallel")),
        name="nsa_prep",
    )(u3, u3, u3, u3, pw0, pw1)


def _rank_select(score, n_sel):
    sub = 8
    assert n_sel % sub == 0
    tiles = [score[r * sub:(r + 1) * sub] for r in range(n_sel // sub)]
    ranks = [jnp.zeros(t.shape, f32) for t in tiles]
    j_in_tile = lax.broadcasted_iota(jnp.int32, (sub, 1), 0)
    for i in range(n_sel):
        row = tiles[i // sub][i % sub:i % sub + 1]
        for r, t in enumerate(tiles):
            if r * sub > i:
                beats = jnp.where(row >= t, 1.0, 0.0)
            elif r * sub + sub - 1 <= i:
                beats = jnp.where(row > t, 1.0, 0.0)
            else:
                beats = jnp.where(row > t, 1.0, jnp.where(row == t, jnp.where(j_in_tile + r * sub > i, 1.0, 0.0), 0.0))
            ranks[r] = ranks[r] + beats
    rank = jnp.concatenate(ranks, axis=0)
    return jnp.where(rank < float(min(N_SELECT, n_sel)), jnp.where(score >= 0.0, 1.0, 0.0), 0.0)


def _nsa_prompt_kernel(qt_ref, ks_ref, vst_ref, kw_ref, vwt_ref, a_ref, b_ref, wck_ref, wcv_ref, gates_ref, gb_ref,
                       o_ref, kc_ref, vct_ref, keep_ref, cmp_ref, qaug_ref, m_ref, l_ref, acc_ref, ot_ref,
                       *, seq, tq):
    i = pl.program_id(1)
    n_ch = seq // CMP_STRIDE
    n_sel = seq // SEL_BLOCK
    lanes = B_GROUP * tq
    tk = 2 * SEL_BLOCK

    @pl.when(i == 0)
    def _():
        pooled = a_ref[...] + pltpu.roll(b_ref[...], n_ch - 1, 0)
        _compress_kv(pooled, wck_ref[...], wcv_ref[...], kc_ref, vct_ref)

    G = B_KV_HEADS
    gs = range(G)
    tok_i = i * tq + lax.broadcasted_iota(jnp.int32, (1, tq), 1)
    t_row = _tile_heads(tok_i.astype(f32))
    key_off = lax.broadcasted_iota(jnp.int32, (tk, 1), 0).astype(f32)
    slopes = [_slope_row(g, tq) * LOG2E for g in gs]

    def q_t(g):
        return jnp.concatenate([qt_ref[(g * B_GROUP + r) * 64:(g * B_GROUP + r + 1) * 64, :] for r in range(B_GROUP)],
                               axis=1)

    o_cmp, prob = _compressed_multi([kc_ref[g] for g in gs], [vct_ref[g] for g in gs], [q_t(g) for g in gs],
                                    t_row, slopes, n_ch)
    for g in gs:
        cmp_ref[g] = o_cmp[g]
    score = jnp.concatenate([_selection_scores(prob[g], tq, tok_i, n_sel, n_sel, n_ch) for g in gs], axis=1)
    keep = _rank_select(score, n_sel)
    for j in range(n_sel):
        keep_ref[j] = (keep[j:j + 1, :] - 1.0) * (-NEG)

    for g in gs:
        qaug_ref[g] = jnp.concatenate([q_t(g), _query_aug(slopes[g], t_row)], axis=0)
    for c in range(2 * G):
        m, l, acc = _attn_init(lanes)
        m_ref[c] = m
        l_ref[c] = l
        acc_ref[c] = acc

    def step(kt, with_window, masked):
        off = pl.multiple_of(kt * tk, tk)
        row0 = keep_ref[2 * kt]
        row1 = keep_ref[2 * kt + 1]
        if masked:
            dist = t_row - (key_off + (kt * tk).astype(f32))
            causal = jnp.where(dist >= 0.0, 0.0, NEG)
            band = jnp.where(dist <= float(WINDOW), causal, NEG)
        s, v, chains = [], [], []
        for g in gs:
            sg = jnp.dot(ks_ref[g, pl.ds(off, tk), :], qaug_ref[g], preferred_element_type=f32)
            if masked:
                sg = sg + causal
            s.append(jnp.concatenate([sg[:SEL_BLOCK] + _tile_heads(row0[:, g * tq:(g + 1) * tq]),
                                      sg[SEL_BLOCK:] + _tile_heads(row1[:, g * tq:(g + 1) * tq])], axis=0))
            v.append(vst_ref[g * 64:(g + 1) * 64, pl.ds(off, tk)])
            chains.append(g)
        if with_window:
            for g in gs:
                sg = jnp.dot(kw_ref[g, pl.ds(off, tk), :], qaug_ref[g], preferred_element_type=f32)
                s.append(sg + band if masked else sg)
                v.append(vwt_ref[g * 64:(g + 1) * 64, pl.ds(off, tk)])
                chains.append(G + g)
        out = _attn_step_multi(s, [(m_ref[c], l_ref[c], acc_ref[c]) for c in chains], v)
        for c, (m, l, acc) in zip(chains, out):
            m_ref[c] = m
            l_ref[c] = l
            acc_ref[c] = acc

    assert tq == tk and WINDOW % tk == 0
    last = i
    lo = jnp.maximum(i - WINDOW // tk, 0)

    def plain_loop(with_window):
        def body(kt, carry):
            step(kt, with_window, False)
            return carry
        return body

    lax.fori_loop(0, lo, plain_loop(False), 0)
    step(lo, True, True)
    lax.fori_loop(lo + 1, last, plain_loop(True), 0)

    @pl.when(last > lo)
    def _():
        step(last, True, True)

    gates_t = jax.nn.sigmoid(gates_ref[...]).T

    def gate_row(branch, g):
        base = branch * B_HEADS + g * B_GROUP
        return jnp.concatenate([gates_t[base + r:base + r + 1, :] for r in range(B_GROUP)], axis=1)

    o_sel = [_attn_finish((m_ref[g], l_ref[g], acc_ref[g])) for g in gs]
    o_win = [_attn_finish((m_ref[G + g], l_ref[G + g], acc_ref[G + g])) for g in gs]
    o_t = [gate_row(0, g) * cmp_ref[g] + gate_row(1, g) * o_sel[g] + gate_row(2, g) * o_win[g] for g in gs]
    for g in gs:
        for r in range(B_GROUP):
            ot_ref[(g * B_GROUP + r) * 64:(g * B_GROUP + r + 1) * 64, :] = o_t[g][:, r * tq:(r + 1) * tq]

    gb = gb_ref[...]
    o_ref[...] = ot_ref[...].T * (gb * jax.nn.sigmoid(gb))


def _nsa_prompt(u3, qt, ks, vst, kw, vwt, a, b, wck, wcv, tq):
    B, T, _ = u3.shape
    n_ch = T // CMP_STRIDE
    n_sel = T // SEL_BLOCK
    lanes = B_GROUP * tq
    per_b3 = lambda s1, s2: pl.BlockSpec((None, s1, s2), lambda b, i: (b, 0, 0))
    per_b4 = pl.BlockSpec((None, B_KV_HEADS, T, B_HEAD_DIM + K_AUG), lambda b, i: (b, 0, 0, 0))
    w_spec = pl.BlockSpec((B_HEAD_DIM, B_HEAD_DIM), lambda b, i: (0, 0))
    return pl.pallas_call(
        functools.partial(_nsa_prompt_kernel, seq=T, tq=tq),
        out_shape=jax.ShapeDtypeStruct((B, T, B_WIDTH), f32),
        grid=(B, T // tq),
        in_specs=[pl.BlockSpec((None, B_WIDTH, tq), lambda b, i: (b, 0, i)),
                  per_b4, per_b3(B_KV_WIDTH, T), per_b4, per_b3(B_KV_WIDTH, T),
                  per_b3(n_ch, 2 * B_KV_WIDTH), per_b3(n_ch, 2 * B_KV_WIDTH), w_spec, w_spec,
                  pl.BlockSpec((None, tq, 128), lambda b, i: (b, i, C_NG // 128)),
                  pl.BlockSpec((None, tq, B_WIDTH), lambda b, i: (b, i, C_GB // B_WIDTH))],
        out_specs=pl.BlockSpec((None, tq, B_WIDTH), lambda b, i: (b, i, 0)),
        scratch_shapes=[pltpu.VMEM((B_KV_HEADS, n_ch, B_HEAD_DIM), bf16),
                        pltpu.VMEM((B_KV_HEADS, B_HEAD_DIM, n_ch), bf16),
                        pltpu.VMEM((n_sel, 1, B_KV_HEADS * tq), f32),
                        pltpu.VMEM((B_KV_HEADS, B_HEAD_DIM, lanes), f32),
                        pltpu.VMEM((B_KV_HEADS, B_HEAD_DIM + K_AUG, lanes), bf16),
                        pltpu.VMEM((2 * B_KV_HEADS, 1, lanes), f32),
                        pltpu.VMEM((2 * B_KV_HEADS, 1, lanes), f32),
                        pltpu.VMEM((2 * B_KV_HEADS, B_HEAD_DIM, lanes), f32),
                        pltpu.VMEM((B_WIDTH, tq), f32)],
        compiler_params=_cparams(("parallel", "arbitrary")),
        name="nsa_prompt",
    )(qt, ks, vst, kw, vwt, a, b, wck, wcv, u3, u3)


def _softmax_rows_update(sc, m, l, acc, pv_fn):
    m_new = jnp.maximum(m, jnp.max(sc, axis=-1, keepdims=True))
    alpha = jnp.exp(m - m_new)
    p = jnp.where(sc > 0.5 * NEG, jnp.exp(sc - m_new), 0.0)
    return m_new, l * alpha + jnp.sum(p, axis=-1, keepdims=True), acc * alpha + pv_fn(p)


def _nsa_sample_kernel(pt_ref, *refs, n_pages, pages_per_step, t_new, t_pad):
    P = pages_per_step
    cmp_pages, sel_pages = refs[:P], refs[P:2 * P]
    (win_ref, q_ref, news_ref, neww_ref, pw0_ref, pw1_ref, pool_ref, wck_ref, wcv_ref, gates_ref, gb_ref,
     o_ref, at_ref, bt_ref, qg_ref, ocmp_ref, keep_ref, m_ref, l_ref, acc_ref) = refs[2 * P:]
    s = pl.program_id(1)
    S = n_pages // P
    G = B_KV_HEADS
    gs = range(G)
    past = n_pages * PAGE_SIZE
    n_ch = past // CMP_STRIDE
    n_sel = past // SEL_BLOCK + 1
    n_j = keep_ref.shape[1]
    R = B_GROUP * t_pad
    row = lax.broadcasted_iota(jnp.int32, (R, 1), 0)
    t_col = (past + row % t_pad).astype(f32)
    slope_col = []
    for g in gs:
        sc_ = jnp.zeros((R, 1), f32)
        for r in range(B_GROUP):
            sc_ = jnp.where(row // t_pad == r, _alibi_slope(g * B_GROUP + r), sc_)
        slope_col.append(sc_)
    ksl = lambda g: slice(g * 64, (g + 1) * 64)
    vsl = lambda g: slice(B_KV_WIDTH + g * 64, B_KV_WIDTH + (g + 1) * 64)

    @pl.when(s < S)
    def _():
        pw0, pw1 = pw0_ref[...], pw1_ref[...]
        xa = jnp.concatenate([(cmp_pages[k][...] * pw0).astype(bf16) for k in range(P)], axis=1)
        xb = jnp.concatenate([(cmp_pages[k][...] * pw1).astype(bf16) for k in range(P)], axis=1)
        n_step = P * PAGE_SIZE // CMP_STRIDE
        off = pl.multiple_of(s * n_step, n_step)
        at_ref[:, pl.ds(off, n_step)] = jnp.dot(xa, pool_ref[...], preferred_element_type=f32)
        bt_ref[:, pl.ds(off, n_step)] = jnp.dot(xb, pool_ref[...], preferred_element_type=f32)

    @pl.when(s == S - 1)
    def _():
        pooled = at_ref[...] + pltpu.roll(bt_ref[...], n_ch - 1, 1)
        q = q_ref[...] * ATTN_SCALE
        qg = [jnp.concatenate([q[:, (g * B_GROUP + r) * 64:(g * B_GROUP + r + 1) * 64] for r in range(B_GROUP)],
                              axis=0).astype(bf16) for g in gs]
        kct = [_mm_tn(wck_ref[...], pooled[ksl(g), :]).astype(bf16) for g in gs]
        vct = [_mm_tn(wcv_ref[...], pooled[vsl(g), :]).astype(bf16) for g in gs]
        blk_end = (lax.broadcasted_iota(jnp.int32, (1, n_ch), 1) * CMP_STRIDE + (CMP_BLOCK - 1)).astype(f32)
        dist = t_col - blk_end
        mask = dist >= 0.0
        sc = [jnp.dot(qg[g], kct[g], preferred_element_type=f32) for g in gs]
        sc = [jnp.where(mask, sc[g] - slope_col[g] * dist, NEG) for g in gs]
        mx = [jnp.max(sc[g], axis=-1, keepdims=True) for g in gs]
        pr = [jnp.where(mask, jnp.exp(sc[g] - mx[g]), 0.0) for g in gs]
        inv = [1.0 / jnp.maximum(jnp.sum(pr[g], axis=-1, keepdims=True), 1e-30) for g in gs]
        pr = [pr[g] * inv[g] for g in gs]
        for g in gs:
            qg_ref[g] = qg[g]
            ocmp_ref[g] = _mm_nt(pr[g], vct[g])
            m_ref[g] = jnp.full((R, 1), NEG, f32)
            l_ref[g] = jnp.zeros((R, 1), f32)
            acc_ref[g] = jnp.zeros((R, B_HEAD_DIM), f32)
        psum = [pr[g][0:t_pad] for g in gs]
        for r in range(1, B_GROUP):
            psum = [psum[g] + pr[g][r * t_pad:(r + 1) * t_pad] for g in gs]
        psum = jnp.concatenate(psum, axis=0)
        n_i = lax.broadcasted_iota(jnp.int32, (n_ch, n_j), 0) * CMP_STRIDE
        j_i = lax.broadcasted_iota(jnp.int32, (n_ch, n_j), 1) * SEL_BLOCK
        overlap = jnp.maximum(jnp.minimum(n_i + CMP_BLOCK, j_i + SEL_BLOCK) - jnp.maximum(n_i, j_i), 0).astype(f32) \
            * (1.0 / CMP_BLOCK)
        imp = jnp.dot(psum, overlap, preferred_element_type=f32, precision=HI)
        j = lax.broadcasted_iota(jnp.int32, (1, n_j), 1)
        tok = past + lax.broadcasted_iota(jnp.int32, (G * t_pad, 1), 0) % t_pad
        back = tok // SEL_BLOCK - j
        forced = (j == 0) | ((back >= 0) & (back < N_LOCAL))
        score = jnp.where(forced, FORCED_SCORE, jnp.where(back >= 0, imp, -1.0))
        score = jnp.where(j < n_sel, score, -2.0)
        rank = jnp.zeros(score.shape, f32)
        for i in range(n_sel):
            col = score[:, i:i + 1]
            rank = rank + jnp.where(col > score, 1.0, jnp.where(col == score, jnp.where(j > i, 1.0, 0.0), 0.0))
        keep_ref[...] = jnp.where(rank < float(min(N_SELECT, n_sel)), jnp.where(score >= 0.0, 1.0, 0.0), 0.0)

    @pl.when(s >= S)
    def _():
        ss = s - S
        n_blk = P * PAGE_SIZE // SEL_BLOCK
        n_key = P * PAGE_SIZE
        pick = (lax.broadcasted_iota(jnp.int32, (n_j, n_blk), 0)
                == lax.broadcasted_iota(jnp.int32, (n_j, n_blk), 1) + ss * n_blk).astype(bf16)
        keep_blk = jnp.dot(keep_ref[...].astype(bf16), pick, preferred_element_type=f32)
        expand = (lax.broadcasted_iota(jnp.int32, (n_blk, n_key), 0)
                  == lax.broadcasted_iota(jnp.int32, (n_blk, n_key), 1) // SEL_BLOCK).astype(bf16)
        bias = (jnp.dot(keep_blk.astype(bf16), expand, preferred_element_type=f32) - 1.0) * (-NEG)
        kpos = (ss * n_key + lax.broadcasted_iota(jnp.int32, (1, n_key), 1)).astype(f32)
        dist = t_col - kpos
        kt = [jnp.concatenate([sel_pages[k][ksl(g), :].astype(bf16) for k in range(P)], axis=1) for g in gs]
        vt = [jnp.concatenate([sel_pages[k][vsl(g), :].astype(bf16) for k in range(P)], axis=1) for g in gs]
        sc = [jnp.dot(qg_ref[g], kt[g], preferred_element_type=f32) for g in gs]
        sc = [sc[g] - slope_col[g] * dist + jnp.concatenate([bias[g * t_pad:(g + 1) * t_pad]] * B_GROUP, axis=0)
              for g in gs]
        m_new = [jnp.maximum(m_ref[g], jnp.max(sc[g], axis=-1, keepdims=True)) for g in gs]
        alpha = [jnp.exp(m_ref[g] - m_new[g]) for g in gs]
        pr = [jnp.where(sc[g] > 0.5 * NEG, jnp.exp(sc[g] - m_new[g]), 0.0) for g in gs]
        l_new = [l_ref[g] * alpha[g] + jnp.sum(pr[g], axis=-1, keepdims=True) for g in gs]
        pv = [_mm_nt(pr[g], vt[g]) for g in gs]
        for g in gs:
            acc_ref[g] = acc_ref[g] * alpha[g] + pv[g]
            m_ref[g] = m_new[g]
            l_ref[g] = l_new[g]

    @pl.when(s == 2 * S - 1)
    def _():
        xs = news_ref[...]
        xw = neww_ref[...]
        win = win_ref[...]
        n_win = win.shape[1]
        new_dist = t_col - (past + lax.broadcasted_iota(jnp.int32, (1, t_pad), 1)).astype(f32)
        win_dist = t_col - (past - n_win + lax.broadcasted_iota(jnp.int32, (1, n_win), 1)).astype(f32)
        keep_new = keep_ref[:, n_sel - 1:n_sel]
        gates = jax.nn.sigmoid(gates_ref[...])
        gb = gb_ref[...]
        silu_gb = gb * jax.nn.sigmoid(gb)
        qg = [qg_ref[g] for g in gs]
        sc = [_mm_nt(qg[g], xs[:, ksl(g)]) - slope_col[g] * new_dist for g in gs]
        sc = [jnp.where(new_dist >= 0.0, sc[g], NEG)
              + (jnp.concatenate([keep_new[g * t_pad:(g + 1) * t_pad]] * B_GROUP, axis=0) - 1.0) * (-NEG) for g in gs]
        sel = [_softmax_rows_update(sc[g], m_ref[g], l_ref[g], acc_ref[g], lambda p, g=g: _mm(p, xs[:, vsl(g)]))
               for g in gs]
        o_sel = [sel[g][2] * (1.0 / jnp.maximum(sel[g][1], 1e-30)) for g in gs]
        sw = [jnp.dot(qg[g], win[ksl(g), :].astype(bf16), preferred_element_type=f32) - slope_col[g] * win_dist
              for g in gs]
        sw = [jnp.where(win_dist <= float(WINDOW), sw[g], NEG) for g in gs]
        sn = [_mm_nt(qg[g], xw[:, ksl(g)]) - slope_col[g] * new_dist for g in gs]
        sn = [jnp.where(new_dist >= 0.0, jnp.where(new_dist <= float(WINDOW), sn[g], NEG), NEG) for g in gs]
        mw = [jnp.maximum(jnp.max(sw[g], axis=-1, keepdims=True), jnp.max(sn[g], axis=-1, keepdims=True)) for g in gs]
        pw = [jnp.where(sw[g] > 0.5 * NEG, jnp.exp(sw[g] - mw[g]), 0.0) for g in gs]
        pn = [jnp.where(sn[g] > 0.5 * NEG, jnp.exp(sn[g] - mw[g]), 0.0) for g in gs]
        lw = [jnp.sum(pw[g], axis=-1, keepdims=True) + jnp.sum(pn[g], axis=-1, keepdims=True) for g in gs]
        o_win = [(_mm_nt(pw[g], win[vsl(g), :]) + _mm(pn[g], xw[:, vsl(g)])) * (1.0 / jnp.maximum(lw[g], 1e-30))
                 for g in gs]
        for g in gs:
            def gate_col(branch):
                base = branch * B_HEADS + g * B_GROUP
                return jnp.concatenate([gates[:, base + r:base + r + 1] for r in range(B_GROUP)], axis=0)

            o = gate_col(0) * ocmp_ref[g] + gate_col(1) * o_sel[g] + gate_col(2) * o_win[g]
            for r in range(B_GROUP):
                cols = slice((g * B_GROUP + r) * 64, (g * B_GROUP + r + 1) * 64)
                o_ref[:, cols] = o[r * t_pad:(r + 1) * t_pad, :] * silu_gb[:, cols]


def _nsa_sample(u3, cache_cmp_t, cache_sel_t, cache_win_t, page_table, pw0_t, pw1_t, wck, wcv, *, t_new, pages_per_step):
    DB, t_pad, _ = u3.shape
    n_pages = page_table.shape[1]
    P = pages_per_step
    assert t_new < CMP_STRIDE and t_new <= t_pad and n_pages % P == 0 and (P * PAGE_SIZE // CMP_STRIDE) % 128 == 0
    S = n_pages // P
    past = n_pages * PAGE_SIZE
    n_ch = past // CMP_STRIDE
    n_sel = past // SEL_BLOCK + 1
    n_j = -(-n_sel // 128) * 128
    R = B_GROUP * t_pad
    kvw = 2 * B_KV_WIDTH
    pool = (lax.broadcasted_iota(jnp.int32, (P * PAGE_SIZE, P * PAGE_SIZE // CMP_STRIDE), 0) // CMP_STRIDE
            == lax.broadcasted_iota(jnp.int32, (P * PAGE_SIZE, P * PAGE_SIZE // CMP_STRIDE), 1)).astype(bf16)
    ublk = lambda w, off: pl.BlockSpec((None, t_pad, w), lambda b, s, pt: (b, 0, off // w))
    const = lambda s1, s2: pl.BlockSpec((s1, s2), lambda b, s, pt: (0, 0))
    page = lambda fn: pl.BlockSpec((None, kvw, PAGE_SIZE), fn)
    cmp_specs = [page(lambda b, s, pt, k=k: (pt[b, jnp.minimum(s, S - 1) * P + k], 0, 0)) for k in range(P)]
    sel_specs = [page(lambda b, s, pt, k=k: (pt[b, jnp.maximum(s - S, 0) * P + k], 0, 0)) for k in range(P)]
    grid_spec = pltpu.PrefetchScalarGridSpec(
        num_scalar_prefetch=1,
        grid=(DB, 2 * S),
        in_specs=cmp_specs + sel_specs + [
            pl.BlockSpec((None, kvw, cache_win_t.shape[2]), lambda b, s, pt: (b, 0, 0)),
            ublk(B_WIDTH, C_Q), ublk(kvw, C_KVS), ublk(kvw, C_KVW),
            const(kvw, PAGE_SIZE), const(kvw, PAGE_SIZE), const(*pool.shape),
            const(B_HEAD_DIM, B_HEAD_DIM), const(B_HEAD_DIM, B_HEAD_DIM),
            ublk(128, C_NG), ublk(B_WIDTH, C_GB)],
        out_specs=pl.BlockSpec((None, t_pad, B_WIDTH), lambda b, s, pt: (b, 0, 0)),
        scratch_shapes=[pltpu.VMEM((kvw, n_ch), f32), pltpu.VMEM((kvw, n_ch), f32),
                        pltpu.VMEM((B_KV_HEADS, R, B_HEAD_DIM), bf16),
                        pltpu.VMEM((B_KV_HEADS, R, B_HEAD_DIM), f32),
                        pltpu.VMEM((B_KV_HEADS * t_pad, n_j), f32),
                        pltpu.VMEM((B_KV_HEADS, R, 1), f32), pltpu.VMEM((B_KV_HEADS, R, 1), f32),
                        pltpu.VMEM((B_KV_HEADS, R, B_HEAD_DIM), f32)])
    return pl.pallas_call(
        functools.partial(_nsa_sample_kernel, n_pages=n_pages, pages_per_step=P, t_new=t_new, t_pad=t_pad),
        out_shape=jax.ShapeDtypeStruct((DB, t_pad, B_WIDTH), f32),
        grid_spec=grid_spec,
        compiler_params=_cparams(("parallel", "arbitrary")),
        name="nsa_sample",
    )(page_table, *([cache_cmp_t] * P), *([cache_sel_t] * P), cache_win_t, u3, u3, u3, pw0_t, pw1_t, pool, wck, wcv,
      u3, u3)


def _pos_weight_tiles(pos_k, pos_v, rows):
    def half(lo):
        t = jnp.concatenate([jnp.tile(pos_k[lo:lo + CMP_STRIDE], (1, B_KV_HEADS)),
                             jnp.tile(pos_v[lo:lo + CMP_STRIDE], (1, B_KV_HEADS))], axis=1)
        return jnp.tile(t, (rows // CMP_STRIDE, 1))
    return half(0), half(CMP_STRIDE)


def _a_cols(u_rows):
    return jnp.concatenate([u_rows[..., :4 * A_WIDTH], u_rows[..., C_LR:C_LR + DECAY_RANK + ICLR_RANK]], axis=-1)


def kernel(x_prompt, x_sample, cache_cmp_kv, cache_sel_kv, cache_win_kv, state_rwkv, state_shift, page_table, norm_in,
           w_in, mu_shift, w0, w_up, a0, a_up, k_k, k_a, r_k, gn_w, gn_b, cmp_pos_k, cmp_pos_v, w_cmp_k, w_cmp_v,
           w_pa, w_pb, w_o, norm_out):
    assert w_in.shape[0] == 1, "one layer"
    B, T, _ = x_prompt.shape
    DB, TS, _ = x_sample.shape
    H, N = A_HEADS, A_HEAD_DIM
    kvs = (2, B_KV_HEADS, B_HEAD_DIM)
    p = dict(mu_shift=mu_shift[0], w0=w0[0], w_up=w_up[0], a0=a0[0], a_up=a_up[0], k_k=k_k[0], k_a=k_a[0],
             r_k=r_k[0].reshape(-1), gn_w=gn_w[0], gn_b=gn_b[0])
    w_packed = _pack_w_in(w_in[0])
    wpa, wpb, wo = w_pa[0].astype(bf16), w_pb[0].astype(bf16), w_o[0].astype(bf16)

    xp = x_prompt.reshape(B * T, D_MODEL)
    up = _proj_in(xp, norm_in[0], w_packed)
    up3 = up.reshape(B, T, U_COLS)
    oa_p, s_p = _rwkv_wide(up3, jnp.zeros((B, 1, 4 * A_WIDTH), f32), jnp.zeros((B, 1, 128), f32),
                      jnp.zeros((B, H, N, N), f32), p, chunk=64, t_valid=64, rows=math.gcd(B, 4))
    tt = 512
    pw0, pw1 = _pos_weight_tiles(cmp_pos_k[0], cmp_pos_v[0], tt)
    qt, ks, vst, kw, vwt, pa, pb = _nsa_prep(up3, pw0, pw1, tt)
    ob_p = _nsa_prompt(up3, qt, ks, vst, kw, vwt, pa, pb, w_cmp_k[0], w_cmp_v[0], 128)
    y_p = _merge(xp, oa_p.reshape(B * T, A_WIDTH), ob_p.reshape(B * T, B_WIDTH), up, wpa, wpb, wo, norm_out)

    t_pad = 8
    xs = x_sample.reshape(DB * TS, D_MODEL)
    us = _proj_in(xs, norm_in[0], w_packed)
    us3 = us.reshape(DB, TS, U_COLS)
    us3p = jnp.pad(us3, ((0, 0), (0, t_pad - TS), (0, 0)))
    shift0 = state_shift[0]
    oa_s, s_s = _rwkv_wide(us3p, shift0[:, None, :4 * A_WIDTH], shift0[:, None, 4 * A_WIDTH:], state_rwkv[0], p,
                      chunk=t_pad, t_valid=TS, rows=2 if DB % 2 == 0 else 1)
    n_pool = cache_cmp_kv.shape[1]
    win = cache_win_kv[0]
    rows_last = lambda c, lead: jnp.transpose(c, (0, 2, 3, 4, 1)).reshape(lead, 2 * B_KV_WIDTH, c.shape[1])
    ob_s = _nsa_sample(us3p, rows_last(cache_cmp_kv[0], n_pool), rows_last(cache_sel_kv[0], n_pool),
                       rows_last(win, DB), page_table, pw0[:PAGE_SIZE].T, pw1[:PAGE_SIZE].T, w_cmp_k[0], w_cmp_v[0],
                       t_new=TS, pages_per_step=min(16, page_table.shape[1]))
    y_s = _merge(xs, oa_s[:, :TS].reshape(DB * TS, A_WIDTH), ob_s[:, :TS].reshape(DB * TS, B_WIDTH), us, wpa, wpb, wo,
                 norm_out)

    def kv_out(u3_, col, lead, t):
        return u3_[..., col:col + 2 * B_KV_WIDTH].reshape((1, lead, t) + kvs)

    wk = min(WINDOW, T)
    new_w_s = kv_out(us3, C_KVW, DB, TS)[0]
    s_win = jnp.concatenate([win, new_w_s], axis=1)[:, TS:][None]
    return (y_p.reshape(B, T, D_MODEL), y_s.reshape(DB, TS, D_MODEL),
            kv_out(up3, C_KVC, B, T), kv_out(up3, C_KVS, B, T), kv_out(up3[:, T - wk:], C_KVW, B, wk),
            s_p[None], _a_cols(up3[:, T - 1])[None],
            kv_out(us3, C_KVC, DB, TS), kv_out(us3, C_KVS, DB, TS), s_win,
            s_s[None], _a_cols(us3[:, TS - 1])[None])
```

```python
import functools
import math

import jax
import jax.numpy as jnp
from jax import lax
from jax.experimental import pallas as pl
from jax.experimental.pallas import tpu as pltpu

f32 = jnp.float32
bf16 = jnp.bfloat16

D_MODEL = 2048
PAGE_SIZE = 128
A_HEADS = 16
A_HEAD_DIM = 64
A_WIDTH = A_HEADS * A_HEAD_DIM
DECAY_RANK = 64
ICLR_RANK = 64
A_COLS = 4 * A_WIDTH + DECAY_RANK + ICLR_RANK
GN_EPS = 64e-5
B_HEADS = 16
B_KV_HEADS = 4
B_GROUP = B_HEADS // B_KV_HEADS
B_HEAD_DIM = 64
B_WIDTH = B_HEADS * B_HEAD_DIM
B_KV_WIDTH = B_KV_HEADS * B_HEAD_DIM
B_COLS = 2 * B_WIDTH + 6 * B_KV_WIDTH + 3 * B_HEADS
CMP_BLOCK = 32
CMP_STRIDE = 16
SEL_BLOCK = 64
N_SELECT = 16
N_LOCAL = 2
WINDOW = 512
FORCED_SCORE = 1e4
ATTN_SCALE = B_HEAD_DIM ** -0.5
RMS_EPS = 1e-6
NEG = -1e30
LOG2E = math.log2(math.e)

C_RKVG = 0
C_Q = 4096
C_GB = 5120
C_GA_M = 6144
C_GB_M = 8192
C_KVC = 10240
C_KVS = 10752
C_KVW = 11264
C_LR = 11776
C_NG = 11904
U_COLS = 12288

VMEM_LIMIT = 56 * 1024 * 1024
HI = lax.Precision.HIGHEST


def _cparams(sem):
    return pltpu.CompilerParams(dimension_semantics=sem, vmem_limit_bytes=VMEM_LIMIT)


def _pack_w_in(w):
    wt = w.T
    a, b, m = wt[:A_COLS], wt[A_COLS:A_COLS + B_COLS], wt[A_COLS + B_COLS:]
    z = jnp.zeros((U_COLS - C_NG - 3 * B_HEADS, w.shape[0]), w.dtype)
    return jnp.concatenate(
        [a[:4 * A_WIDTH], b[:2 * B_WIDTH], m, b[2 * B_WIDTH:2 * B_WIDTH + 6 * B_KV_WIDTH],
         a[4 * A_WIDTH:], b[2 * B_WIDTH + 6 * B_KV_WIDTH:], z], axis=0).astype(bf16)


def _proj_in_kernel(x_ref, g_ref, w_ref, o_ref, xn_ref):
    @pl.when(pl.program_id(1) == 0)
    def _():
        x = x_ref[...]
        ms = jnp.mean(x * x, axis=-1, keepdims=True)
        xn_ref[...] = (x * lax.rsqrt(ms + RMS_EPS) * g_ref[...]).astype(bf16)

    o_ref[...] = lax.dot_general(xn_ref[...], w_ref[...], (((1,), (1,)), ((), ())), preferred_element_type=f32)


def _proj_in(x2d, norm_g, w_packed_t):
    m = x2d.shape[0]
    tm = min(1024, m)
    tn = 1024
    return pl.pallas_call(
        _proj_in_kernel,
        out_shape=jax.ShapeDtypeStruct((m, U_COLS), f32),
        grid=(m // tm, U_COLS // tn),
        in_specs=[pl.BlockSpec((tm, D_MODEL), lambda i, j: (i, 0)),
                  pl.BlockSpec((1, D_MODEL), lambda i, j: (0, 0)),
                  pl.BlockSpec((tn, D_MODEL), lambda i, j: (j, 0))],
        out_specs=pl.BlockSpec((tm, tn), lambda i, j: (i, j)),
        scratch_shapes=[pltpu.VMEM((tm, D_MODEL), bf16)],
        compiler_params=_cparams(("parallel", "arbitrary")),
        name="proj_in",
    )(x2d, norm_g.reshape(1, D_MODEL), w_packed_t)


def _mm(a, b):
    return jnp.dot(a.astype(bf16), b.astype(bf16), preferred_element_type=f32)


def _mm_nt(a, b):
    return lax.dot_general(a.astype(bf16), b.astype(bf16), (((1,), (1,)), ((), ())), preferred_element_type=f32)


def _mm_tn(a, b):
    return lax.dot_general(a.astype(bf16), b.astype(bf16), (((0,), (0,)), ((), ())), preferred_element_type=f32)


HEADS_PER_GROUP = 4
N_HEAD_GROUPS = A_HEADS // HEADS_PER_GROUP
GROUP_WIDTH = HEADS_PER_GROUP * A_HEAD_DIM


def _rwkv_wide_kernel(um_ref, ulr_ref, pm_ref, plr_ref, s0_ref, mum_ref, mulr_ref, w0_ref, wup_ref, a0_ref,
                      aup_ref, kk_ref, ka_ref, rk_ref, gnw_ref, gnb_ref, o_ref, sout_ref,
                      s_ref, prevm_ref, prevlr_ref, *, chunk, t_valid):
    C = chunk
    N, HG, NG, GW = A_HEAD_DIM, HEADS_PER_GROUP, N_HEAD_GROUPS, GROUP_WIDTH
    RB = um_ref.shape[0]
    c = pl.program_id(1)

    @pl.when(c == 0)
    def _():
        for bi in range(RB):
            for q in range(NG):
                s_ref[bi, q] = jnp.concatenate([s0_ref[bi, q * HG + h] for h in range(HG)], axis=1)
        prevm_ref[...] = pm_ref[...]
        prevlr_ref[...] = plr_ref[...]

    row = lax.broadcasted_iota(jnp.int32, (C, 1), 0)
    tril_b = (lax.broadcasted_iota(jnp.int32, (C, C), 0) >= lax.broadcasted_iota(jnp.int32, (C, C), 1)).astype(bf16)
    t_i = lax.broadcasted_iota(jnp.int32, (C, HG * C), 0)
    s_i = lax.broadcasted_iota(jnp.int32, (C, HG * C), 1) % C
    tril_incl = t_i >= s_i
    tril_strict = t_i > s_i
    eye = (t_i == s_i).astype(f32)
    head_of_dim = lax.broadcasted_iota(jnp.int32, (1, GW), 1) // N
    head_of_tok = lax.broadcasted_iota(jnp.int32, (1, HG * C), 1) // C
    ones_bd = (lax.broadcasted_iota(jnp.int32, (GW, GW), 0) // N
               == lax.broadcasted_iota(jnp.int32, (GW, GW), 1) // N).astype(bf16)
    n_sq = int(math.log2(C)) - 1
    valid = (row < t_valid).astype(f32) if t_valid < C else None
    gcols = [slice(q * GW, (q + 1) * GW) for q in range(NG)]
    qs = range(NG)

    def shifted(u, prev):
        return jnp.where(row == 0, prev, pltpu.roll(u, 1, 0))

    def head_sums(*xs):
        stacked = jnp.concatenate([x.astype(bf16)[:, gc] for x in xs for gc in gcols], axis=0)
        sums = jnp.dot(stacked, ones_bd, preferred_element_type=f32)
        return [jnp.concatenate([sums[(i * NG + q) * C:(i * NG + q + 1) * C] for q in qs], axis=1)
                for i in range(len(xs))]

    def block_rows(x, head_of_lane):
        return jnp.concatenate([jnp.where(head_of_lane == h, x, 0.0) for h in range(HG)], axis=0).astype(bf16)

    def mm(a, b):
        return jnp.dot(a.astype(bf16), b, preferred_element_type=f32)

    def mm_nt(a, b):
        return lax.dot_general(a.astype(bf16), b, (((1,), (1,)), ((), ())), preferred_element_type=f32)

    def prepare(bi, out):
        um = um_ref[bi]
        ulr = ulr_ref[bi]
        prev_m = prevm_ref[bi]
        prev_lr = prevlr_ref[bi]
        prevm_ref[bi] = um[C - 1:C, :]
        prevlr_ref[bi] = ulr[C - 1:C, :]
        uslr = ulr + mulr_ref[...] * (shifted(ulr, prev_lr) - ulr)
        zw = w0_ref[...] + _mm(jnp.tanh(uslr[:, :DECAY_RANK]), wup_ref[...])
        a = jax.nn.sigmoid(a0_ref[...] + _mm(uslr[:, DECAY_RANK:], aup_ref[...]))
        yield
        secs = []
        for i in range(4):
            cols = slice(i * A_WIDTH, (i + 1) * A_WIDTH)
            secs.append(um[:, cols] + mum_ref[:, cols] * (shifted(um[:, cols], prev_m[:, cols]) - um[:, cols]))
            yield
        r, k, v, g = secs
        logw = -math.exp(-0.5) * jax.nn.sigmoid(zw)
        if valid is not None:
            logw = logw * valid
        yield
        logw_hi = logw.astype(bf16)
        logw_lo = (logw - logw_hi.astype(f32)).astype(bf16)
        cum = (jnp.dot(tril_b, logw_hi, preferred_element_type=f32)
               + jnp.dot(tril_b, logw_lo, preferred_element_type=f32))
        yield
        e_pos = jnp.exp(cum)
        e_neg = jnp.exp(-cum)
        yield
        e_prev = jnp.exp(cum - logw)
        kk = k * kk_ref[...]
        yield
        k2 = k * (1.0 + (a - 1.0) * ka_ref[...])
        yield
        ssq, bonus = head_sums(kk * kk, r * k2 * rk_ref[...])
        yield
        kk = kk * lax.rsqrt(jnp.maximum(ssq, 1e-24))
        kn = k2 * e_neg
        bn = kk * a * e_neg
        if valid is not None:
            kn = kn * valid
            bn = bn * valid
        yield
        lhs = jnp.concatenate([kk * e_prev, r * e_pos], axis=0).astype(bf16)
        out.update(v=v, g=g, bonus=bonus, kn=kn, bn=bn, lhs=lhs, e_last=e_pos[C - 1:C, :])
        yield

    def solve(bi, pre, out):
        v = [pre["v"][:, gc] for gc in gcols]
        kn = [pre["kn"][:, gc] for gc in gcols]
        bn = [pre["bn"][:, gc] for gc in gcols]
        lhs = [pre["lhs"][:, gc] for gc in gcols]
        s0 = [s_ref[bi, q] for q in qs]
        qk_b = [mm_nt(lhs[q], block_rows(bn[q], head_of_dim)) for q in qs]
        yield
        qk_k = [mm_nt(lhs[q], block_rows(kn[q], head_of_dim)) for q in qs]
        yield
        w0s = [mm_nt(lhs[q], block_rows(s0[q], head_of_dim)) for q in qs]
        yield
        pw = [jnp.where(tril_strict, -qk_b[q][:C], 0.0) for q in qs]
        tinv = [eye + pw[q] for q in qs]
        bd = [block_rows(pw[q], head_of_tok) for q in qs]
        for _ in range(n_sq):
            pw = [mm(pw[q], bd[q]) for q in qs]
            yield
            bd = [block_rows(pw[q], head_of_tok) for q in qs]
            tinv = [tinv[q] + mm(tinv[q], bd[q]) for q in qs]
            yield
        bd_v = [block_rows(v[q], head_of_dim) for q in qs]
        rhs_u = [w0s[q][:C] + mm(jnp.where(tril_strict, qk_k[q][:C], 0.0), bd_v[q]) for q in qs]
        yield
        u = [mm(tinv[q], block_rows(rhs_u[q], head_of_dim)) for q in qs]
        yield
        p_cat = [jnp.concatenate([jnp.where(tril_incl, qk_k[q][C:], 0.0), jnp.where(tril_incl, -qk_b[q][C:], 0.0)],
                                 axis=1) for q in qs]
        out["o"] = [w0s[q][C:] + mm(p_cat[q], jnp.concatenate([bd_v[q], block_rows(u[q], head_of_dim)], axis=0))
                    for q in qs]
        yield
        full = [lax.dot_general(jnp.concatenate([v[q], -u[q]], axis=0).astype(bf16),
                                jnp.concatenate([kn[q], bn[q]], axis=0).astype(bf16),
                                (((0,), (0,)), ((), ())), preferred_element_type=f32) for q in qs]
        yield
        for q in qs:
            upd = jnp.where(head_of_dim == 0, full[q][0:N], 0.0)
            for h in range(1, HG):
                upd = upd + jnp.where(head_of_dim == h, full[q][h * N:(h + 1) * N], 0.0)
            s_ref[bi, q] = (s0[q] + upd) * pre["e_last"][:, gcols[q]]
        yield

    def finish(bi, pre, mid):
        o = jnp.concatenate(mid["o"], axis=1)
        mean = head_sums(o)[0] * (1.0 / N)
        yield
        d = o - mean
        var = head_sums(d * d)[0] * (1.0 / N)
        yield
        g = pre["g"]
        on = d * lax.rsqrt(var + GN_EPS) * gnw_ref[...] + gnb_ref[...] + pre["bonus"] * pre["v"]
        o_ref[bi] = on * (g * jax.nn.sigmoid(g))
        yield

    def interleave(gens):
        gens = list(gens)
        while gens:
            for gen in list(gens):
                if next(gen, "done") == "done":
                    gens.remove(gen)

    pre = [{} for _ in range(RB)]
    mid = [{} for _ in range(RB)]
    halves = [range(0, RB // 2), range(RB // 2, RB)] if RB > 1 else [range(RB)]
    prep = lambda rows_: [prepare(bi, pre[bi]) for bi in rows_]
    solv = lambda rows_: [solve(bi, pre[bi], mid[bi]) for bi in rows_]
    fin = lambda rows_: [finish(bi, pre[bi], mid[bi]) for bi in rows_]
    interleave(prep(halves[0]))
    for i, rows_ in enumerate(halves):
        interleave(solv(rows_) + (prep(halves[i + 1]) if i + 1 < len(halves) else [])
                   + (fin(halves[i - 1]) if i >= 1 else []))
    interleave(fin(halves[-1]))

    @pl.when(c == pl.num_programs(1) - 1)
    def _():
        for bi in range(RB):
            for q in range(NG):
                s_q = s_ref[bi, q]
                for h in range(HG):
                    sout_ref[bi, q * HG + h] = s_q[:, h * N:(h + 1) * N]


def _rwkv_wide(u3, prev_m, prev_lr, s0, p, *, chunk, t_valid, rows):
    B, T, _ = u3.shape
    C = chunk
    NG, N, GW = N_HEAD_GROUPS, A_HEAD_DIM, GROUP_WIDTH
    assert B % rows == 0 and T % C == 0
    row = lambda v: v.reshape(1, -1)
    vec = lambda n: pl.BlockSpec((1, n), lambda b, c: (0, 0))
    in_specs = [
        pl.BlockSpec((rows, C, 4 * A_WIDTH), lambda b, c: (b, c, 0)),
        pl.BlockSpec((rows, C, 128), lambda b, c: (b, c, C_LR // 128)),
        pl.BlockSpec((rows, 1, 4 * A_WIDTH), lambda b, c: (b, 0, 0)),
        pl.BlockSpec((rows, 1, 128), lambda b, c: (b, 0, 0)),
        pl.BlockSpec((rows, A_HEADS, N, N), lambda b, c: (b, 0, 0, 0)),
        vec(4 * A_WIDTH), vec(128), vec(A_WIDTH),
        pl.BlockSpec((DECAY_RANK, A_WIDTH), lambda b, c: (0, 0)),
        vec(A_WIDTH),
        pl.BlockSpec((ICLR_RANK, A_WIDTH), lambda b, c: (0, 0)),
        vec(A_WIDTH), vec(A_WIDTH), vec(A_WIDTH), vec(A_WIDTH), vec(A_WIDTH),
    ]
    o_a, s_out = pl.pallas_call(
        functools.partial(_rwkv_wide_kernel, chunk=C, t_valid=t_valid),
        out_shape=(jax.ShapeDtypeStruct((B, T, A_WIDTH), f32), jax.ShapeDtypeStruct((B, A_HEADS, N, N), f32)),
        grid=(B // rows, T // C),
        in_specs=in_specs,
        out_specs=(pl.BlockSpec((rows, C, A_WIDTH), lambda b, c: (b, c, 0)),
                   pl.BlockSpec((rows, A_HEADS, N, N), lambda b, c: (b, 0, 0, 0))),
        scratch_shapes=[pltpu.VMEM((rows, NG, N, GW), f32), pltpu.VMEM((rows, 1, 4 * A_WIDTH), f32),
                        pltpu.VMEM((rows, 1, 128), f32)],
        compiler_params=_cparams(("parallel", "arbitrary")),
        name="rwkv7",
    )(u3, u3, prev_m, prev_lr, s0, row(p["mu_shift"][:4 * A_WIDTH]), row(p["mu_shift"][4 * A_WIDTH:]),
      row(p["w0"]), p["w_up"], row(p["a0"]), p["a_up"], row(p["k_k"]), row(p["k_a"]), row(p["r_k"]),
      row(p["gn_w"]), row(p["gn_b"]))
    return o_a, s_out


def _merge_kernel(x_ref, oa_ref, ob_ref, ga_ref, gb_ref, wpa_ref, wpb_ref, wo_ref, gout_ref, y_ref):
    pa = jnp.dot(oa_ref[...].astype(bf16), wpa_ref[...], preferred_element_type=f32)
    pb = jnp.dot(ob_ref[...].astype(bf16), wpb_ref[...], preferred_element_type=f32)
    merged = jax.nn.sigmoid(ga_ref[...]) * pa + jax.nn.sigmoid(gb_ref[...]) * pb
    h = x_ref[...] + jnp.dot(merged.astype(bf16), wo_ref[...], preferred_element_type=f32)
    ms = jnp.mean(h * h, axis=-1, keepdims=True)
    y_ref[...] = h * lax.rsqrt(ms + RMS_EPS) * gout_ref[...]


def _merge(x2d, o_a, o_b, u2d, w_pa, w_pb, w_o, norm_out):
    m = x2d.shape[0]
    tm = min(256, m)
    const = lambda shape: pl.BlockSpec(shape, lambda i: (0, 0), pipeline_mode=pl.Buffered(1))
    return pl.pallas_call(
        _merge_kernel,
        out_shape=jax.ShapeDtypeStruct((m, D_MODEL), f32),
        grid=(m // tm,),
        in_specs=[pl.BlockSpec((tm, D_MODEL), lambda i: (i, 0)),
                  pl.BlockSpec((tm, A_WIDTH), lambda i: (i, 0)),
                  pl.BlockSpec((tm, B_WIDTH), lambda i: (i, 0)),
                  pl.BlockSpec((tm, D_MODEL), lambda i: (i, C_GA_M // D_MODEL)),
                  pl.BlockSpec((tm, D_MODEL), lambda i: (i, C_GB_M // D_MODEL)),
                  const((A_WIDTH, D_MODEL)), const((B_WIDTH, D_MODEL)), const((D_MODEL, D_MODEL)),
                  const((1, D_MODEL))],
        out_specs=pl.BlockSpec((tm, D_MODEL), lambda i: (i, 0)),
        compiler_params=_cparams(("parallel",)),
        name="merge_out",
    )(x2d, o_a, o_b, u2d, u2d, w_pa, w_pb, w_o, norm_out.reshape(1, D_MODEL))


def _alibi_slope(head):
    return 2.0 ** (-8.0 * (head + 1) / B_HEADS)


def _slope_row(g, tq):
    lane_head = lax.broadcasted_iota(jnp.int32, (1, B_GROUP * tq), 1) // tq
    out = jnp.zeros((1, B_GROUP * tq), f32)
    for r in range(B_GROUP):
        out = jnp.where(lane_head == r, _alibi_slope(g * B_GROUP + r), out)
    return out


def _tile_heads(row):
    return jnp.concatenate([row] * B_GROUP, axis=1)


def _overlap_t(n_sel_rows, n_ch):
    j = lax.broadcasted_iota(jnp.int32, (n_sel_rows, n_ch), 0) * SEL_BLOCK
    n = lax.broadcasted_iota(jnp.int32, (n_sel_rows, n_ch), 1) * CMP_STRIDE
    ov = jnp.minimum(n + CMP_BLOCK, j + SEL_BLOCK) - jnp.maximum(n, j)
    return jnp.maximum(ov, 0).astype(f32) * (1.0 / CMP_BLOCK)


M_INIT = 0.1 * NEG


def _attn_step_multi(s, carries, v_t):
    n = range(len(s))
    m_new = [jnp.maximum(carries[i][0], jnp.max(s[i], axis=0, keepdims=True)) for i in n]
    alpha = [jnp.exp2(carries[i][0] - m_new[i]) for i in n]
    p = [jnp.exp2(s[i] - m_new[i]) for i in n]
    l = [carries[i][1] * alpha[i] + jnp.sum(p[i], axis=0, keepdims=True) for i in n]
    pv = [jnp.dot(v_t[i], p[i].astype(bf16), preferred_element_type=f32) for i in n]
    return [(m_new[i], l[i], carries[i][2] * alpha[i] + pv[i]) for i in n]


def _attn_init(lanes):
    return (jnp.full((1, lanes), M_INIT, f32), jnp.zeros((1, lanes), f32), jnp.zeros((B_HEAD_DIM, lanes), f32))


def _split3(x):
    hi = x.astype(bf16)
    r1 = x - hi.astype(f32)
    mid = r1.astype(bf16)
    lo = (r1 - mid.astype(f32)).astype(bf16)
    return hi, mid, lo


POS_SPLIT = 128
K_AUG = 16


def _key_aug(pos_i):
    one = jnp.ones(pos_i.shape, f32)
    hi = (pos_i // POS_SPLIT).astype(f32)
    lo = (pos_i % POS_SPLIT).astype(f32)
    zero = jnp.zeros((pos_i.shape[0], K_AUG - 9), f32)
    return jnp.concatenate([one, one, one, hi, hi, hi, lo, lo, lo, zero], axis=1).astype(bf16)


def _query_aug(slope_row, t_row):
    rows = _split3(-slope_row * t_row) + _split3(slope_row * float(POS_SPLIT)) + _split3(slope_row)
    zero = jnp.zeros((K_AUG - 9, slope_row.shape[1]), bf16)
    return jnp.concatenate(list(rows) + [zero], axis=0)


def _attn_finish(carry):
    _, l, acc = carry
    return acc * (1.0 / jnp.maximum(l, 1e-30))


def _compressed_multi(kc, vc_t, q_t, t_row, slope_rows, n_ch):
    n = range(len(kc))
    blk_end = (lax.broadcasted_iota(jnp.int32, (n_ch, 1), 0) * CMP_STRIDE + (CMP_BLOCK - 1)).astype(f32)
    dist = t_row - blk_end
    mask = dist >= 0.0
    s = [jnp.dot(kc[i], q_t[i], preferred_element_type=f32) for i in n]
    s = [jnp.where(mask, s[i] - slope_rows[i] * dist, NEG) for i in n]
    m = [jnp.max(s[i], axis=0, keepdims=True) for i in n]
    p = [jnp.where(mask, jnp.exp2(s[i] - m[i]), 0.0) for i in n]
    inv = [1.0 / jnp.maximum(jnp.sum(p[i], axis=0, keepdims=True), 1e-30) for i in n]
    p = [p[i] * inv[i] for i in n]
    o = [jnp.dot(vc_t[i], p[i].astype(bf16), preferred_element_type=f32) for i in n]
    return o, p


def _selection_scores(p, tq, tok_row_i, n_rows, n_sel, n_ch):
    psum = p[:, 0:tq]
    for r in range(1, B_GROUP):
        psum = psum + p[:, r * tq:(r + 1) * tq]
    imp = jnp.dot(_overlap_t(n_rows, n_ch), psum, preferred_element_type=f32, precision=HI)
    j = lax.broadcasted_iota(jnp.int32, (n_rows, 1), 0)
    back = tok_row_i // SEL_BLOCK - j
    forced = (j == 0) | ((back >= 0) & (back < N_LOCAL))
    score = jnp.where(forced, FORCED_SCORE, jnp.where(back >= 0, imp, -1.0))
    return jnp.where(j < n_sel, score, -2.0)


def _kv_pool_partials(x, pw0, pw1):
    rows = x.shape[0]
    n = rows // CMP_STRIDE
    pool = (lax.broadcasted_iota(jnp.int32, (n, rows), 1) // CMP_STRIDE
            == lax.broadcasted_iota(jnp.int32, (n, rows), 0)).astype(f32)
    a = jnp.dot(pool, x * pw0, preferred_element_type=f32, precision=HI)
    b = jnp.dot(pool, x * pw1, preferred_element_type=f32, precision=HI)
    return a, b


def _compress_kv(pooled, wck, wcv, kc_ref, vct_ref):
    for g in range(B_KV_HEADS):
        kc_ref[g] = _mm(pooled[:, g * 64:(g + 1) * 64], wck).astype(bf16)
        vc = _mm(pooled[:, B_KV_WIDTH + g * 64:B_KV_WIDTH + (g + 1) * 64], wcv)
        vct_ref[g] = vc.T.astype(bf16)


def _nsa_prep_kernel(q_ref, kvc_ref, kvs_ref, kvw_ref, pw0_ref, pw1_ref,
                     qt_ref, ks_ref, vst_ref, kw_ref, vwt_ref, a_ref, b_ref):
    qt_ref[...] = (q_ref[...] * (ATTN_SCALE * LOG2E)).T.astype(bf16)
    tt = q_ref.shape[0]
    aug = _key_aug(pl.program_id(1) * tt + lax.broadcasted_iota(jnp.int32, (tt, 1), 0))
    for src, k_ref, vt_ref in ((kvs_ref, ks_ref, vst_ref), (kvw_ref, kw_ref, vwt_ref)):
        x = src[...]
        for g in range(B_KV_HEADS):
            k_ref[g] = jnp.concatenate([x[:, g * 64:(g + 1) * 64].astype(bf16), aug], axis=1)
        vt_ref[...] = x[:, B_KV_WIDTH:].T.astype(bf16)
    a, b = _kv_pool_partials(kvc_ref[...], pw0_ref[...], pw1_ref[...])
    a_ref[...] = a
    b_ref[...] = b


def _nsa_prep(u3, pw0, pw1, tt):
    B, T, _ = u3.shape
    n_ch = tt // CMP_STRIDE
    ublk = lambda w, off: pl.BlockSpec((None, tt, w), lambda b, t: (b, t, off // w))
    const = pl.BlockSpec((tt, 2 * B_KV_WIDTH), lambda b, t: (0, 0))
    return pl.pallas_call(
        _nsa_prep_kernel,
        out_shape=(jax.ShapeDtypeStruct((B, B_WIDTH, T), bf16),
                   jax.ShapeDtypeStruct((B, B_KV_HEADS, T, B_HEAD_DIM + K_AUG), bf16),
                   jax.ShapeDtypeStruct((B, B_KV_WIDTH, T), bf16),
                   jax.ShapeDtypeStruct((B, B_KV_HEADS, T, B_HEAD_DIM + K_AUG), bf16),
                   jax.ShapeDtypeStruct((B, B_KV_WIDTH, T), bf16),
                   jax.ShapeDtypeStruct((B, T // CMP_STRIDE, 2 * B_KV_WIDTH), f32),
                   jax.ShapeDtypeStruct((B, T // CMP_STRIDE, 2 * B_KV_WIDTH), f32)),
        grid=(B, T // tt),
        in_specs=[ublk(B_WIDTH, C_Q), ublk(2 * B_KV_WIDTH, C_KVC), ublk(2 * B_KV_WIDTH, C_KVS),
                  ublk(2 * B_KV_WIDTH, C_KVW), const, const],
        out_specs=(pl.BlockSpec((None, B_WIDTH, tt), lambda b, t: (b, 0, t)),
                   pl.BlockSpec((None, B_KV_HEADS, tt, B_HEAD_DIM + K_AUG), lambda b, t: (b, 0, t, 0)),
                   pl.BlockSpec((None, B_KV_WIDTH, tt), lambda b, t: (b, 0, t)),
                   pl.BlockSpec((None, B_KV_HEADS, tt, B_HEAD_DIM + K_AUG), lambda b, t: (b, 0, t, 0)),
                   pl.BlockSpec((None, B_KV_WIDTH, tt), lambda b, t: (b, 0, t)),
                   pl.BlockSpec((None, n_ch, 2 * B_KV_WIDTH), lambda b, t: (b, t, 0)),
                   pl.BlockSpec((None, n_ch, 2 * B_KV_WIDTH), lambda b, t: (b, t, 0))),
        compiler_params=_cparams(("parallel", "parallel")),
        name="nsa_prep",
    )(u3, u3, u3, u3, pw0, pw1)


def _rank_select(score, n_sel):
    sub = 8
    assert n_sel % sub == 0
    tiles = [score[r * sub:(r + 1) * sub] for r in range(n_sel // sub)]
    ranks = [jnp.zeros(t.shape, f32) for t in tiles]
    j_in_tile = lax.broadcasted_iota(jnp.int32, (sub, 1), 0)
    for i in range(n_sel):
        row = tiles[i // sub][i % sub:i % sub + 1]
        for r, t in enumerate(tiles):
            if r * sub > i:
                beats = jnp.where(row >= t, 1.0, 0.0)
            elif r * sub + sub - 1 <= i:
                beats = jnp.where(row > t, 1.0, 0.0)
            else:
                beats = jnp.where(row > t, 1.0, jnp.where(row == t, jnp.where(j_in_tile + r * sub > i, 1.0, 0.0), 0.0))
            ranks[r] = ranks[r] + beats
    rank = jnp.concatenate(ranks, axis=0)
    return jnp.where(rank < float(min(N_SELECT, n_sel)), jnp.where(score >= 0.0, 1.0, 0.0), 0.0)


def _nsa_prompt_kernel(qt_ref, ks_ref, vst_ref, kw_ref, vwt_ref, a_ref, b_ref, wck_ref, wcv_ref, gates_ref, gb_ref,
                       o_ref, kc_ref, vct_ref, keep_ref, cmp_ref, qaug_ref, m_ref, l_ref, acc_ref, ot_ref,
                       *, seq, tq):
    i = pl.program_id(1)
    n_ch = seq // CMP_STRIDE
    n_sel = seq // SEL_BLOCK
    lanes = B_GROUP * tq
    tk = 2 * SEL_BLOCK

    @pl.when(i == 0)
    def _():
        pooled = a_ref[...] + pltpu.roll(b_ref[...], n_ch - 1, 0)
        _compress_kv(pooled, wck_ref[...], wcv_ref[...], kc_ref, vct_ref)

    G = B_KV_HEADS
    gs = range(G)
    tok_i = i * tq + lax.broadcasted_iota(jnp.int32, (1, tq), 1)
    t_row = _tile_heads(tok_i.astype(f32))
    key_off = lax.broadcasted_iota(jnp.int32, (tk, 1), 0).astype(f32)
    slopes = [_slope_row(g, tq) * LOG2E for g in gs]

    def q_t(g):
        return jnp.concatenate([qt_ref[(g * B_GROUP + r) * 64:(g * B_GROUP + r + 1) * 64, :] for r in range(B_GROUP)],
                               axis=1)

    o_cmp, prob = _compressed_multi([kc_ref[g] for g in gs], [vct_ref[g] for g in gs], [q_t(g) for g in gs],
                                    t_row, slopes, n_ch)
    for g in gs:
        cmp_ref[g] = o_cmp[g]
    score = jnp.concatenate([_selection_scores(prob[g], tq, tok_i, n_sel, n_sel, n_ch) for g in gs], axis=1)
    keep = _rank_select(score, n_sel)
    for j in range(n_sel):
        keep_ref[j] = (keep[j:j + 1, :] - 1.0) * (-NEG)

    for g in gs:
        qaug_ref[g] = jnp.concatenate([q_t(g), _query_aug(slopes[g], t_row)], axis=0)
    for c in range(2 * G):
        m, l, acc = _attn_init(lanes)
        m_ref[c] = m
        l_ref[c] = l
        acc_ref[c] = acc

    def step(kt, with_window, masked):
        off = pl.multiple_of(kt * tk, tk)
        row0 = keep_ref[2 * kt]
        row1 = keep_ref[2 * kt + 1]
        if masked:
            dist = t_row - (key_off + (kt * tk).astype(f32))
            causal = jnp.where(dist >= 0.0, 0.0, NEG)
            band = jnp.where(dist <= float(WINDOW), causal, NEG)
        s, v, chains = [], [], []
        for g in gs:
            sg = jnp.dot(ks_ref[g, pl.ds(off, tk), :], qaug_ref[g], preferred_element_type=f32)
            if masked:
                sg = sg + causal
            s.append(jnp.concatenate([sg[:SEL_BLOCK] + _tile_heads(row0[:, g * tq:(g + 1) * tq]),
                                      sg[SEL_BLOCK:] + _tile_heads(row1[:, g * tq:(g + 1) * tq])], axis=0))
            v.append(vst_ref[g * 64:(g + 1) * 64, pl.ds(off, tk)])
            chains.append(g)
        if with_window:
            for g in gs:
                sg = jnp.dot(kw_ref[g, pl.ds(off, tk), :], qaug_ref[g], preferred_element_type=f32)
                s.append(sg + band if masked else sg)
                v.append(vwt_ref[g * 64:(g + 1) * 64, pl.ds(off, tk)])
                chains.append(G + g)
        out = _attn_step_multi(s, [(m_ref[c], l_ref[c], acc_ref[c]) for c in chains], v)
        for c, (m, l, acc) in zip(chains, out):
            m_ref[c] = m
            l_ref[c] = l
            acc_ref[c] = acc

    assert tq == tk and WINDOW % tk == 0
    last = i
    lo = jnp.maximum(i - WINDOW // tk, 0)

    def plain_loop(with_window):
        def body(kt, carry):
            step(kt, with_window, False)
            return carry
        return body

    lax.fori_loop(0, lo, plain_loop(False), 0)
    step(lo, True, True)
    lax.fori_loop(lo + 1, last, plain_loop(True), 0)

    @pl.when(last > lo)
    def _():
        step(last, True, True)

    gates_t = jax.nn.sigmoid(gates_ref[...]).T

    def gate_row(branch, g):
        base = branch * B_HEADS + g * B_GROUP
        return jnp.concatenate([gates_t[base + r:base + r + 1, :] for r in range(B_GROUP)], axis=1)

    o_sel = [_attn_finish((m_ref[g], l_ref[g], acc_ref[g])) for g in gs]
    o_win = [_attn_finish((m_ref[G + g], l_ref[G + g], acc_ref[G + g])) for g in gs]
    o_t = [gate_row(0, g) * cmp_ref[g] + gate_row(1, g) * o_sel[g] + gate_row(2, g) * o_win[g] for g in gs]
    for g in gs:
        for r in range(B_GROUP):
            ot_ref[(g * B_GROUP + r) * 64:(g * B_GROUP + r + 1) * 64, :] = o_t[g][:, r * tq:(r + 1) * tq]

    gb = gb_ref[...]
    o_ref[...] = ot_ref[...].T * (gb * jax.nn.sigmoid(gb))


def _nsa_prompt(u3, qt, ks, vst, kw, vwt, a, b, wck, wcv, tq):
    B, T, _ = u3.shape
    n_ch = T // CMP_STRIDE
    n_sel = T // SEL_BLOCK
    lanes = B_GROUP * tq
    per_b3 = lambda s1, s2: pl.BlockSpec((None, s1, s2), lambda b, i: (b, 0, 0))
    per_b4 = pl.BlockSpec((None, B_KV_HEADS, T, B_HEAD_DIM + K_AUG), lambda b, i: (b, 0, 0, 0))
    w_spec = pl.BlockSpec((B_HEAD_DIM, B_HEAD_DIM), lambda b, i: (0, 0))
    return pl.pallas_call(
        functools.partial(_nsa_prompt_kernel, seq=T, tq=tq),
        out_shape=jax.ShapeDtypeStruct((B, T, B_WIDTH), f32),
        grid=(B, T // tq),
        in_specs=[pl.BlockSpec((None, B_WIDTH, tq), lambda b, i: (b, 0, i)),
                  per_b4, per_b3(B_KV_WIDTH, T), per_b4, per_b3(B_KV_WIDTH, T),
                  per_b3(n_ch, 2 * B_KV_WIDTH), per_b3(n_ch, 2 * B_KV_WIDTH), w_spec, w_spec,
                  pl.BlockSpec((None, tq, 128), lambda b, i: (b, i, C_NG // 128)),
                  pl.BlockSpec((None, tq, B_WIDTH), lambda b, i: (b, i, C_GB // B_WIDTH))],
        out_specs=pl.BlockSpec((None, tq, B_WIDTH), lambda b, i: (b, i, 0)),
        scratch_shapes=[pltpu.VMEM((B_KV_HEADS, n_ch, B_HEAD_DIM), bf16),
                        pltpu.VMEM((B_KV_HEADS, B_HEAD_DIM, n_ch), bf16),
                        pltpu.VMEM((n_sel, 1, B_KV_HEADS * tq), f32),
                        pltpu.VMEM((B_KV_HEADS, B_HEAD_DIM, lanes), f32),
                        pltpu.VMEM((B_KV_HEADS, B_HEAD_DIM + K_AUG, lanes), bf16),
                        pltpu.VMEM((2 * B_KV_HEADS, 1, lanes), f32),
                        pltpu.VMEM((2 * B_KV_HEADS, 1, lanes), f32),
                        pltpu.VMEM((2 * B_KV_HEADS, B_HEAD_DIM, lanes), f32),
                        pltpu.VMEM((B_WIDTH, tq), f32)],
        compiler_params=_cparams(("parallel", "arbitrary")),
        name="nsa_prompt",
    )(qt, ks, vst, kw, vwt, a, b, wck, wcv, u3, u3)


def _softmax_rows_update(sc, m, l, acc, pv_fn):
    m_new = jnp.maximum(m, jnp.max(sc, axis=-1, keepdims=True))
    alpha = jnp.exp(m - m_new)
    p = jnp.where(sc > 0.5 * NEG, jnp.exp(sc - m_new), 0.0)
    return m_new, l * alpha + jnp.sum(p, axis=-1, keepdims=True), acc * alpha + pv_fn(p)


def _nsa_sample_kernel(pt_ref, *refs, n_pages, pages_per_step, t_new, t_pad):
    P = pages_per_step
    cmp_pages, sel_pages = refs[:P], refs[P:2 * P]
    (win_ref, q_ref, news_ref, neww_ref, pw0_ref, pw1_ref, pool_ref, wck_ref, wcv_ref, gates_ref, gb_ref,
     o_ref, at_ref, bt_ref, qg_ref, ocmp_ref, keep_ref, m_ref, l_ref, acc_ref) = refs[2 * P:]
    s = pl.program_id(1)
    S = n_pages // P
    G = B_KV_HEADS
    gs = range(G)
    past = n_pages * PAGE_SIZE
    n_ch = past // CMP_STRIDE
    n_sel = past // SEL_BLOCK + 1
    n_j = keep_ref.shape[1]
    R = B_GROUP * t_pad
    row = lax.broadcasted_iota(jnp.int32, (R, 1), 0)
    t_col = (past + row % t_pad).astype(f32)
    slope_col = []
    for g in gs:
        sc_ = jnp.zeros((R, 1), f32)
        for r in range(B_GROUP):
            sc_ = jnp.where(row // t_pad == r, _alibi_slope(g * B_GROUP + r), sc_)
        slope_col.append(sc_)
    ksl = lambda g: slice(g * 64, (g + 1) * 64)
    vsl = lambda g: slice(B_KV_WIDTH + g * 64, B_KV_WIDTH + (g + 1) * 64)

    @pl.when(s < S)
    def _():
        pw0, pw1 = pw0_ref[...], pw1_ref[...]
        xa = jnp.concatenate([(cmp_pages[k][...] * pw0).astype(bf16) for k in range(P)], axis=1)
        xb = jnp.concatenate([(cmp_pages[k][...] * pw1).astype(bf16) for k in range(P)], axis=1)
        n_step = P * PAGE_SIZE // CMP_STRIDE
        off = pl.multiple_of(s * n_step, n_step)
        at_ref[:, pl.ds(off, n_step)] = jnp.dot(xa, pool_ref[...], preferred_element_type=f32)
        bt_ref[:, pl.ds(off, n_step)] = jnp.dot(xb, pool_ref[...], preferred_element_type=f32)

    @pl.when(s == S - 1)
    def _():
        pooled = at_ref[...] + pltpu.roll(bt_ref[...], n_ch - 1, 1)
        q = q_ref[...] * ATTN_SCALE
        qg = [jnp.concatenate([q[:, (g * B_GROUP + r) * 64:(g * B_GROUP + r + 1) * 64] for r in range(B_GROUP)],
                              axis=0).astype(bf16) for g in gs]
        kct = [_mm_tn(wck_ref[...], pooled[ksl(g), :]).astype(bf16) for g in gs]
        vct = [_mm_tn(wcv_ref[...], pooled[vsl(g), :]).astype(bf16) for g in gs]
        blk_end = (lax.broadcasted_iota(jnp.int32, (1, n_ch), 1) * CMP_STRIDE + (CMP_BLOCK - 1)).astype(f32)
        dist = t_col - blk_end
        mask = dist >= 0.0
        sc = [jnp.dot(qg[g], kct[g], preferred_element_type=f32) for g in gs]
        sc = [jnp.where(mask, sc[g] - slope_col[g] * dist, NEG) for g in gs]
        mx = [jnp.max(sc[g], axis=-1, keepdims=True) for g in gs]
        pr = [jnp.where(mask, jnp.exp(sc[g] - mx[g]), 0.0) for g in gs]
        inv = [1.0 / jnp.maximum(jnp.sum(pr[g], axis=-1, keepdims=True), 1e-30) for g in gs]
        pr = [pr[g] * inv[g] for g in gs]
        for g in gs:
            qg_ref[g] = qg[g]
            ocmp_ref[g] = _mm_nt(pr[g], vct[g])
            m_ref[g] = jnp.full((R, 1), NEG, f32)
            l_ref[g] = jnp.zeros((R, 1), f32)
            acc_ref[g] = jnp.zeros((R, B_HEAD_DIM), f32)
        psum = [pr[g][0:t_pad] for g in gs]
        for r in range(1, B_GROUP):
            psum = [psum[g] + pr[g][r * t_pad:(r + 1) * t_pad] for g in gs]
        psum = jnp.concatenate(psum, axis=0)
        n_i = lax.broadcasted_iota(jnp.int32, (n_ch, n_j), 0) * CMP_STRIDE
        j_i = lax.broadcasted_iota(jnp.int32, (n_ch, n_j), 1) * SEL_BLOCK
        overlap = jnp.maximum(jnp.minimum(n_i + CMP_BLOCK, j_i + SEL_BLOCK) - jnp.maximum(n_i, j_i), 0).astype(f32) \
            * (1.0 / CMP_BLOCK)
        imp = jnp.dot(psum, overlap, preferred_element_type=f32, precision=HI)
        j = lax.broadcasted_iota(jnp.int32, (1, n_j), 1)
        tok = past + lax.broadcasted_iota(jnp.int32, (G * t_pad, 1), 0) % t_pad
        back = tok // SEL_BLOCK - j
        forced = (j == 0) | ((back >= 0) & (back < N_LOCAL))
        score = jnp.where(forced, FORCED_SCORE, jnp.where(back >= 0, imp, -1.0))
        score = jnp.where(j < n_sel, score, -2.0)
        rank = jnp.zeros(score.shape, f32)
        for i in range(n_sel):
            col = score[:, i:i + 1]
            rank = rank + jnp.where(col > score, 1.0, jnp.where(col == score, jnp.where(j > i, 1.0, 0.0), 0.0))
        keep_ref[...] = jnp.where(rank < float(min(N_SELECT, n_sel)), jnp.where(score >= 0.0, 1.0, 0.0), 0.0)

    @pl.when(s >= S)
    def _():
        ss = s - S
        n_blk = P * PAGE_SIZE // SEL_BLOCK
        n_key = P * PAGE_SIZE
        pick = (lax.broadcasted_iota(jnp.int32, (n_j, n_blk), 0)
                == lax.broadcasted_iota(jnp.int32, (n_j, n_blk), 1) + ss * n_blk).astype(bf16)
        keep_blk = jnp.dot(keep_ref[...].astype(bf16), pick, preferred_element_type=f32)
        expand = (lax.broadcasted_iota(jnp.int32, (n_blk, n_key), 0)
                  == lax.broadcasted_iota(jnp.int32, (n_blk, n_key), 1) // SEL_BLOCK).astype(bf16)
        bias = (jnp.dot(keep_blk.astype(bf16), expand, preferred_element_type=f32) - 1.0) * (-NEG)
        kpos = (ss * n_key + lax.broadcasted_iota(jnp.int32, (1, n_key), 1)).astype(f32)
        dist = t_col - kpos
        kt = [jnp.concatenate([sel_pages[k][ksl(g), :].astype(bf16) for k in range(P)], axis=1) for g in gs]
        vt = [jnp.concatenate([sel_pages[k][vsl(g), :].astype(bf16) for k in range(P)], axis=1) for g in gs]
        sc = [jnp.dot(qg_ref[g], kt[g], preferred_element_type=f32) for g in gs]
        sc = [sc[g] - slope_col[g] * dist + jnp.concatenate([bias[g * t_pad:(g + 1) * t_pad]] * B_GROUP, axis=0)
              for g in gs]
        m_new = [jnp.maximum(m_ref[g], jnp.max(sc[g], axis=-1, keepdims=True)) for g in gs]
        alpha = [jnp.exp(m_ref[g] - m_new[g]) for g in gs]
        pr = [jnp.where(sc[g] > 0.5 * NEG, jnp.exp(sc[g] - m_new[g]), 0.0) for g in gs]
        l_new = [l_ref[g] * alpha[g] + jnp.sum(pr[g], axis=-1, keepdims=True) for g in gs]
        pv = [_mm_nt(pr[g], vt[g]) for g in gs]
        for g in gs:
            acc_ref[g] = acc_ref[g] * alpha[g] + pv[g]
            m_ref[g] = m_new[g]
            l_ref[g] = l_new[g]

    @pl.when(s == 2 * S - 1)
    def _():
        xs = news_ref[...]
        xw = neww_ref[...]
        win = win_ref[...]
        n_win = win.shape[1]
        new_dist = t_col - (past + lax.broadcasted_iota(jnp.int32, (1, t_pad), 1)).astype(f32)
        win_dist = t_col - (past - n_win + lax.broadcasted_iota(jnp.int32, (1, n_win), 1)).astype(f32)
        keep_new = keep_ref[:, n_sel - 1:n_sel]
        gates = jax.nn.sigmoid(gates_ref[...])
        gb = gb_ref[...]
        silu_gb = gb * jax.nn.sigmoid(gb)
        qg = [qg_ref[g] for g in gs]
        sc = [_mm_nt(qg[g], xs[:, ksl(g)]) - slope_col[g] * new_dist for g in gs]
        sc = [jnp.where(new_dist >= 0.0, sc[g], NEG)
              + (jnp.concatenate([keep_new[g * t_pad:(g + 1) * t_pad]] * B_GROUP, axis=0) - 1.0) * (-NEG) for g in gs]
        sel = [_softmax_rows_update(sc[g], m_ref[g], l_ref[g], acc_ref[g], lambda p, g=g: _mm(p, xs[:, vsl(g)]))
               for g in gs]
        o_sel = [sel[g][2] * (1.0 / jnp.maximum(sel[g][1], 1e-30)) for g in gs]
        sw = [jnp.dot(qg[g], win[ksl(g), :].astype(bf16), preferred_element_type=f32) - slope_col[g] * win_dist
              for g in gs]
        sw = [jnp.where(win_dist <= float(WINDOW), sw[g], NEG) for g in gs]
        sn = [_mm_nt(qg[g], xw[:, ksl(g)]) - slope_col[g] * new_dist for g in gs]
        sn = [jnp.where(new_dist >= 0.0, jnp.where(new_dist <= float(WINDOW), sn[g], NEG), NEG) for g in gs]
        mw = [jnp.maximum(jnp.max(sw[g], axis=-1, keepdims=True), jnp.max(sn[g], axis=-1, keepdims=True)) for g in gs]
        pw = [jnp.where(sw[g] > 0.5 * NEG, jnp.exp(sw[g] - mw[g]), 0.0) for g in gs]
        pn = [jnp.where(sn[g] > 0.5 * NEG, jnp.exp(sn[g] - mw[g]), 0.0) for g in gs]
        lw = [jnp.sum(pw[g], axis=-1, keepdims=True) + jnp.sum(pn[g], axis=-1, keepdims=True) for g in gs]
        o_win = [(_mm_nt(pw[g], win[vsl(g), :]) + _mm(pn[g], xw[:, vsl(g)])) * (1.0 / jnp.maximum(lw[g], 1e-30))
                 for g in gs]
        for g in gs:
            def gate_col(branch):
                base = branch * B_HEADS + g * B_GROUP
                return jnp.concatenate([gates[:, base + r:base + r + 1] for r in range(B_GROUP)], axis=0)

            o = gate_col(0) * ocmp_ref[g] + gate_col(1) * o_sel[g] + gate_col(2) * o_win[g]
            for r in range(B_GROUP):
                cols = slice((g * B_GROUP + r) * 64, (g * B_GROUP + r + 1) * 64)
                o_ref[:, cols] = o[r * t_pad:(r + 1) * t_pad, :] * silu_gb[:, cols]


def _nsa_sample(u3, cache_cmp_t, cache_sel_t, cache_win_t, page_table, pw0_t, pw1_t, wck, wcv, *, t_new, pages_per_step):
    DB, t_pad, _ = u3.shape
    n_pages = page_table.shape[1]
    P = pages_per_step
    assert t_new < CMP_STRIDE and t_new <= t_pad and n_pages % P == 0 and (P * PAGE_SIZE // CMP_STRIDE) % 128 == 0
    S = n_pages // P
    past = n_pages * PAGE_SIZE
    n_ch = past // CMP_STRIDE
    n_sel = past // SEL_BLOCK + 1
    n_j = -(-n_sel // 128) * 128
    R = B_GROUP * t_pad
    kvw = 2 * B_KV_WIDTH
    pool = (lax.broadcasted_iota(jnp.int32, (P * PAGE_SIZE, P * PAGE_SIZE // CMP_STRIDE), 0) // CMP_STRIDE
            == lax.broadcasted_iota(jnp.int32, (P * PAGE_SIZE, P * PAGE_SIZE // CMP_STRIDE), 1)).astype(bf16)
    ublk = lambda w, off: pl.BlockSpec((None, t_pad, w), lambda b, s, pt: (b, 0, off // w))
    const = lambda s1, s2: pl.BlockSpec((s1, s2), lambda b, s, pt: (0, 0))
    page = lambda fn: pl.BlockSpec((None, kvw, PAGE_SIZE), fn)
    cmp_specs = [page(lambda b, s, pt, k=k: (pt[b, jnp.minimum(s, S - 1) * P + k], 0, 0)) for k in range(P)]
    sel_specs = [page(lambda b, s, pt, k=k: (pt[b, jnp.maximum(s - S, 0) * P + k], 0, 0)) for k in range(P)]
    grid_spec = pltpu.PrefetchScalarGridSpec(
        num_scalar_prefetch=1,
        grid=(DB, 2 * S),
        in_specs=cmp_specs + sel_specs + [
            pl.BlockSpec((None, kvw, cache_win_t.shape[2]), lambda b, s, pt: (b, 0, 0)),
            ublk(B_WIDTH, C_Q), ublk(kvw, C_KVS), ublk(kvw, C_KVW),
            const(kvw, PAGE_SIZE), const(kvw, PAGE_SIZE), const(*pool.shape),
            const(B_HEAD_DIM, B_HEAD_DIM), const(B_HEAD_DIM, B_HEAD_DIM),
            ublk(128, C_NG), ublk(B_WIDTH, C_GB)],
        out_specs=pl.BlockSpec((None, t_pad, B_WIDTH), lambda b, s, pt: (b, 0, 0)),
        scratch_shapes=[pltpu.VMEM((kvw, n_ch), f32), pltpu.VMEM((kvw, n_ch), f32),
                        pltpu.VMEM((B_KV_HEADS, R, B_HEAD_DIM), bf16),
                        pltpu.VMEM((B_KV_HEADS, R, B_HEAD_DIM), f32),
                        pltpu.VMEM((B_KV_HEADS * t_pad, n_j), f32),
                        pltpu.VMEM((B_KV_HEADS, R, 1), f32), pltpu.VMEM((B_KV_HEADS, R, 1), f32),
                        pltpu.VMEM((B_KV_HEADS, R, B_HEAD_DIM), f32)])
    return pl.pallas_call(
        functools.partial(_nsa_sample_kernel, n_pages=n_pages, pages_per_step=P, t_new=t_new, t_pad=t_pad),
        out_shape=jax.ShapeDtypeStruct((DB, t_pad, B_WIDTH), f32),
        grid_spec=grid_spec,
        compiler_params=_cparams(("parallel", "arbitrary")),
        name="nsa_sample",
    )(page_table, *([cache_cmp_t] * P), *([cache_sel_t] * P), cache_win_t, u3, u3, u3, pw0_t, pw1_t, pool, wck, wcv,
      u3, u3)


def _pos_weight_tiles(pos_k, pos_v, rows):
    def half(lo):
        t = jnp.concatenate([jnp.tile(pos_k[lo:lo + CMP_STRIDE], (1, B_KV_HEADS)),
                             jnp.tile(pos_v[lo:lo + CMP_STRIDE], (1, B_KV_HEADS))], axis=1)
        return jnp.tile(t, (rows // CMP_STRIDE, 1))
    return half(0), half(CMP_STRIDE)


def _a_cols(u_rows):
    return jnp.concatenate([u_rows[..., :4 * A_WIDTH], u_rows[..., C_LR:C_LR + DECAY_RANK + ICLR_RANK]], axis=-1)


def kernel(x_prompt, x_sample, cache_cmp_kv, cache_sel_kv, cache_win_kv, state_rwkv, state_shift, page_table, norm_in,
           w_in, mu_shift, w0, w_up, a0, a_up, k_k, k_a, r_k, gn_w, gn_b, cmp_pos_k, cmp_pos_v, w_cmp_k, w_cmp_v,
           w_pa, w_pb, w_o, norm_out):
    assert w_in.shape[0] == 1, "one layer"
    B, T, _ = x_prompt.shape
    DB, TS, _ = x_sample.shape
    H, N = A_HEADS, A_HEAD_DIM
    kvs = (2, B_KV_HEADS, B_HEAD_DIM)
    p = dict(mu_shift=mu_shift[0], w0=w0[0], w_up=w_up[0], a0=a0[0], a_up=a_up[0], k_k=k_k[0], k_a=k_a[0],
             r_k=r_k[0].reshape(-1), gn_w=gn_w[0], gn_b=gn_b[0])
    w_packed = _pack_w_in(w_in[0])
    wpa, wpb, wo = w_pa[0].astype(bf16), w_pb[0].astype(bf16), w_o[0].astype(bf16)

    xp = x_prompt.reshape(B * T, D_MODEL)
    up = _proj_in(xp, norm_in[0], w_packed)
    up3 = up.reshape(B, T, U_COLS)
    oa_p, s_p = _rwkv_wide(up3, jnp.zeros((B, 1, 4 * A_WIDTH), f32), jnp.zeros((B, 1, 128), f32),
                      jnp.zeros((B, H, N, N), f32), p, chunk=64, t_valid=64, rows=math.gcd(B, 4))
    tt = 512
    pw0, pw1 = _pos_weight_tiles(cmp_pos_k[0], cmp_pos_v[0], tt)
    qt, ks, vst, kw, vwt, pa, pb = _nsa_prep(up3, pw0, pw1, tt)
    ob_p = _nsa_prompt(up3, qt, ks, vst, kw, vwt, pa, pb, w_cmp_k[0], w_cmp_v[0], 128)
    y_p = _merge(xp, oa_p.reshape(B * T, A_WIDTH), ob_p.reshape(B * T, B_WIDTH), up, wpa, wpb, wo, norm_out)

    t_pad = 8
    xs = x_sample.reshape(DB * TS, D_MODEL)
    us = _proj_in(xs, norm_in[0], w_packed)
    us3 = us.reshape(DB, TS, U_COLS)
    us3p = jnp.pad(us3, ((0, 0), (0, t_pad - TS), (0, 0)))
    shift0 = state_shift[0]
    oa_s, s_s = _rwkv_wide(us3p, shift0[:, None, :4 * A_WIDTH], shift0[:, None, 4 * A_WIDTH:], state_rwkv[0], p,
                      chunk=t_pad, t_valid=TS, rows=2 if DB % 2 == 0 else 1)
    n_pool = cache_cmp_kv.shape[1]
    win = cache_win_kv[0]
    rows_last = lambda c, lead: jnp.transpose(c, (0, 2, 3, 4, 1)).reshape(lead, 2 * B_KV_WIDTH, c.shape[1])
    ob_s = _nsa_sample(us3p, rows_last(cache_cmp_kv[0], n_pool), rows_last(cache_sel_kv[0], n_pool),
                       rows_last(win, DB), page_table, pw0[:PAGE_SIZE].T, pw1[:PAGE_SIZE].T, w_cmp_k[0], w_cmp_v[0],
                       t_new=TS, pages_per_step=min(32, page_table.shape[1]))
    y_s = _merge(xs, oa_s[:, :TS].reshape(DB * TS, A_WIDTH), ob_s[:, :TS].reshape(DB * TS, B_WIDTH), us, wpa, wpb, wo,
                 norm_out)

    def kv_out(u3_, col, lead, t):
        return u3_[..., col:col + 2 * B_KV_WIDTH].reshape((1, lead, t) + kvs)

    wk = min(WINDOW, T)
    new_w_s = kv_out(us3, C_KVW, DB, TS)[0]
    s_win = jnp.concatenate([win, new_w_s], axis=1)[:, TS:][None]
    return (y_p.reshape(B, T, D_MODEL), y_s.reshape(DB, TS, D_MODEL),
            kv_out(up3, C_KVC, B, T), kv_out(up3, C_KVS, B, T), kv_out(up3[:, T - wk:], C_KVW, B, wk),
            s_p[None], _a_cols(up3[:, T - 1])[None],
            kv_out(us3, C_KVC, DB, TS), kv_out(us3, C_KVS, DB, TS), s_win,
            s_s[None], _a_cols(us3[:, TS - 1])[None])
```

```python
import functools
import math

import jax
import jax.numpy as jnp
from jax import lax
from jax.experimental import pallas as pl
from jax.experimental.pallas import tpu as pltpu

f32 = jnp.float32
bf16 = jnp.bfloat16

D_MODEL = 2048
PAGE_SIZE = 128
A_HEADS = 16
A_HEAD_DIM = 64
A_WIDTH = A_HEADS * A_HEAD_DIM
DECAY_RANK = 64
ICLR_RANK = 64
A_COLS = 4 * A_WIDTH + DECAY_RANK + ICLR_RANK
GN_EPS = 64e-5
B_HEADS = 16
B_KV_HEADS = 4
B_GROUP = B_HEADS // B_KV_HEADS
B_HEAD_DIM = 64
B_WIDTH = B_HEADS * B_HEAD_DIM
B_KV_WIDTH = B_KV_HEADS * B_HEAD_DIM
B_COLS = 2 * B_WIDTH + 6 * B_KV_WIDTH + 3 * B_HEADS
CMP_BLOCK = 32
CMP_STRIDE = 16
SEL_BLOCK = 64
N_SELECT = 16
N_LOCAL = 2
WINDOW = 512
FORCED_SCORE = 1e4
ATTN_SCALE = B_HEAD_DIM ** -0.5
RMS_EPS = 1e-6
NEG = -1e30
LOG2E = math.log2(math.e)

C_RKVG = 0
C_Q = 4096
C_GB = 5120
C_GA_M = 6144
C_GB_M = 8192
C_KVC = 10240
C_KVS = 10752
C_KVW = 11264
C_LR = 11776
C_NG = 11904
U_COLS = 12288

VMEM_LIMIT = 56 * 1024 * 1024
HI = lax.Precision.HIGHEST


def _cparams(sem):
    return pltpu.CompilerParams(dimension_semantics=sem, vmem_limit_bytes=VMEM_LIMIT)


def _pack_w_in(w):
    wt = w.T
    a, b, m = wt[:A_COLS], wt[A_COLS:A_COLS + B_COLS], wt[A_COLS + B_COLS:]
    z = jnp.zeros((U_COLS - C_NG - 3 * B_HEADS, w.shape[0]), w.dtype)
    return jnp.concatenate(
        [a[:4 * A_WIDTH], b[:2 * B_WIDTH], m, b[2 * B_WIDTH:2 * B_WIDTH + 6 * B_KV_WIDTH],
         a[4 * A_WIDTH:], b[2 * B_WIDTH + 6 * B_KV_WIDTH:], z], axis=0).astype(bf16)


def _proj_in_kernel(x_ref, g_ref, w_ref, o_ref, xn_ref):
    @pl.when(pl.program_id(1) == 0)
    def _():
        x = x_ref[...]
        ms = jnp.mean(x * x, axis=-1, keepdims=True)
        xn_ref[...] = (x * lax.rsqrt(ms + RMS_EPS) * g_ref[...]).astype(bf16)

    o_ref[...] = lax.dot_general(xn_ref[...], w_ref[...], (((1,), (1,)), ((), ())), preferred_element_type=f32)


def _proj_in(x2d, norm_g, w_packed_t):
    m = x2d.shape[0]
    tm = min(1024, m)
    tn = 1024
    return pl.pallas_call(
        _proj_in_kernel,
        out_shape=jax.ShapeDtypeStruct((m, U_COLS), f32),
        grid=(m // tm, U_COLS // tn),
        in_specs=[pl.BlockSpec((tm, D_MODEL), lambda i, j: (i, 0)),
                  pl.BlockSpec((1, D_MODEL), lambda i, j: (0, 0)),
                  pl.BlockSpec((tn, D_MODEL), lambda i, j: (j, 0))],
        out_specs=pl.BlockSpec((tm, tn), lambda i, j: (i, j)),
        scratch_shapes=[pltpu.VMEM((tm, D_MODEL), bf16)],
        compiler_params=_cparams(("parallel", "arbitrary")),
        name="proj_in",
    )(x2d, norm_g.reshape(1, D_MODEL), w_packed_t)


def _mm(a, b):
    return jnp.dot(a.astype(bf16), b.astype(bf16), preferred_element_type=f32)


def _mm_nt(a, b):
    return lax.dot_general(a.astype(bf16), b.astype(bf16), (((1,), (1,)), ((), ())), preferred_element_type=f32)


def _mm_tn(a, b):
    return lax.dot_general(a.astype(bf16), b.astype(bf16), (((0,), (0,)), ((), ())), preferred_element_type=f32)


HEADS_PER_GROUP = 4
N_HEAD_GROUPS = A_HEADS // HEADS_PER_GROUP
GROUP_WIDTH = HEADS_PER_GROUP * A_HEAD_DIM


def _rwkv_wide_kernel(um_ref, ulr_ref, pm_ref, plr_ref, s0_ref, mum_ref, mulr_ref, w0_ref, wup_ref, a0_ref,
                      aup_ref, kk_ref, ka_ref, rk_ref, gnw_ref, gnb_ref, o_ref, sout_ref,
                      s_ref, prevm_ref, prevlr_ref, *, chunk, t_valid):
    C = chunk
    N, HG, NG, GW = A_HEAD_DIM, HEADS_PER_GROUP, N_HEAD_GROUPS, GROUP_WIDTH
    RB = um_ref.shape[0]
    c = pl.program_id(1)

    @pl.when(c == 0)
    def _():
        for bi in range(RB):
            for q in range(NG):
                s_ref[bi, q] = jnp.concatenate([s0_ref[bi, q * HG + h] for h in range(HG)], axis=1)
        prevm_ref[...] = pm_ref[...]
        prevlr_ref[...] = plr_ref[...]

    row = lax.broadcasted_iota(jnp.int32, (C, 1), 0)
    tril_b = (lax.broadcasted_iota(jnp.int32, (C, C), 0) >= lax.broadcasted_iota(jnp.int32, (C, C), 1)).astype(bf16)
    t_i = lax.broadcasted_iota(jnp.int32, (C, HG * C), 0)
    s_i = lax.broadcasted_iota(jnp.int32, (C, HG * C), 1) % C
    tril_incl = t_i >= s_i
    tril_strict = t_i > s_i
    eye = (t_i == s_i).astype(f32)
    head_of_dim = lax.broadcasted_iota(jnp.int32, (1, GW), 1) // N
    head_of_tok = lax.broadcasted_iota(jnp.int32, (1, HG * C), 1) // C
    ones_bd = (lax.broadcasted_iota(jnp.int32, (GW, GW), 0) // N
               == lax.broadcasted_iota(jnp.int32, (GW, GW), 1) // N).astype(bf16)
    n_sq = int(math.log2(C)) - 1
    valid = (row < t_valid).astype(f32) if t_valid < C else None
    gcols = [slice(q * GW, (q + 1) * GW) for q in range(NG)]
    qs = range(NG)

    def shifted(u, prev):
        return jnp.where(row == 0, prev, pltpu.roll(u, 1, 0))

    def head_sums(*xs):
        stacked = jnp.concatenate([x.astype(bf16)[:, gc] for x in xs for gc in gcols], axis=0)
        sums = jnp.dot(stacked, ones_bd, preferred_element_type=f32)
        return [jnp.concatenate([sums[(i * NG + q) * C:(i * NG + q + 1) * C] for q in qs], axis=1)
                for i in range(len(xs))]

    def block_rows(x, head_of_lane):
        return jnp.concatenate([jnp.where(head_of_lane == h, x, 0.0) for h in range(HG)], axis=0).astype(bf16)

    def mm(a, b):
        return jnp.dot(a.astype(bf16), b, preferred_element_type=f32)

    def mm_nt(a, b):
        return lax.dot_general(a.astype(bf16), b, (((1,), (1,)), ((), ())), preferred_element_type=f32)

    def prepare(bi, out):
        um = um_ref[bi]
        ulr = ulr_ref[bi]
        prev_m = prevm_ref[bi]
        prev_lr = prevlr_ref[bi]
        prevm_ref[bi] = um[C - 1:C, :]
        prevlr_ref[bi] = ulr[C - 1:C, :]
        uslr = ulr + mulr_ref[...] * (shifted(ulr, prev_lr) - ulr)
        zw = w0_ref[...] + _mm(jnp.tanh(uslr[:, :DECAY_RANK]), wup_ref[...])
        a = jax.nn.sigmoid(a0_ref[...] + _mm(uslr[:, DECAY_RANK:], aup_ref[...]))
        yield
        secs = []
        for i in range(4):
            cols = slice(i * A_WIDTH, (i + 1) * A_WIDTH)
            secs.append(um[:, cols] + mum_ref[:, cols] * (shifted(um[:, cols], prev_m[:, cols]) - um[:, cols]))
            yield
        r, k, v, g = secs
        logw = -math.exp(-0.5) * jax.nn.sigmoid(zw)
        if valid is not None:
            logw = logw * valid
        yield
        logw_hi = logw.astype(bf16)
        logw_lo = (logw - logw_hi.astype(f32)).astype(bf16)
        cum = (jnp.dot(tril_b, logw_hi, preferred_element_type=f32)
               + jnp.dot(tril_b, logw_lo, preferred_element_type=f32))
        yield
        e_pos = jnp.exp(cum)
        e_neg = jnp.exp(-cum)
        yield
        e_prev = jnp.exp(cum - logw)
        kk = k * kk_ref[...]
        yield
        k2 = k * (1.0 + (a - 1.0) * ka_ref[...])
        yield
        ssq, bonus = head_sums(kk * kk, r * k2 * rk_ref[...])
        yield
        kk = kk * lax.rsqrt(jnp.maximum(ssq, 1e-24))
        kn = k2 * e_neg
        bn = kk * a * e_neg
        if valid is not None:
            kn = kn * valid
            bn = bn * valid
        yield
        lhs = jnp.concatenate([kk * e_prev, r * e_pos], axis=0).astype(bf16)
        out.update(v=v, g=g, bonus=bonus, kn=kn, bn=bn, lhs=lhs, e_last=e_pos[C - 1:C, :])
        yield

    def solve(bi, pre, out):
        v = [pre["v"][:, gc] for gc in gcols]
        kn = [pre["kn"][:, gc] for gc in gcols]
        bn = [pre["bn"][:, gc] for gc in gcols]
        lhs = [pre["lhs"][:, gc] for gc in gcols]
        s0 = [s_ref[bi, q] for q in qs]
        qk_b = [mm_nt(lhs[q], block_rows(bn[q], head_of_dim)) for q in qs]
        yield
        qk_k = [mm_nt(lhs[q], block_rows(kn[q], head_of_dim)) for q in qs]
        yield
        w0s = [mm_nt(lhs[q], block_rows(s0[q], head_of_dim)) for q in qs]
        yield
        pw = [jnp.where(tril_strict, -qk_b[q][:C], 0.0) for q in qs]
        tinv = [eye + pw[q] for q in qs]
        bd = [block_rows(pw[q], head_of_tok) for q in qs]
        for _ in range(n_sq):
            pw = [mm(pw[q], bd[q]) for q in qs]
            yield
            bd = [block_rows(pw[q], head_of_tok) for q in qs]
            tinv = [tinv[q] + mm(tinv[q], bd[q]) for q in qs]
            yield
        bd_v = [block_rows(v[q], head_of_dim) for q in qs]
        rhs_u = [w0s[q][:C] + mm(jnp.where(tril_strict, qk_k[q][:C], 0.0), bd_v[q]) for q in qs]
        yield
        u = [mm(tinv[q], block_rows(rhs_u[q], head_of_dim)) for q in qs]
        yield
        p_cat = [jnp.concatenate([jnp.where(tril_incl, qk_k[q][C:], 0.0), jnp.where(tril_incl, -qk_b[q][C:], 0.0)],
                                 axis=1) for q in qs]
        out["o"] = [w0s[q][C:] + mm(p_cat[q], jnp.concatenate([bd_v[q], block_rows(u[q], head_of_dim)], axis=0))
                    for q in qs]
        yield
        full = [lax.dot_general(jnp.concatenate([v[q], -u[q]], axis=0).astype(bf16),
                                jnp.concatenate([kn[q], bn[q]], axis=0).astype(bf16),
                                (((0,), (0,)), ((), ())), preferred_element_type=f32) for q in qs]
        yield
        for q in qs:
            upd = jnp.where(head_of_dim == 0, full[q][0:N], 0.0)
            for h in range(1, HG):
                upd = upd + jnp.where(head_of_dim == h, full[q][h * N:(h + 1) * N], 0.0)
            s_ref[bi, q] = (s0[q] + upd) * pre["e_last"][:, gcols[q]]
        yield

    def finish(bi, pre, mid):
        o = jnp.concatenate(mid["o"], axis=1)
        mean = head_sums(o)[0] * (1.0 / N)
        yield
        d = o - mean
        var = head_sums(d * d)[0] * (1.0 / N)
        yield
        g = pre["g"]
        on = d * lax.rsqrt(var + GN_EPS) * gnw_ref[...] + gnb_ref[...] + pre["bonus"] * pre["v"]
        o_ref[bi] = on * (g * jax.nn.sigmoid(g))
        yield

    def interleave(gens):
        gens = list(gens)
        while gens:
            for gen in list(gens):
                if next(gen, "done") == "done":
                    gens.remove(gen)

    pre = [{} for _ in range(RB)]
    mid = [{} for _ in range(RB)]
    halves = [range(0, RB // 2), range(RB // 2, RB)] if RB > 1 else [range(RB)]
    prep = lambda rows_: [prepare(bi, pre[bi]) for bi in rows_]
    solv = lambda rows_: [solve(bi, pre[bi], mid[bi]) for bi in rows_]
    fin = lambda rows_: [finish(bi, pre[bi], mid[bi]) for bi in rows_]
    interleave(prep(halves[0]))
    for i, rows_ in enumerate(halves):
        interleave(solv(rows_) + (prep(halves[i + 1]) if i + 1 < len(halves) else [])
                   + (fin(halves[i - 1]) if i >= 1 else []))
    interleave(fin(halves[-1]))

    @pl.when(c == pl.num_programs(1) - 1)
    def _():
        for bi in range(RB):
            for q in range(NG):
                s_q = s_ref[bi, q]
                for h in range(HG):
                    sout_ref[bi, q * HG + h] = s_q[:, h * N:(h + 1) * N]


def _rwkv_wide(u3, prev_m, prev_lr, s0, p, *, chunk, t_valid, rows):
    B, T, _ = u3.shape
    C = chunk
    NG, N, GW = N_HEAD_GROUPS, A_HEAD_DIM, GROUP_WIDTH
    assert B % rows == 0 and T % C == 0
    row = lambda v: v.reshape(1, -1)
    vec = lambda n: pl.BlockSpec((1, n), lambda b, c: (0, 0))
    in_specs = [
        pl.BlockSpec((rows, C, 4 * A_WIDTH), lambda b, c: (b, c, 0)),
        pl.BlockSpec((rows, C, 128), lambda b, c: (b, c, C_LR // 128)),
        pl.BlockSpec((rows, 1, 4 * A_WIDTH), lambda b, c: (b, 0, 0)),
        pl.BlockSpec((rows, 1, 128), lambda b, c: (b, 0, 0)),
        pl.BlockSpec((rows, A_HEADS, N, N), lambda b, c: (b, 0, 0, 0)),
        vec(4 * A_WIDTH), vec(128), vec(A_WIDTH),
        pl.BlockSpec((DECAY_RANK, A_WIDTH), lambda b, c: (0, 0)),
        vec(A_WIDTH),
        pl.BlockSpec((ICLR_RANK, A_WIDTH), lambda b, c: (0, 0)),
        vec(A_WIDTH), vec(A_WIDTH), vec(A_WIDTH), vec(A_WIDTH), vec(A_WIDTH),
    ]
    o_a, s_out = pl.pallas_call(
        functools.partial(_rwkv_wide_kernel, chunk=C, t_valid=t_valid),
        out_shape=(jax.ShapeDtypeStruct((B, T, A_WIDTH), f32), jax.ShapeDtypeStruct((B, A_HEADS, N, N), f32)),
        grid=(B // rows, T // C),
        in_specs=in_specs,
        out_specs=(pl.BlockSpec((rows, C, A_WIDTH), lambda b, c: (b, c, 0)),
                   pl.BlockSpec((rows, A_HEADS, N, N), lambda b, c: (b, 0, 0, 0))),
        scratch_shapes=[pltpu.VMEM((rows, NG, N, GW), f32), pltpu.VMEM((rows, 1, 4 * A_WIDTH), f32),
                        pltpu.VMEM((rows, 1, 128), f32)],
        compiler_params=_cparams(("parallel", "arbitrary")),
        name="rwkv7",
    )(u3, u3, prev_m, prev_lr, s0, row(p["mu_shift"][:4 * A_WIDTH]), row(p["mu_shift"][4 * A_WIDTH:]),
      row(p["w0"]), p["w_up"], row(p["a0"]), p["a_up"], row(p["k_k"]), row(p["k_a"]), row(p["r_k"]),
      row(p["gn_w"]), row(p["gn_b"]))
    return o_a, s_out


def _merge_kernel(x_ref, oa_ref, ob_ref, ga_ref, gb_ref, wpa_ref, wpb_ref, wo_ref, gout_ref, y_ref):
    pa = jnp.dot(oa_ref[...].astype(bf16), wpa_ref[...], preferred_element_type=f32)
    pb = jnp.dot(ob_ref[...].astype(bf16), wpb_ref[...], preferred_element_type=f32)
    merged = jax.nn.sigmoid(ga_ref[...]) * pa + jax.nn.sigmoid(gb_ref[...]) * pb
    h = x_ref[...] + jnp.dot(merged.astype(bf16), wo_ref[...], preferred_element_type=f32)
    ms = jnp.mean(h * h, axis=-1, keepdims=True)
    y_ref[...] = h * lax.rsqrt(ms + RMS_EPS) * gout_ref[...]


def _merge(x2d, o_a, o_b, u2d, w_pa, w_pb, w_o, norm_out):
    m = x2d.shape[0]
    tm = min(256, m)
    const = lambda shape: pl.BlockSpec(shape, lambda i: (0, 0), pipeline_mode=pl.Buffered(1))
    return pl.pallas_call(
        _merge_kernel,
        out_shape=jax.ShapeDtypeStruct((m, D_MODEL), f32),
        grid=(m // tm,),
        in_specs=[pl.BlockSpec((tm, D_MODEL), lambda i: (i, 0)),
                  pl.BlockSpec((tm, A_WIDTH), lambda i: (i, 0)),
                  pl.BlockSpec((tm, B_WIDTH), lambda i: (i, 0)),
                  pl.BlockSpec((tm, D_MODEL), lambda i: (i, C_GA_M // D_MODEL)),
                  pl.BlockSpec((tm, D_MODEL), lambda i: (i, C_GB_M // D_MODEL)),
                  const((A_WIDTH, D_MODEL)), const((B_WIDTH, D_MODEL)), const((D_MODEL, D_MODEL)),
                  const((1, D_MODEL))],
        out_specs=pl.BlockSpec((tm, D_MODEL), lambda i: (i, 0)),
        compiler_params=_cparams(("parallel",)),
        name="merge_out",
    )(x2d, o_a, o_b, u2d, u2d, w_pa, w_pb, w_o, norm_out.reshape(1, D_MODEL))


def _alibi_slope(head):
    return 2.0 ** (-8.0 * (head + 1) / B_HEADS)


def _slope_row(g, tq):
    lane_head = lax.broadcasted_iota(jnp.int32, (1, B_GROUP * tq), 1) // tq
    out = jnp.zeros((1, B_GROUP * tq), f32)
    for r in range(B_GROUP):
        out = jnp.where(lane_head == r, _alibi_slope(g * B_GROUP + r), out)
    return out


def _tile_heads(row):
    return jnp.concatenate([row] * B_GROUP, axis=1)


def _overlap_t(n_sel_rows, n_ch):
    j = lax.broadcasted_iota(jnp.int32, (n_sel_rows, n_ch), 0) * SEL_BLOCK
    n = lax.broadcasted_iota(jnp.int32, (n_sel_rows, n_ch), 1) * CMP_STRIDE
    ov = jnp.minimum(n + CMP_BLOCK, j + SEL_BLOCK) - jnp.maximum(n, j)
    return jnp.maximum(ov, 0).astype(f32) * (1.0 / CMP_BLOCK)


M_INIT = 0.1 * NEG


def _attn_step_multi(s, carries, v_t):
    n = range(len(s))
    m_new = [jnp.maximum(carries[i][0], jnp.max(s[i], axis=0, keepdims=True)) for i in n]
    alpha = [jnp.exp2(carries[i][0] - m_new[i]) for i in n]
    p = [jnp.exp2(s[i] - m_new[i]) for i in n]
    l = [carries[i][1] * alpha[i] + jnp.sum(p[i], axis=0, keepdims=True) for i in n]
    pv = [jnp.dot(v_t[i], p[i].astype(bf16), preferred_element_type=f32) for i in n]
    return [(m_new[i], l[i], carries[i][2] * alpha[i] + pv[i]) for i in n]


def _attn_init(lanes):
    return (jnp.full((1, lanes), M_INIT, f32), jnp.zeros((1, lanes), f32), jnp.zeros((B_HEAD_DIM, lanes), f32))


def _split3(x):
    hi = x.astype(bf16)
    r1 = x - hi.astype(f32)
    mid = r1.astype(bf16)
    lo = (r1 - mid.astype(f32)).astype(bf16)
    return hi, mid, lo


POS_SPLIT = 128
K_AUG = 16


def _key_aug(pos_i):
    one = jnp.ones(pos_i.shape, f32)
    hi = (pos_i // POS_SPLIT).astype(f32)
    lo = (pos_i % POS_SPLIT).astype(f32)
    zero = jnp.zeros((pos_i.shape[0], K_AUG - 9), f32)
    return jnp.concatenate([one, one, one, hi, hi, hi, lo, lo, lo, zero], axis=1).astype(bf16)


def _query_aug(slope_row, t_row):
    rows = _split3(-slope_row * t_row) + _split3(slope_row * float(POS_SPLIT)) + _split3(slope_row)
    zero = jnp.zeros((K_AUG - 9, slope_row.shape[1]), bf16)
    return jnp.concatenate(list(rows) + [zero], axis=0)


def _attn_finish(carry):
    _, l, acc = carry
    return acc * (1.0 / jnp.maximum(l, 1e-30))


def _compressed_multi(kc, vc_t, q_t, t_row, slope_rows, n_ch):
    n = range(len(kc))
    blk_end = (lax.broadcasted_iota(jnp.int32, (n_ch, 1), 0) * CMP_STRIDE + (CMP_BLOCK - 1)).astype(f32)
    dist = t_row - blk_end
    mask = dist >= 0.0
    s = [jnp.dot(kc[i], q_t[i], preferred_element_type=f32) for i in n]
    s = [jnp.where(mask, s[i] - slope_rows[i] * dist, NEG) for i in n]
    m = [jnp.max(s[i], axis=0, keepdims=True) for i in n]
    p = [jnp.where(mask, jnp.exp2(s[i] - m[i]), 0.0) for i in n]
    inv = [1.0 / jnp.maximum(jnp.sum(p[i], axis=0, keepdims=True), 1e-30) for i in n]
    p = [p[i] * inv[i] for i in n]
    o = [jnp.dot(vc_t[i], p[i].astype(bf16), preferred_element_type=f32) for i in n]
    return o, p


def _selection_scores(p, tq, tok_row_i, n_rows, n_sel, n_ch):
    psum = p[:, 0:tq]
    for r in range(1, B_GROUP):
        psum = psum + p[:, r * tq:(r + 1) * tq]
    imp = jnp.dot(_overlap_t(n_rows, n_ch), psum, preferred_element_type=f32, precision=HI)
    j = lax.broadcasted_iota(jnp.int32, (n_rows, 1), 0)
    back = tok_row_i // SEL_BLOCK - j
    forced = (j == 0) | ((back >= 0) & (back < N_LOCAL))
    score = jnp.where(forced, FORCED_SCORE, jnp.where(back >= 0, imp, -1.0))
    return jnp.where(j < n_sel, score, -2.0)


def _kv_pool_partials(x, pw0, pw1):
    rows = x.shape[0]
    n = rows // CMP_STRIDE
    pool = (lax.broadcasted_iota(jnp.int32, (n, rows), 1) // CMP_STRIDE
            == lax.broadcasted_iota(jnp.int32, (n, rows), 0)).astype(f32)
    a = jnp.dot(pool, x * pw0, preferred_element_type=f32, precision=HI)
    b = jnp.dot(pool, x * pw1, preferred_element_type=f32, precision=HI)
    return a, b


def _compress_kv(pooled, wck, wcv, kc_ref, vct_ref):
    for g in range(B_KV_HEADS):
        kc_ref[g] = _mm(pooled[:, g * 64:(g + 1) * 64], wck).astype(bf16)
        vc = _mm(pooled[:, B_KV_WIDTH + g * 64:B_KV_WIDTH + (g + 1) * 64], wcv)
        vct_ref[g] = vc.T.astype(bf16)


def _nsa_prep_kernel(q_ref, kvc_ref, kvs_ref, kvw_ref, pw0_ref, pw1_ref,
                     qt_ref, ks_ref, vst_ref, kw_ref, vwt_ref, a_ref, b_ref):
    qt_ref[...] = (q_ref[...] * (ATTN_SCALE * LOG2E)).T.astype(bf16)
    tt = q_ref.shape[0]
    aug = _key_aug(pl.program_id(1) * tt + lax.broadcasted_iota(jnp.int32, (tt, 1), 0))
    for src, k_ref, vt_ref in ((kvs_ref, ks_ref, vst_ref), (kvw_ref, kw_ref, vwt_ref)):
        x = src[...]
        for g in range(B_KV_HEADS):
            k_ref[g] = jnp.concatenate([x[:, g * 64:(g + 1) * 64].astype(bf16), aug], axis=1)
        vt_ref[...] = x[:, B_KV_WIDTH:].T.astype(bf16)
    a, b = _kv_pool_partials(kvc_ref[...], pw0_ref[...], pw1_ref[...])
    a_ref[...] = a
    b_ref[...] = b


def _nsa_prep(u3, pw0, pw1, tt):
    B, T, _ = u3.shape
    n_ch = tt // CMP_STRIDE
    ublk = lambda w, off: pl.BlockSpec((None, tt, w), lambda b, t: (b, t, off // w))
    const = pl.BlockSpec((tt, 2 * B_KV_WIDTH), lambda b, t: (0, 0))
    return pl.pallas_call(
        _nsa_prep_kernel,
        out_shape=(jax.ShapeDtypeStruct((B, B_WIDTH, T), bf16),
                   jax.ShapeDtypeStruct((B, B_KV_HEADS, T, B_HEAD_DIM + K_AUG), bf16),
                   jax.ShapeDtypeStruct((B, B_KV_WIDTH, T), bf16),
                   jax.ShapeDtypeStruct((B, B_KV_HEADS, T, B_HEAD_DIM + K_AUG), bf16),
                   jax.ShapeDtypeStruct((B, B_KV_WIDTH, T), bf16),
                   jax.ShapeDtypeStruct((B, T // CMP_STRIDE, 2 * B_KV_WIDTH), f32),
                   jax.ShapeDtypeStruct((B, T // CMP_STRIDE, 2 * B_KV_WIDTH), f32)),
        grid=(B, T // tt),
        in_specs=[ublk(B_WIDTH, C_Q), ublk(2 * B_KV_WIDTH, C_KVC), ublk(2 * B_KV_WIDTH, C_KVS),
                  ublk(2 * B_KV_WIDTH, C_KVW), const, const],
        out_specs=(pl.BlockSpec((None, B_WIDTH, tt), lambda b, t: (b, 0, t)),
                   pl.BlockSpec((None, B_KV_HEADS, tt, B_HEAD_DIM + K_AUG), lambda b, t: (b, 0, t, 0)),
                   pl.BlockSpec((None, B_KV_WIDTH, tt), lambda b, t: (b, 0, t)),
                   pl.BlockSpec((None, B_KV_HEADS, tt, B_HEAD_DIM + K_AUG), lambda b, t: (b, 0, t, 0)),
                   pl.BlockSpec((None, B_KV_WIDTH, tt), lambda b, t: (b, 0, t)),
                   pl.BlockSpec((None, n_ch, 2 * B_KV_WIDTH), lambda b, t: (b, t, 0)),
                   pl.BlockSpec((None, n_ch, 2 * B_KV_WIDTH), lambda b, t: (b, t, 0))),
        compiler_params=_cparams(("parallel", "parallel")),
        name="nsa_prep",
    )(u3, u3, u3, u3, pw0, pw1)


def _rank_select(score, n_sel):
    sub = 8
    assert n_sel % sub == 0
    tiles = [score[r * sub:(r + 1) * sub] for r in range(n_sel // sub)]
    ranks = [jnp.zeros(t.shape, f32) for t in tiles]
    j_in_tile = lax.broadcasted_iota(jnp.int32, (sub, 1), 0)
    for i in range(n_sel):
        row = tiles[i // sub][i % sub:i % sub + 1]
        for r, t in enumerate(tiles):
            if r * sub > i:
                beats = jnp.where(row >= t, 1.0, 0.0)
            elif r * sub + sub - 1 <= i:
                beats = jnp.where(row > t, 1.0, 0.0)
            else:
                beats = jnp.where(row > t, 1.0, jnp.where(row == t, jnp.where(j_in_tile + r * sub > i, 1.0, 0.0), 0.0))
            ranks[r] = ranks[r] + beats
    rank = jnp.concatenate(ranks, axis=0)
    return jnp.where(rank < float(min(N_SELECT, n_sel)), jnp.where(score >= 0.0, 1.0, 0.0), 0.0)


def _nsa_prompt_kernel(qt_ref, ks_ref, vst_ref, kw_ref, vwt_ref, a_ref, b_ref, wck_ref, wcv_ref, gates_ref, gb_ref,
                       o_ref, kc_ref, vct_ref, keep_ref, cmp_ref, qaug_ref, m_ref, l_ref, acc_ref, ot_ref,
                       *, seq, tq):
    i = pl.program_id(1)
    n_ch = seq // CMP_STRIDE
    n_sel = seq // SEL_BLOCK
    lanes = B_GROUP * tq
    tk = 2 * SEL_BLOCK

    @pl.when(i == 0)
    def _():
        pooled = a_ref[...] + pltpu.roll(b_ref[...], n_ch - 1, 0)
        _compress_kv(pooled, wck_ref[...], wcv_ref[...], kc_ref, vct_ref)

    G = B_KV_HEADS
    gs = range(G)
    tok_i = i * tq + lax.broadcasted_iota(jnp.int32, (1, tq), 1)
    t_row = _tile_heads(tok_i.astype(f32))
    key_off = lax.broadcasted_iota(jnp.int32, (tk, 1), 0).astype(f32)
    slopes = [_slope_row(g, tq) * LOG2E for g in gs]

    def q_t(g):
        return jnp.concatenate([qt_ref[(g * B_GROUP + r) * 64:(g * B_GROUP + r + 1) * 64, :] for r in range(B_GROUP)],
                               axis=1)

    o_cmp, prob = _compressed_multi([kc_ref[g] for g in gs], [vct_ref[g] for g in gs], [q_t(g) for g in gs],
                                    t_row, slopes, n_ch)
    for g in gs:
        cmp_ref[g] = o_cmp[g]
    score = jnp.concatenate([_selection_scores(prob[g], tq, tok_i, n_sel, n_sel, n_ch) for g in gs], axis=1)
    keep = _rank_select(score, n_sel)
    for j in range(n_sel):
        keep_ref[j] = (keep[j:j + 1, :] - 1.0) * (-NEG)

    for g in gs:
        qaug_ref[g] = jnp.concatenate([q_t(g), _query_aug(slopes[g], t_row)], axis=0)
    for c in range(2 * G):
        m, l, acc = _attn_init(lanes)
        m_ref[c] = m
        l_ref[c] = l
        acc_ref[c] = acc

    def step(kt, with_window, masked):
        off = pl.multiple_of(kt * tk, tk)
        row0 = keep_ref[2 * kt]
        row1 = keep_ref[2 * kt + 1]
        if masked:
            dist = t_row - (key_off + (kt * tk).astype(f32))
            causal = jnp.where(dist >= 0.0, 0.0, NEG)
            band = jnp.where(dist <= float(WINDOW), causal, NEG)
        s, v, chains = [], [], []
        for g in gs:
            sg = jnp.dot(ks_ref[g, pl.ds(off, tk), :], qaug_ref[g], preferred_element_type=f32)
            if masked:
                sg = sg + causal
            s.append(jnp.concatenate([sg[:SEL_BLOCK] + _tile_heads(row0[:, g * tq:(g + 1) * tq]),
                                      sg[SEL_BLOCK:] + _tile_heads(row1[:, g * tq:(g + 1) * tq])], axis=0))
            v.append(vst_ref[g * 64:(g + 1) * 64, pl.ds(off, tk)])
            chains.append(g)
        if with_window:
            for g in gs:
                sg = jnp.dot(kw_ref[g, pl.ds(off, tk), :], qaug_ref[g], preferred_element_type=f32)
                s.append(sg + band if masked else sg)
                v.append(vwt_ref[g * 64:(g + 1) * 64, pl.ds(off, tk)])
                chains.append(G + g)
        out = _attn_step_multi(s, [(m_ref[c], l_ref[c], acc_ref[c]) for c in chains], v)
        for c, (m, l, acc) in zip(chains, out):
            m_ref[c] = m
            l_ref[c] = l
            acc_ref[c] = acc

    assert tq == tk and WINDOW % tk == 0
    last = i
    lo = jnp.maximum(i - WINDOW // tk, 0)

    def plain_loop(with_window):
        def body(kt, carry):
            step(kt, with_window, False)
            return carry
        return body

    lax.fori_loop(0, lo, plain_loop(False), 0)
    step(lo, True, True)
    lax.fori_loop(lo + 1, last, plain_loop(True), 0)

    @pl.when(last > lo)
    def _():
        step(last, True, True)

    gates_t = jax.nn.sigmoid(gates_ref[...]).T

    def gate_row(branch, g):
        base = branch * B_HEADS + g * B_GROUP
        return jnp.concatenate([gates_t[base + r:base + r + 1, :] for r in range(B_GROUP)], axis=1)

    o_sel = [_attn_finish((m_ref[g], l_ref[g], acc_ref[g])) for g in gs]
    o_win = [_attn_finish((m_ref[G + g], l_ref[G + g], acc_ref[G + g])) for g in gs]
    o_t = [gate_row(0, g) * cmp_ref[g] + gate_row(1, g) * o_sel[g] + gate_row(2, g) * o_win[g] for g in gs]
    for g in gs:
        for r in range(B_GROUP):
            ot_ref[(g * B_GROUP + r) * 64:(g * B_GROUP + r + 1) * 64, :] = o_t[g][:, r * tq:(r + 1) * tq]

    gb = gb_ref[...]
    o_ref[...] = ot_ref[...].T * (gb * jax.nn.sigmoid(gb))


def _nsa_prompt(u3, qt, ks, vst, kw, vwt, a, b, wck, wcv, tq):
    B, T, _ = u3.shape
    n_ch = T // CMP_STRIDE
    n_sel = T // SEL_BLOCK
    lanes = B_GROUP * tq
    per_b3 = lambda s1, s2: pl.BlockSpec((None, s1, s2), lambda b, i: (b, 0, 0))
    per_b4 = pl.BlockSpec((None, B_KV_HEADS, T, B_HEAD_DIM + K_AUG), lambda b, i: (b, 0, 0, 0))
    w_spec = pl.BlockSpec((B_HEAD_DIM, B_HEAD_DIM), lambda b, i: (0, 0))
    return pl.pallas_call(
        functools.partial(_nsa_prompt_kernel, seq=T, tq=tq),
        out_shape=jax.ShapeDtypeStruct((B, T, B_WIDTH), f32),
        grid=(B, T // tq),
        in_specs=[pl.BlockSpec((None, B_WIDTH, tq), lambda b, i: (b, 0, i)),
                  per_b4, per_b3(B_KV_WIDTH, T), per_b4, per_b3(B_KV_WIDTH, T),
                  per_b3(n_ch, 2 * B_KV_WIDTH), per_b3(n_ch, 2 * B_KV_WIDTH), w_spec, w_spec,
                  pl.BlockSpec((None, tq, 128), lambda b, i: (b, i, C_NG // 128)),
                  pl.BlockSpec((None, tq, B_WIDTH), lambda b, i: (b, i, C_GB // B_WIDTH))],
        out_specs=pl.BlockSpec((None, tq, B_WIDTH), lambda b, i: (b, i, 0)),
        scratch_shapes=[pltpu.VMEM((B_KV_HEADS, n_ch, B_HEAD_DIM), bf16),
                        pltpu.VMEM((B_KV_HEADS, B_HEAD_DIM, n_ch), bf16),
                        pltpu.VMEM((n_sel, 1, B_KV_HEADS * tq), f32),
                        pltpu.VMEM((B_KV_HEADS, B_HEAD_DIM, lanes), f32),
                        pltpu.VMEM((B_KV_HEADS, B_HEAD_DIM + K_AUG, lanes), bf16),
                        pltpu.VMEM((2 * B_KV_HEADS, 1, lanes), f32),
                        pltpu.VMEM((2 * B_KV_HEADS, 1, lanes), f32),
                        pltpu.VMEM((2 * B_KV_HEADS, B_HEAD_DIM, lanes), f32),
                        pltpu.VMEM((B_WIDTH, tq), f32)],
        compiler_params=_cparams(("parallel", "arbitrary")),
        name="nsa_prompt",
    )(qt, ks, vst, kw, vwt, a, b, wck, wcv, u3, u3)


def _softmax_rows_update(sc, m, l, acc, pv_fn):
    m_new = jnp.maximum(m, jnp.max(sc, axis=-1, keepdims=True))
    alpha = jnp.exp(m - m_new)
    p = jnp.where(sc > 0.5 * NEG, jnp.exp(sc - m_new), 0.0)
    return m_new, l * alpha + jnp.sum(p, axis=-1, keepdims=True), acc * alpha + pv_fn(p)


def _sample_rows(past, t_pad):
    R = B_GROUP * t_pad
    row = lax.broadcasted_iota(jnp.int32, (R, 1), 0)
    t_col = (past + row % t_pad).astype(f32)
    slope_col = []
    for g in range(B_KV_HEADS):
        sc_ = jnp.zeros((R, 1), f32)
        for r in range(B_GROUP):
            sc_ = jnp.where(row // t_pad == r, _alibi_slope(g * B_GROUP + r), sc_)
        slope_col.append(sc_)
    return t_col, slope_col


def _sample_queries(q_ref):
    q = q_ref[...] * ATTN_SCALE
    return [jnp.concatenate([q[:, (g * B_GROUP + r) * 64:(g * B_GROUP + r + 1) * 64] for r in range(B_GROUP)],
                            axis=0).astype(bf16) for g in range(B_KV_HEADS)]


def _ksl(g):
    return slice(g * B_HEAD_DIM, (g + 1) * B_HEAD_DIM)


def _vsl(g):
    return slice(B_KV_WIDTH + g * B_HEAD_DIM, B_KV_WIDTH + (g + 1) * B_HEAD_DIM)


def _nsa_sample_cmp_kernel(pt_ref, *refs, n_pages, pages_per_step, t_pad):
    P = pages_per_step
    cmp_pages = refs[:P]
    q_ref, pw0_ref, pw1_ref, pool_ref, wck_ref, wcv_ref, ocmp_ref, keep_ref, at_ref, bt_ref = refs[P:]
    s = pl.program_id(1)
    S = n_pages // P
    G = B_KV_HEADS
    gs = range(G)
    past = n_pages * PAGE_SIZE
    n_ch = past // CMP_STRIDE
    n_sel = past // SEL_BLOCK + 1
    n_j = keep_ref.shape[1]
    t_col, slope_col = _sample_rows(past, t_pad)
    ksl, vsl = _ksl, _vsl

    pw0, pw1 = pw0_ref[...], pw1_ref[...]
    xa = jnp.concatenate([(cmp_pages[k][...] * pw0).astype(bf16) for k in range(P)], axis=1)
    xb = jnp.concatenate([(cmp_pages[k][...] * pw1).astype(bf16) for k in range(P)], axis=1)
    n_step = P * PAGE_SIZE // CMP_STRIDE
    off = pl.multiple_of(s * n_step, n_step)
    at_ref[:, pl.ds(off, n_step)] = jnp.dot(xa, pool_ref[...], preferred_element_type=f32)
    bt_ref[:, pl.ds(off, n_step)] = jnp.dot(xb, pool_ref[...], preferred_element_type=f32)

    @pl.when(s == S - 1)
    def _():
        pooled = at_ref[...] + pltpu.roll(bt_ref[...], n_ch - 1, 1)
        qg = _sample_queries(q_ref)
        kct =[_mm_tn(wck_ref[...], pooled[ksl(g), :]).astype(bf16) for g in gs]
        vct = [_mm_tn(wcv_ref[...], pooled[vsl(g), :]).astype(bf16) for g in gs]
        blk_end = (lax.broadcasted_iota(jnp.int32, (1, n_ch), 1) * CMP_STRIDE + (CMP_BLOCK - 1)).astype(f32)
        dist = t_col - blk_end
        mask = dist >= 0.0
        sc = [jnp.dot(qg[g], kct[g], preferred_element_type=f32) for g in gs]
        sc = [jnp.where(mask, sc[g] - slope_col[g] * dist, NEG) for g in gs]
        mx = [jnp.max(sc[g], axis=-1, keepdims=True) for g in gs]
        pr = [jnp.where(mask, jnp.exp(sc[g] - mx[g]), 0.0) for g in gs]
        inv = [1.0 / jnp.maximum(jnp.sum(pr[g], axis=-1, keepdims=True), 1e-30) for g in gs]
        pr = [pr[g] * inv[g] for g in gs]
        for g in gs:
            ocmp_ref[g] = _mm_nt(pr[g], vct[g])
        psum = [pr[g][0:t_pad] for g in gs]
        for r in range(1, B_GROUP):
            psum = [psum[g] + pr[g][r * t_pad:(r + 1) * t_pad] for g in gs]
        psum = jnp.concatenate(psum, axis=0)
        n_i = lax.broadcasted_iota(jnp.int32, (n_ch, n_j), 0) * CMP_STRIDE
        j_i = lax.broadcasted_iota(jnp.int32, (n_ch, n_j), 1) * SEL_BLOCK
        overlap = jnp.maximum(jnp.minimum(n_i + CMP_BLOCK, j_i + SEL_BLOCK) - jnp.maximum(n_i, j_i), 0).astype(f32) \
            * (1.0 / CMP_BLOCK)
        imp = jnp.dot(psum, overlap, preferred_element_type=f32, precision=HI)
        j = lax.broadcasted_iota(jnp.int32, (1, n_j), 1)
        tok = past + lax.broadcasted_iota(jnp.int32, (G * t_pad, 1), 0) % t_pad
        back = tok // SEL_BLOCK - j
        forced = (j == 0) | ((back >= 0) & (back < N_LOCAL))
        score = jnp.where(forced, FORCED_SCORE, jnp.where(back >= 0, imp, -1.0))
        score = jnp.where(j < n_sel, score, -2.0)
        rank = jnp.zeros(score.shape, f32)
        for i in range(n_sel):
            col = score[:, i:i + 1]
            rank = rank + jnp.where(col > score, 1.0, jnp.where(col == score, jnp.where(j > i, 1.0, 0.0), 0.0))
        keep_ref[...] = jnp.where(rank < float(min(N_SELECT, n_sel)), jnp.where(score >= 0.0, 1.0, 0.0), 0.0)

def _nsa_sample_sel_kernel(pt_ref, need_ref, src_ref, *refs, n_pages, pages_per_step, t_pad):
    P = pages_per_step
    sel_pages = refs[:P]
    (win_ref, q_ref, news_ref, neww_ref, gates_ref, gb_ref, ocmp_ref, keep_ref,
     o_ref, qg_ref, m_ref, l_ref, acc_ref) = refs[P:]
    b = pl.program_id(0)
    s = pl.program_id(1)
    S = n_pages // P
    G = B_KV_HEADS
    gs = range(G)
    past = n_pages * PAGE_SIZE
    n_sel = past // SEL_BLOCK + 1
    n_j = keep_ref.shape[1]
    R = B_GROUP * t_pad
    t_col, slope_col = _sample_rows(past, t_pad)
    ksl, vsl = _ksl, _vsl

    @pl.when(s == 0)
    def _():
        qg = _sample_queries(q_ref)
        for g in gs:
            qg_ref[g] = qg[g]
            m_ref[g] = jnp.full((R, 1), NEG, f32)
            l_ref[g] = jnp.zeros((R, 1), f32)
            acc_ref[g] = jnp.zeros((R, B_HEAD_DIM), f32)

    @pl.when(need_ref[b, s] == 1)
    def _():
        ss = s
        n_blk = P * PAGE_SIZE // SEL_BLOCK
        n_key = P * PAGE_SIZE
        pick = (lax.broadcasted_iota(jnp.int32, (n_j, n_blk), 0)
                == lax.broadcasted_iota(jnp.int32, (n_j, n_blk), 1) + ss * n_blk).astype(bf16)
        keep_blk = jnp.dot(keep_ref[...].astype(bf16), pick, preferred_element_type=f32)
        expand = (lax.broadcasted_iota(jnp.int32, (n_blk, n_key), 0)
                  == lax.broadcasted_iota(jnp.int32, (n_blk, n_key), 1) // SEL_BLOCK).astype(bf16)
        bias = (jnp.dot(keep_blk.astype(bf16), expand, preferred_element_type=f32) - 1.0) * (-NEG)
        kpos = (ss * n_key + lax.broadcasted_iota(jnp.int32, (1, n_key), 1)).astype(f32)
        dist = t_col - kpos
        kt = [jnp.concatenate([sel_pages[k][ksl(g), :].astype(bf16) for k in range(P)], axis=1) for g in gs]
        vt = [jnp.concatenate([sel_pages[k][vsl(g), :].astype(bf16) for k in range(P)], axis=1) for g in gs]
        sc = [jnp.dot(qg_ref[g], kt[g], preferred_element_type=f32) for g in gs]
        sc = [sc[g] - slope_col[g] * dist + jnp.concatenate([bias[g * t_pad:(g + 1) * t_pad]] * B_GROUP, axis=0)
              for g in gs]
        m_new = [jnp.maximum(m_ref[g], jnp.max(sc[g], axis=-1, keepdims=True)) for g in gs]
        alpha = [jnp.exp(m_ref[g] - m_new[g]) for g in gs]
        pr = [jnp.where(sc[g] > 0.5 * NEG, jnp.exp(sc[g] - m_new[g]), 0.0) for g in gs]
        l_new = [l_ref[g] * alpha[g] + jnp.sum(pr[g], axis=-1, keepdims=True) for g in gs]
        pv = [_mm_nt(pr[g], vt[g]) for g in gs]
        for g in gs:
            acc_ref[g] = acc_ref[g] * alpha[g] + pv[g]
            m_ref[g] = m_new[g]
            l_ref[g] = l_new[g]

    @pl.when(s == S - 1)
    def _():
        xs = news_ref[...]
        xw = neww_ref[...]
        win = win_ref[...]
        n_win = win.shape[1]
        new_dist = t_col - (past + lax.broadcasted_iota(jnp.int32, (1, t_pad), 1)).astype(f32)
        win_dist = t_col - (past - n_win + lax.broadcasted_iota(jnp.int32, (1, n_win), 1)).astype(f32)
        keep_new = keep_ref[:, n_sel - 1:n_sel]
        gates = jax.nn.sigmoid(gates_ref[...])
        gb = gb_ref[...]
        silu_gb = gb * jax.nn.sigmoid(gb)
        qg = [qg_ref[g] for g in gs]
        sc = [_mm_nt(qg[g], xs[:, ksl(g)]) - slope_col[g] * new_dist for g in gs]
        sc = [jnp.where(new_dist >= 0.0, sc[g], NEG)
              + (jnp.concatenate([keep_new[g * t_pad:(g + 1) * t_pad]] * B_GROUP, axis=0) - 1.0) * (-NEG) for g in gs]
        sel = [_softmax_rows_update(sc[g], m_ref[g], l_ref[g], acc_ref[g], lambda p, g=g: _mm(p, xs[:, vsl(g)]))
               for g in gs]
        o_sel = [sel[g][2] * (1.0 / jnp.maximum(sel[g][1], 1e-30)) for g in gs]
        sw = [jnp.dot(qg[g], win[ksl(g), :].astype(bf16), preferred_element_type=f32) - slope_col[g] * win_dist
              for g in gs]
        sw = [jnp.where(win_dist <= float(WINDOW), sw[g], NEG) for g in gs]
        sn = [_mm_nt(qg[g], xw[:, ksl(g)]) - slope_col[g] * new_dist for g in gs]
        sn = [jnp.where(new_dist >= 0.0, jnp.where(new_dist <= float(WINDOW), sn[g], NEG), NEG) for g in gs]
        mw = [jnp.maximum(jnp.max(sw[g], axis=-1, keepdims=True), jnp.max(sn[g], axis=-1, keepdims=True)) for g in gs]
        pw = [jnp.where(sw[g] > 0.5 * NEG, jnp.exp(sw[g] - mw[g]), 0.0) for g in gs]
        pn = [jnp.where(sn[g] > 0.5 * NEG, jnp.exp(sn[g] - mw[g]), 0.0) for g in gs]
        lw = [jnp.sum(pw[g], axis=-1, keepdims=True) + jnp.sum(pn[g], axis=-1, keepdims=True) for g in gs]
        o_win = [(_mm_nt(pw[g], win[vsl(g), :]) + _mm(pn[g], xw[:, vsl(g)])) * (1.0 / jnp.maximum(lw[g], 1e-30))
                 for g in gs]
        for g in gs:
            def gate_col(branch):
                base = branch * B_HEADS + g * B_GROUP
                return jnp.concatenate([gates[:, base + r:base + r + 1] for r in range(B_GROUP)], axis=0)

            o = gate_col(0) * ocmp_ref[g] + gate_col(1) * o_sel[g] + gate_col(2) * o_win[g]
            for r in range(B_GROUP):
                cols = slice((g * B_GROUP + r) * 64, (g * B_GROUP + r + 1) * 64)
                o_ref[:, cols] = o[r * t_pad:(r + 1) * t_pad, :] * silu_gb[:, cols]


def _nsa_sample(u3, cache_cmp_t, cache_sel_t, cache_win_t, page_table, pw0_t, pw1_t, wck, wcv, *, t_new, pages_per_step):
    DB, t_pad, _ = u3.shape
    n_pages = page_table.shape[1]
    P = pages_per_step
    P2 = min(8, n_pages)
    assert t_new < CMP_STRIDE and t_new <= t_pad and n_pages % P == 0 and (P * PAGE_SIZE // CMP_STRIDE) % 128 == 0
    assert n_pages % P2 == 0
    S, S2 = n_pages // P, n_pages // P2
    past = n_pages * PAGE_SIZE
    n_ch = past // CMP_STRIDE
    n_sel = past // SEL_BLOCK + 1
    n_j = -(-n_sel // 128) * 128
    R = B_GROUP * t_pad
    G = B_KV_HEADS
    kvw = 2 * B_KV_WIDTH
    pool = (lax.broadcasted_iota(jnp.int32, (P * PAGE_SIZE, P * PAGE_SIZE // CMP_STRIDE), 0) // CMP_STRIDE
            == lax.broadcasted_iota(jnp.int32, (P * PAGE_SIZE, P * PAGE_SIZE // CMP_STRIDE), 1)).astype(bf16)
    page = lambda fn: pl.BlockSpec((None, kvw, PAGE_SIZE), fn)

    ublk = lambda w, off: pl.BlockSpec((None, t_pad, w), lambda b, s, pt: (b, 0, off // w))
    const = lambda s1, s2: pl.BlockSpec((s1, s2), lambda b, s, pt: (0, 0))
    o_cmp, keep = pl.pallas_call(
        functools.partial(_nsa_sample_cmp_kernel, n_pages=n_pages, pages_per_step=P, t_pad=t_pad),
        out_shape=(jax.ShapeDtypeStruct((DB, G, R, B_HEAD_DIM), f32), jax.ShapeDtypeStruct((DB, G * t_pad, n_j), f32)),
        grid_spec=pltpu.PrefetchScalarGridSpec(
            num_scalar_prefetch=1,
            grid=(DB, S),
            in_specs=[page(lambda b, s, pt, k=k: (pt[b, s * P + k], 0, 0)) for k in range(P)] + [
                ublk(B_WIDTH, C_Q), const(kvw, PAGE_SIZE), const(kvw, PAGE_SIZE), const(*pool.shape),
                const(B_HEAD_DIM, B_HEAD_DIM), const(B_HEAD_DIM, B_HEAD_DIM)],
            out_specs=(pl.BlockSpec((None, G, R, B_HEAD_DIM), lambda b, s, pt: (b, 0, 0, 0)),
                       pl.BlockSpec((None, G * t_pad, n_j), lambda b, s, pt: (b, 0, 0))),
            scratch_shapes=[pltpu.VMEM((kvw, n_ch), f32), pltpu.VMEM((kvw, n_ch), f32)]),
        compiler_params=_cparams(("parallel", "arbitrary")),
        name="nsa_sample_cmp",
    )(page_table, *([cache_cmp_t] * P), u3, pw0_t, pw1_t, pool, wck, wcv)

    blocks_per_step = P2 * PAGE_SIZE // SEL_BLOCK
    need = jnp.any(keep[:, :, :S2 * blocks_per_step].reshape(DB, G * t_pad, S2, blocks_per_step) > 0.5, axis=(1, 3))
    src = jnp.maximum(lax.cummax(jnp.where(need, jnp.arange(S2, dtype=jnp.int32), -1), axis=1), 0)

    ublk = lambda w, off: pl.BlockSpec((None, t_pad, w), lambda b, s, pt, nd, sr: (b, 0, off // w))
    return pl.pallas_call(
        functools.partial(_nsa_sample_sel_kernel, n_pages=n_pages, pages_per_step=P2, t_pad=t_pad),
        out_shape=jax.ShapeDtypeStruct((DB, t_pad, B_WIDTH), f32),
        grid_spec=pltpu.PrefetchScalarGridSpec(
            num_scalar_prefetch=3,
            grid=(DB, S2),
            in_specs=[page(lambda b, s, pt, nd, sr, k=k: (pt[b, sr[b, s] * P2 + k], 0, 0)) for k in range(P2)] + [
                pl.BlockSpec((None, kvw, cache_win_t.shape[2]), lambda b, s, pt, nd, sr: (b, 0, 0)),
                ublk(B_WIDTH, C_Q), ublk(kvw, C_KVS), ublk(kvw, C_KVW), ublk(128, C_NG), ublk(B_WIDTH, C_GB),
                pl.BlockSpec((None, G, R, B_HEAD_DIM), lambda b, s, pt, nd, sr: (b, 0, 0, 0)),
                pl.BlockSpec((None, G * t_pad, n_j), lambda b, s, pt, nd, sr: (b, 0, 0))],
            out_specs=pl.BlockSpec((None, t_pad, B_WIDTH), lambda b, s, pt, nd, sr: (b, 0, 0)),
            scratch_shapes=[pltpu.VMEM((G, R, B_HEAD_DIM), bf16),
                            pltpu.VMEM((G, R, 1), f32), pltpu.VMEM((G, R, 1), f32),
                            pltpu.VMEM((G, R, B_HEAD_DIM), f32)]),
        compiler_params=_cparams(("parallel", "arbitrary")),
        name="nsa_sample_sel",
    )(page_table, need.astype(jnp.int32), src, *([cache_sel_t] * P2), cache_win_t, u3, u3, u3, u3, u3, o_cmp, keep)


def _pos_weight_tiles(pos_k, pos_v, rows):
    def half(lo):
        t = jnp.concatenate([jnp.tile(pos_k[lo:lo + CMP_STRIDE], (1, B_KV_HEADS)),
                             jnp.tile(pos_v[lo:lo + CMP_STRIDE], (1, B_KV_HEADS))], axis=1)
        return jnp.tile(t, (rows // CMP_STRIDE, 1))
    return half(0), half(CMP_STRIDE)


def _a_cols(u_rows):
    return jnp.concatenate([u_rows[..., :4 * A_WIDTH], u_rows[..., C_LR:C_LR + DECAY_RANK + ICLR_RANK]], axis=-1)


def kernel(x_prompt, x_sample, cache_cmp_kv, cache_sel_kv, cache_win_kv, state_rwkv, state_shift, page_table, norm_in,
           w_in, mu_shift, w0, w_up, a0, a_up, k_k, k_a, r_k, gn_w, gn_b, cmp_pos_k, cmp_pos_v, w_cmp_k, w_cmp_v,
           w_pa, w_pb, w_o, norm_out):
    assert w_in.shape[0] == 1, "one layer"
    B, T, _ = x_prompt.shape
    DB, TS, _ = x_sample.shape
    H, N = A_HEADS, A_HEAD_DIM
    kvs = (2, B_KV_HEADS, B_HEAD_DIM)
    p = dict(mu_shift=mu_shift[0], w0=w0[0], w_up=w_up[0], a0=a0[0], a_up=a_up[0], k_k=k_k[0], k_a=k_a[0],
             r_k=r_k[0].reshape(-1), gn_w=gn_w[0], gn_b=gn_b[0])
    w_packed = _pack_w_in(w_in[0])
    wpa, wpb, wo = w_pa[0].astype(bf16), w_pb[0].astype(bf16), w_o[0].astype(bf16)

    xp = x_prompt.reshape(B * T, D_MODEL)
    up = _proj_in(xp, norm_in[0], w_packed)
    up3 = up.reshape(B, T, U_COLS)
    oa_p, s_p = _rwkv_wide(up3, jnp.zeros((B, 1, 4 * A_WIDTH), f32), jnp.zeros((B, 1, 128), f32),
                      jnp.zeros((B, H, N, N), f32), p, chunk=64, t_valid=64, rows=math.gcd(B, 4))
    tt = 512
    pw0, pw1 = _pos_weight_tiles(cmp_pos_k[0], cmp_pos_v[0], tt)
    qt, ks, vst, kw, vwt, pa, pb = _nsa_prep(up3, pw0, pw1, tt)
    ob_p = _nsa_prompt(up3, qt, ks, vst, kw, vwt, pa, pb, w_cmp_k[0], w_cmp_v[0], 128)
    y_p = _merge(xp, oa_p.reshape(B * T, A_WIDTH), ob_p.reshape(B * T, B_WIDTH), up, wpa, wpb, wo, norm_out)

    t_pad = 8
    xs = x_sample.reshape(DB * TS, D_MODEL)
    us = _proj_in(xs, norm_in[0], w_packed)
    us3 = us.reshape(DB, TS, U_COLS)
    us3p = jnp.pad(us3, ((0, 0), (0, t_pad - TS), (0, 0)))
    shift0 = state_shift[0]
    oa_s, s_s = _rwkv_wide(us3p, shift0[:, None, :4 * A_WIDTH], shift0[:, None, 4 * A_WIDTH:], state_rwkv[0], p,
                      chunk=t_pad, t_valid=TS, rows=2 if DB % 2 == 0 else 1)
    n_pool = cache_cmp_kv.shape[1]
    win = cache_win_kv[0]
    rows_last = lambda c, lead: jnp.transpose(c, (0, 2, 3, 4, 1)).reshape(lead, 2 * B_KV_WIDTH, c.shape[1])
    ob_s = _nsa_sample(us3p, rows_last(cache_cmp_kv[0], n_pool), rows_last(cache_sel_kv[0], n_pool),
                       rows_last(win, DB), page_table, pw0[:PAGE_SIZE].T, pw1[:PAGE_SIZE].T, w_cmp_k[0], w_cmp_v[0],
                       t_new=TS, pages_per_step=min(32, page_table.shape[1]))
    y_s = _merge(xs, oa_s[:, :TS].reshape(DB * TS, A_WIDTH), ob_s[:, :TS].reshape(DB * TS, B_WIDTH), us, wpa, wpb, wo,
                 norm_out)

    def kv_out(u3_, col, lead, t):
        return u3_[..., col:col + 2 * B_KV_WIDTH].reshape((1, lead, t) + kvs)

    wk = min(WINDOW, T)
    new_w_s = kv_out(us3, C_KVW, DB, TS)[0]
    s_win = jnp.concatenate([win, new_w_s], axis=1)[:, TS:][None]
    return (y_p.reshape(B, T, D_MODEL), y_s.reshape(DB, TS, D_MODEL),
            kv_out(up3, C_KVC, B, T), kv_out(up3, C_KVS, B, T), kv_out(up3[:, T - wk:], C_KVW, B, wk),
            s_p[None], _a_cols(up3[:, T - 1])[None],
            kv_out(us3, C_KVC, DB, TS), kv_out(us3, C_KVS, DB, TS), s_win,
            s_s[None], _a_cols(us3[:, TS - 1])[None])
```

```python
import functools
import math

import jax
import jax.numpy as jnp
from jax import lax
from jax.experimental import pallas as pl
from jax.experimental.pallas import tpu as pltpu

f32 = jnp.float32
bf16 = jnp.bfloat16

D_MODEL = 2048
PAGE_SIZE = 128
A_HEADS = 16
A_HEAD_DIM = 64
A_WIDTH = A_HEADS * A_HEAD_DIM
DECAY_RANK = 64
ICLR_RANK = 64
A_COLS = 4 * A_WIDTH + DECAY_RANK + ICLR_RANK
GN_EPS = 64e-5
B_HEADS = 16
B_KV_HEADS = 4
B_GROUP = B_HEADS // B_KV_HEADS
B_HEAD_DIM = 64
B_WIDTH = B_HEADS * B_HEAD_DIM
B_KV_WIDTH = B_KV_HEADS * B_HEAD_DIM
B_COLS = 2 * B_WIDTH + 6 * B_KV_WIDTH + 3 * B_HEADS
CMP_BLOCK = 32
CMP_STRIDE = 16
SEL_BLOCK = 64
N_SELECT = 16
N_LOCAL = 2
WINDOW = 512
FORCED_SCORE = 1e4
ATTN_SCALE = B_HEAD_DIM ** -0.5
RMS_EPS = 1e-6
NEG = -1e30
LOG2E = math.log2(math.e)

C_RKVG = 0
C_Q = 4096
C_GB = 5120
C_GA_M = 6144
C_GB_M = 8192
C_KVC = 10240
C_KVS = 10752
C_KVW = 11264
C_LR = 11776
C_NG = 11904
U_COLS = 12288

VMEM_LIMIT = 56 * 1024 * 1024
HI = lax.Precision.HIGHEST


def _cparams(sem):
    return pltpu.CompilerParams(dimension_semantics=sem, vmem_limit_bytes=VMEM_LIMIT)


def _pack_w_in(w):
    wt = w.T
    a, b, m = wt[:A_COLS], wt[A_COLS:A_COLS + B_COLS], wt[A_COLS + B_COLS:]
    z = jnp.zeros((U_COLS - C_NG - 3 * B_HEADS, w.shape[0]), w.dtype)
    return jnp.concatenate(
        [a[:4 * A_WIDTH], b[:2 * B_WIDTH], m, b[2 * B_WIDTH:2 * B_WIDTH + 6 * B_KV_WIDTH],
         a[4 * A_WIDTH:], b[2 * B_WIDTH + 6 * B_KV_WIDTH:], z], axis=0).astype(bf16)


def _proj_in_kernel(x_ref, g_ref, w_ref, o_ref, xn_ref):
    @pl.when(pl.program_id(1) == 0)
    def _():
        x = x_ref[...]
        ms = jnp.mean(x * x, axis=-1, keepdims=True)
        xn_ref[...] = (x * lax.rsqrt(ms + RMS_EPS) * g_ref[...]).astype(bf16)

    o_ref[...] = lax.dot_general(xn_ref[...], w_ref[...], (((1,), (1,)), ((), ())), preferred_element_type=f32)


def _proj_in(x2d, norm_g, w_packed_t):
    m = x2d.shape[0]
    tm = min(1024, m)
    tn = 1024
    return pl.pallas_call(
        _proj_in_kernel,
        out_shape=jax.ShapeDtypeStruct((m, U_COLS), f32),
        grid=(m // tm, U_COLS // tn),
        in_specs=[pl.BlockSpec((tm, D_MODEL), lambda i, j: (i, 0)),
                  pl.BlockSpec((1, D_MODEL), lambda i, j: (0, 0)),
                  pl.BlockSpec((tn, D_MODEL), lambda i, j: (j, 0))],
        out_specs=pl.BlockSpec((tm, tn), lambda i, j: (i, j)),
        scratch_shapes=[pltpu.VMEM((tm, D_MODEL), bf16)],
        compiler_params=_cparams(("parallel", "arbitrary")),
        name="proj_in",
    )(x2d, norm_g.reshape(1, D_MODEL), w_packed_t)


def _mm(a, b):
    return jnp.dot(a.astype(bf16), b.astype(bf16), preferred_element_type=f32)


def _mm_nt(a, b):
    return lax.dot_general(a.astype(bf16), b.astype(bf16), (((1,), (1,)), ((), ())), preferred_element_type=f32)


def _mm_tn(a, b):
    return lax.dot_general(a.astype(bf16), b.astype(bf16), (((0,), (0,)), ((), ())), preferred_element_type=f32)


HEADS_PER_GROUP = 4
N_HEAD_GROUPS = A_HEADS // HEADS_PER_GROUP
GROUP_WIDTH = HEADS_PER_GROUP * A_HEAD_DIM


def _rwkv_wide_kernel(um_ref, ulr_ref, pm_ref, plr_ref, s0_ref, mum_ref, mulr_ref, w0_ref, wup_ref, a0_ref,
                      aup_ref, kk_ref, ka_ref, rk_ref, gnw_ref, gnb_ref, o_ref, sout_ref,
                      s_ref, prevm_ref, prevlr_ref, *, chunk, t_valid):
    C = chunk
    N, HG, NG, GW = A_HEAD_DIM, HEADS_PER_GROUP, N_HEAD_GROUPS, GROUP_WIDTH
    RB = um_ref.shape[0]
    c = pl.program_id(1)

    @pl.when(c == 0)
    def _():
        for bi in range(RB):
            for q in range(NG):
                s_ref[bi, q] = jnp.concatenate([s0_ref[bi, q * HG + h] for h in range(HG)], axis=1)
        prevm_ref[...] = pm_ref[...]
        prevlr_ref[...] = plr_ref[...]

    row = lax.broadcasted_iota(jnp.int32, (C, 1), 0)
    tril_b = (lax.broadcasted_iota(jnp.int32, (C, C), 0) >= lax.broadcasted_iota(jnp.int32, (C, C), 1)).astype(bf16)
    t_i = lax.broadcasted_iota(jnp.int32, (C, HG * C), 0)
    s_i = lax.broadcasted_iota(jnp.int32, (C, HG * C), 1) % C
    tril_incl = t_i >= s_i
    tril_strict = t_i > s_i
    eye = (t_i == s_i).astype(f32)
    head_of_dim = lax.broadcasted_iota(jnp.int32, (1, GW), 1) // N
    head_of_tok = lax.broadcasted_iota(jnp.int32, (1, HG * C), 1) // C
    ones_bd = (lax.broadcasted_iota(jnp.int32, (GW, GW), 0) // N
               == lax.broadcasted_iota(jnp.int32, (GW, GW), 1) // N).astype(bf16)
    n_sq = int(math.log2(C)) - 1
    valid = (row < t_valid).astype(f32) if t_valid < C else None
    gcols = [slice(q * GW, (q + 1) * GW) for q in range(NG)]
    qs = range(NG)

    def shifted(u, prev):
        return jnp.where(row == 0, prev, pltpu.roll(u, 1, 0))

    def head_sums(*xs):
        stacked = jnp.concatenate([x.astype(bf16)[:, gc] for x in xs for gc in gcols], axis=0)
        sums = jnp.dot(stacked, ones_bd, preferred_element_type=f32)
        return [jnp.concatenate([sums[(i * NG + q) * C:(i * NG + q + 1) * C] for q in qs], axis=1)
                for i in range(len(xs))]

    def block_rows(x, head_of_lane):
        return jnp.concatenate([jnp.where(head_of_lane == h, x, 0.0) for h in range(HG)], axis=0).astype(bf16)

    def mm(a, b):
        return jnp.dot(a.astype(bf16), b, preferred_element_type=f32)

    def mm_nt(a, b):
        return lax.dot_general(a.astype(bf16), b, (((1,), (1,)), ((), ())), preferred_element_type=f32)

    def prepare(bi, out):
        um = um_ref[bi]
        ulr = ulr_ref[bi]
        prev_m = prevm_ref[bi]
        prev_lr = prevlr_ref[bi]
        prevm_ref[bi] = um[C - 1:C, :]
        prevlr_ref[bi] = ulr[C - 1:C, :]
        uslr = ulr + mulr_ref[...] * (shifted(ulr, prev_lr) - ulr)
        zw = w0_ref[...] + _mm(jnp.tanh(uslr[:, :DECAY_RANK]), wup_ref[...])
        a = jax.nn.sigmoid(a0_ref[...] + _mm(uslr[:, DECAY_RANK:], aup_ref[...]))
        yield
        secs = []
        for i in range(4):
            cols = slice(i * A_WIDTH, (i + 1) * A_WIDTH)
            secs.append(um[:, cols] + mum_ref[:, cols] * (shifted(um[:, cols], prev_m[:, cols]) - um[:, cols]))
            yield
        r, k, v, g = secs
        logw = -math.exp(-0.5) * jax.nn.sigmoid(zw)
        if valid is not None:
            logw = logw * valid
        yield
        logw_hi = logw.astype(bf16)
        logw_lo = (logw - logw_hi.astype(f32)).astype(bf16)
        cum = (jnp.dot(tril_b, logw_hi, preferred_element_type=f32)
               + jnp.dot(tril_b, logw_lo, preferred_element_type=f32))
        yield
        e_pos = jnp.exp(cum)
        e_neg = jnp.exp(-cum)
        yield
        e_prev = jnp.exp(cum - logw)
        kk = k * kk_ref[...]
        yield
        k2 = k * (1.0 + (a - 1.0) * ka_ref[...])
        yield
        ssq, bonus = head_sums(kk * kk, r * k2 * rk_ref[...])
        yield
        kk = kk * lax.rsqrt(jnp.maximum(ssq, 1e-24))
        kn = k2 * e_neg
        bn = kk * a * e_neg
        if valid is not None:
            kn = kn * valid
            bn = bn * valid
        yield
        lhs = jnp.concatenate([kk * e_prev, r * e_pos], axis=0).astype(bf16)
        out.update(v=v, g=g, bonus=bonus, kn=kn, bn=bn, lhs=lhs, e_last=e_pos[C - 1:C, :])
        yield

    def solve(bi, pre, out):
        v = [pre["v"][:, gc] for gc in gcols]
        kn = [pre["kn"][:, gc] for gc in gcols]
        bn = [pre["bn"][:, gc] for gc in gcols]
        lhs = [pre["lhs"][:, gc] for gc in gcols]
        s0 = [s_ref[bi, q] for q in qs]
        qk_b = [mm_nt(lhs[q], block_rows(bn[q], head_of_dim)) for q in qs]
        yield
        qk_k = [mm_nt(lhs[q], block_rows(kn[q], head_of_dim)) for q in qs]
        yield
        w0s = [mm_nt(lhs[q], block_rows(s0[q], head_of_dim)) for q in qs]
        yield
        pw = [jnp.where(tril_strict, -qk_b[q][:C], 0.0) for q in qs]
        tinv = [eye + pw[q] for q in qs]
        bd = [block_rows(pw[q], head_of_tok) for q in qs]
        for _ in range(n_sq):
            pw = [mm(pw[q], bd[q]) for q in qs]
            yield
            bd = [block_rows(pw[q], head_of_tok) for q in qs]
            tinv = [tinv[q] + mm(tinv[q], bd[q]) for q in qs]
            yield
        bd_v = [block_rows(v[q], head_of_dim) for q in qs]
        rhs_u = [w0s[q][:C] + mm(jnp.where(tril_strict, qk_k[q][:C], 0.0), bd_v[q]) for q in qs]
        yield
        u = [mm(tinv[q], block_rows(rhs_u[q], head_of_dim)) for q in qs]
        yield
        p_cat = [jnp.concatenate([jnp.where(tril_incl, qk_k[q][C:], 0.0), jnp.where(tril_incl, -qk_b[q][C:], 0.0)],
                                 axis=1) for q in qs]
        out["o"] = [w0s[q][C:] + mm(p_cat[q], jnp.concatenate([bd_v[q], block_rows(u[q], head_of_dim)], axis=0))
                    for q in qs]
        yield
        full = [lax.dot_general(jnp.concatenate([v[q], -u[q]], axis=0).astype(bf16),
                                jnp.concatenate([kn[q], bn[q]], axis=0).astype(bf16),
                                (((0,), (0,)), ((), ())), preferred_element_type=f32) for q in qs]
        yield
        for q in qs:
            upd = jnp.where(head_of_dim == 0, full[q][0:N], 0.0)
            for h in range(1, HG):
                upd = upd + jnp.where(head_of_dim == h, full[q][h * N:(h + 1) * N], 0.0)
            s_ref[bi, q] = (s0[q] + upd) * pre["e_last"][:, gcols[q]]
        yield

    def finish(bi, pre, mid):
        o = jnp.concatenate(mid["o"], axis=1)
        mean = head_sums(o)[0] * (1.0 / N)
        yield
        d = o - mean
        var = head_sums(d * d)[0] * (1.0 / N)
        yield
        g = pre["g"]
        on = d * lax.rsqrt(var + GN_EPS) * gnw_ref[...] + gnb_ref[...] + pre["bonus"] * pre["v"]
        o_ref[bi] = on * (g * jax.nn.sigmoid(g))
        yield

    def interleave(gens):
        gens = list(gens)
        while gens:
            for gen in list(gens):
                if next(gen, "done") == "done":
                    gens.remove(gen)

    pre = [{} for _ in range(RB)]
    mid = [{} for _ in range(RB)]
    halves = [range(0, RB // 2), range(RB // 2, RB)] if RB > 1 else [range(RB)]
    prep = lambda rows_: [prepare(bi, pre[bi]) for bi in rows_]
    solv = lambda rows_: [solve(bi, pre[bi], mid[bi]) for bi in rows_]
    fin = lambda rows_: [finish(bi, pre[bi], mid[bi]) for bi in rows_]
    interleave(prep(halves[0]))
    for i, rows_ in enumerate(halves):
        interleave(solv(rows_) + (prep(halves[i + 1]) if i + 1 < len(halves) else [])
                   + (fin(halves[i - 1]) if i >= 1 else []))
    interleave(fin(halves[-1]))

    @pl.when(c == pl.num_programs(1) - 1)
    def _():
        for bi in range(RB):
            for q in range(NG):
                s_q = s_ref[bi, q]
                for h in range(HG):
                    sout_ref[bi, q * HG + h] = s_q[:, h * N:(h + 1) * N]


def _rwkv_wide(u3, prev_m, prev_lr, s0, p, *, chunk, t_valid, rows):
    B, T, _ = u3.shape
    C = chunk
    NG, N, GW = N_HEAD_GROUPS, A_HEAD_DIM, GROUP_WIDTH
    assert B % rows == 0 and T % C == 0
    row = lambda v: v.reshape(1, -1)
    vec = lambda n: pl.BlockSpec((1, n), lambda b, c: (0, 0))
    in_specs = [
        pl.BlockSpec((rows, C, 4 * A_WIDTH), lambda b, c: (b, c, 0)),
        pl.BlockSpec((rows, C, 128), lambda b, c: (b, c, C_LR // 128)),
        pl.BlockSpec((rows, 1, 4 * A_WIDTH), lambda b, c: (b, 0, 0)),
        pl.BlockSpec((rows, 1, 128), lambda b, c: (b, 0, 0)),
        pl.BlockSpec((rows, A_HEADS, N, N), lambda b, c: (b, 0, 0, 0)),
        vec(4 * A_WIDTH), vec(128), vec(A_WIDTH),
        pl.BlockSpec((DECAY_RANK, A_WIDTH), lambda b, c: (0, 0)),
        vec(A_WIDTH),
        pl.BlockSpec((ICLR_RANK, A_WIDTH), lambda b, c: (0, 0)),
        vec(A_WIDTH), vec(A_WIDTH), vec(A_WIDTH), vec(A_WIDTH), vec(A_WIDTH),
    ]
    o_a, s_out = pl.pallas_call(
        functools.partial(_rwkv_wide_kernel, chunk=C, t_valid=t_valid),
        out_shape=(jax.ShapeDtypeStruct((B, T, A_WIDTH), f32), jax.ShapeDtypeStruct((B, A_HEADS, N, N), f32)),
        grid=(B // rows, T // C),
        in_specs=in_specs,
        out_specs=(pl.BlockSpec((rows, C, A_WIDTH), lambda b, c: (b, c, 0)),
                   pl.BlockSpec((rows, A_HEADS, N, N), lambda b, c: (b, 0, 0, 0))),
        scratch_shapes=[pltpu.VMEM((rows, NG, N, GW), f32), pltpu.VMEM((rows, 1, 4 * A_WIDTH), f32),
                        pltpu.VMEM((rows, 1, 128), f32)],
        compiler_params=_cparams(("parallel", "arbitrary")),
        name="rwkv7",
    )(u3, u3, prev_m, prev_lr, s0, row(p["mu_shift"][:4 * A_WIDTH]), row(p["mu_shift"][4 * A_WIDTH:]),
      row(p["w0"]), p["w_up"], row(p["a0"]), p["a_up"], row(p["k_k"]), row(p["k_a"]), row(p["r_k"]),
      row(p["gn_w"]), row(p["gn_b"]))
    return o_a, s_out


def _merge_kernel(x_ref, oa_ref, ob_ref, ga_ref, gb_ref, wpa_ref, wpb_ref, wo_ref, gout_ref, y_ref):
    pa = jnp.dot(oa_ref[...].astype(bf16), wpa_ref[...], preferred_element_type=f32)
    pb = jnp.dot(ob_ref[...].astype(bf16), wpb_ref[...], preferred_element_type=f32)
    merged = jax.nn.sigmoid(ga_ref[...]) * pa + jax.nn.sigmoid(gb_ref[...]) * pb
    h = x_ref[...] + jnp.dot(merged.astype(bf16), wo_ref[...], preferred_element_type=f32)
    ms = jnp.mean(h * h, axis=-1, keepdims=True)
    y_ref[...] = h * lax.rsqrt(ms + RMS_EPS) * gout_ref[...]


def _merge(x2d, o_a, o_b, u2d, w_pa, w_pb, w_o, norm_out):
    m = x2d.shape[0]
    tm = min(256, m)
    const = lambda shape: pl.BlockSpec(shape, lambda i: (0, 0), pipeline_mode=pl.Buffered(1))
    return pl.pallas_call(
        _merge_kernel,
        out_shape=jax.ShapeDtypeStruct((m, D_MODEL), f32),
        grid=(m // tm,),
        in_specs=[pl.BlockSpec((tm, D_MODEL), lambda i: (i, 0)),
                  pl.BlockSpec((tm, A_WIDTH), lambda i: (i, 0)),
                  pl.BlockSpec((tm, B_WIDTH), lambda i: (i, 0)),
                  pl.BlockSpec((tm, D_MODEL), lambda i: (i, C_GA_M // D_MODEL)),
                  pl.BlockSpec((tm, D_MODEL), lambda i: (i, C_GB_M // D_MODEL)),
                  const((A_WIDTH, D_MODEL)), const((B_WIDTH, D_MODEL)), const((D_MODEL, D_MODEL)),
                  const((1, D_MODEL))],
        out_specs=pl.BlockSpec((tm, D_MODEL), lambda i: (i, 0)),
        compiler_params=_cparams(("parallel",)),
        name="merge_out",
    )(x2d, o_a, o_b, u2d, u2d, w_pa, w_pb, w_o, norm_out.reshape(1, D_MODEL))


def _alibi_slope(head):
    return 2.0 ** (-8.0 * (head + 1) / B_HEADS)


def _slope_row(g, tq):
    lane_head = lax.broadcasted_iota(jnp.int32, (1, B_GROUP * tq), 1) // tq
    out = jnp.zeros((1, B_GROUP * tq), f32)
    for r in range(B_GROUP):
        out = jnp.where(lane_head == r, _alibi_slope(g * B_GROUP + r), out)
    return out


def _tile_heads(row):
    return jnp.concatenate([row] * B_GROUP, axis=1)


def _overlap_t(n_sel_rows, n_ch):
    j = lax.broadcasted_iota(jnp.int32, (n_sel_rows, n_ch), 0) * SEL_BLOCK
    n = lax.broadcasted_iota(jnp.int32, (n_sel_rows, n_ch), 1) * CMP_STRIDE
    ov = jnp.minimum(n + CMP_BLOCK, j + SEL_BLOCK) - jnp.maximum(n, j)
    return jnp.maximum(ov, 0).astype(f32) * (1.0 / CMP_BLOCK)


M_INIT = 0.1 * NEG


def _attn_step_multi(s, carries, v_t):
    n = range(len(s))
    m_new = [jnp.maximum(carries[i][0], jnp.max(s[i], axis=0, keepdims=True)) for i in n]
    alpha = [jnp.exp2(carries[i][0] - m_new[i]) for i in n]
    p = [jnp.exp2(s[i] - m_new[i]) for i in n]
    l = [carries[i][1] * alpha[i] + jnp.sum(p[i], axis=0, keepdims=True) for i in n]
    pv = [jnp.dot(v_t[i], p[i].astype(bf16), preferred_element_type=f32) for i in n]
    return [(m_new[i], l[i], carries[i][2] * alpha[i] + pv[i]) for i in n]


def _attn_init(lanes):
    return (jnp.full((1, lanes), M_INIT, f32), jnp.zeros((1, lanes), f32), jnp.zeros((B_HEAD_DIM, lanes), f32))


def _split3(x):
    hi = x.astype(bf16)
    r1 = x - hi.astype(f32)
    mid = r1.astype(bf16)
    lo = (r1 - mid.astype(f32)).astype(bf16)
    return hi, mid, lo


POS_SPLIT = 128
K_AUG = 16


def _key_aug(pos_i):
    one = jnp.ones(pos_i.shape, f32)
    hi = (pos_i // POS_SPLIT).astype(f32)
    lo = (pos_i % POS_SPLIT).astype(f32)
    zero = jnp.zeros((pos_i.shape[0], K_AUG - 9), f32)
    return jnp.concatenate([one, one, one, hi, hi, hi, lo, lo, lo, zero], axis=1).astype(bf16)


def _query_aug(slope_row, t_row):
    rows = _split3(-slope_row * t_row) + _split3(slope_row * float(POS_SPLIT)) + _split3(slope_row)
    zero = jnp.zeros((K_AUG - 9, slope_row.shape[1]), bf16)
    return jnp.concatenate(list(rows) + [zero], axis=0)


def _attn_finish(carry):
    _, l, acc = carry
    return acc * (1.0 / jnp.maximum(l, 1e-30))


def _compressed_multi(kc, vc_t, q_t, t_row, slope_rows, n_ch):
    n = range(len(kc))
    blk_end = (lax.broadcasted_iota(jnp.int32, (n_ch, 1), 0) * CMP_STRIDE + (CMP_BLOCK - 1)).astype(f32)
    dist = t_row - blk_end
    mask = dist >= 0.0
    s = [jnp.dot(kc[i], q_t[i], preferred_element_type=f32) for i in n]
    s = [jnp.where(mask, s[i] - slope_rows[i] * dist, NEG) for i in n]
    m = [jnp.max(s[i], axis=0, keepdims=True) for i in n]
    p = [jnp.where(mask, jnp.exp2(s[i] - m[i]), 0.0) for i in n]
    inv = [1.0 / jnp.maximum(jnp.sum(p[i], axis=0, keepdims=True), 1e-30) for i in n]
    p = [p[i] * inv[i] for i in n]
    o = [jnp.dot(vc_t[i], p[i].astype(bf16), preferred_element_type=f32) for i in n]
    return o, p


def _selection_scores(p, tq, tok_row_i, n_rows, n_sel, n_ch):
    psum = p[:, 0:tq]
    for r in range(1, B_GROUP):
        psum = psum + p[:, r * tq:(r + 1) * tq]
    imp = jnp.dot(_overlap_t(n_rows, n_ch), psum, preferred_element_type=f32, precision=HI)
    j = lax.broadcasted_iota(jnp.int32, (n_rows, 1), 0)
    back = tok_row_i // SEL_BLOCK - j
    forced = (j == 0) | ((back >= 0) & (back < N_LOCAL))
    score = jnp.where(forced, FORCED_SCORE, jnp.where(back >= 0, imp, -1.0))
    return jnp.where(j < n_sel, score, -2.0)


def _kv_pool_partials(x, pw0, pw1):
    rows = x.shape[0]
    n = rows // CMP_STRIDE
    pool = (lax.broadcasted_iota(jnp.int32, (n, rows), 1) // CMP_STRIDE
            == lax.broadcasted_iota(jnp.int32, (n, rows), 0)).astype(f32)
    a = jnp.dot(pool, x * pw0, preferred_element_type=f32, precision=HI)
    b = jnp.dot(pool, x * pw1, preferred_element_type=f32, precision=HI)
    return a, b


def _compress_kv(pooled, wck, wcv, kc_ref, vct_ref):
    for g in range(B_KV_HEADS):
        kc_ref[g] = _mm(pooled[:, g * 64:(g + 1) * 64], wck).astype(bf16)
        vc = _mm(pooled[:, B_KV_WIDTH + g * 64:B_KV_WIDTH + (g + 1) * 64], wcv)
        vct_ref[g] = vc.T.astype(bf16)


def _nsa_prep_kernel(q_ref, kvc_ref, kvs_ref, kvw_ref, pw0_ref, pw1_ref,
                     qt_ref, ks_ref, vst_ref, kw_ref, vwt_ref, a_ref, b_ref):
    qt_ref[...] = (q_ref[...] * (ATTN_SCALE * LOG2E)).T.astype(bf16)
    tt = q_ref.shape[0]
    aug = _key_aug(pl.program_id(1) * tt + lax.broadcasted_iota(jnp.int32, (tt, 1), 0))
    for src, k_ref, vt_ref in ((kvs_ref, ks_ref, vst_ref), (kvw_ref, kw_ref, vwt_ref)):
        x = src[...]
        for g in range(B_KV_HEADS):
            k_ref[g] = jnp.concatenate([x[:, g * 64:(g + 1) * 64].astype(bf16), aug], axis=1)
        vt_ref[...] = x[:, B_KV_WIDTH:].T.astype(bf16)
    a, b = _kv_pool_partials(kvc_ref[...], pw0_ref[...], pw1_ref[...])
    a_ref[...] = a
    b_ref[...] = b


def _nsa_prep(u3, pw0, pw1, tt):
    B, T, _ = u3.shape
    n_ch = tt // CMP_STRIDE
    ublk = lambda w, off: pl.BlockSpec((None, tt, w), lambda b, t: (b, t, off // w))
    const = pl.BlockSpec((tt, 2 * B_KV_WIDTH), lambda b, t: (0, 0))
    return pl.pallas_call(
        _nsa_prep_kernel,
        out_shape=(jax.ShapeDtypeStruct((B, B_WIDTH, T), bf16),
                   jax.ShapeDtypeStruct((B, B_KV_HEADS, T, B_HEAD_DIM + K_AUG), bf16),
                   jax.ShapeDtypeStruct((B, B_KV_WIDTH, T), bf16),
                   jax.ShapeDtypeStruct((B, B_KV_HEADS, T, B_HEAD_DIM + K_AUG), bf16),
                   jax.ShapeDtypeStruct((B, B_KV_WIDTH, T), bf16),
                   jax.ShapeDtypeStruct((B, T // CMP_STRIDE, 2 * B_KV_WIDTH), f32),
                   jax.ShapeDtypeStruct((B, T // CMP_STRIDE, 2 * B_KV_WIDTH), f32)),
        grid=(B, T // tt),
        in_specs=[ublk(B_WIDTH, C_Q), ublk(2 * B_KV_WIDTH, C_KVC), ublk(2 * B_KV_WIDTH, C_KVS),
                  ublk(2 * B_KV_WIDTH, C_KVW), const, const],
        out_specs=(pl.BlockSpec((None, B_WIDTH, tt), lambda b, t: (b, 0, t)),
                   pl.BlockSpec((None, B_KV_HEADS, tt, B_HEAD_DIM + K_AUG), lambda b, t: (b, 0, t, 0)),
                   pl.BlockSpec((None, B_KV_WIDTH, tt), lambda b, t: (b, 0, t)),
                   pl.BlockSpec((None, B_KV_HEADS, tt, B_HEAD_DIM + K_AUG), lambda b, t: (b, 0, t, 0)),
                   pl.BlockSpec((None, B_KV_WIDTH, tt), lambda b, t: (b, 0, t)),
                   pl.BlockSpec((None, n_ch, 2 * B_KV_WIDTH), lambda b, t: (b, t, 0)),
                   pl.BlockSpec((None, n_ch, 2 * B_KV_WIDTH), lambda b, t: (b, t, 0))),
        compiler_params=_cparams(("parallel", "parallel")),
        name="nsa_prep",
    )(u3, u3, u3, u3, pw0, pw1)


def _rank_select(score, n_sel):
    sub = 8
    assert n_sel % sub == 0
    tiles = [score[r * sub:(r + 1) * sub] for r in range(n_sel // sub)]
    ranks = [jnp.zeros(t.shape, f32) for t in tiles]
    j_in_tile = lax.broadcasted_iota(jnp.int32, (sub, 1), 0)
    for i in range(n_sel):
        row = tiles[i // sub][i % sub:i % sub + 1]
        for r, t in enumerate(tiles):
            if r * sub > i:
                beats = jnp.where(row >= t, 1.0, 0.0)
            elif r * sub + sub - 1 <= i:
                beats = jnp.where(row > t, 1.0, 0.0)
            else:
                beats = jnp.where(row > t, 1.0, jnp.where(row == t, jnp.where(j_in_tile + r * sub > i, 1.0, 0.0), 0.0))
            ranks[r] = ranks[r] + beats
    rank = jnp.concatenate(ranks, axis=0)
    return jnp.where(rank < float(min(N_SELECT, n_sel)), jnp.where(score >= 0.0, 1.0, 0.0), 0.0)


def _nsa_prompt_kernel(qt_ref, ks_ref, vst_ref, kw_ref, vwt_ref, a_ref, b_ref, wck_ref, wcv_ref, gates_ref, gb_ref,
                       o_ref, kc_ref, vct_ref, keep_ref, cmp_ref, qaug_ref, m_ref, l_ref, acc_ref, ot_ref,
                       *, seq, tq):
    i = pl.program_id(1)
    n_ch = seq // CMP_STRIDE
    n_sel = seq // SEL_BLOCK
    lanes = B_GROUP * tq
    tk = 2 * SEL_BLOCK

    @pl.when(i == 0)
    def _():
        pooled = a_ref[...] + pltpu.roll(b_ref[...], n_ch - 1, 0)
        _compress_kv(pooled, wck_ref[...], wcv_ref[...], kc_ref, vct_ref)

    G = B_KV_HEADS
    gs = range(G)
    tok_i = i * tq + lax.broadcasted_iota(jnp.int32, (1, tq), 1)
    t_row = _tile_heads(tok_i.astype(f32))
    key_off = lax.broadcasted_iota(jnp.int32, (tk, 1), 0).astype(f32)
    slopes = [_slope_row(g, tq) * LOG2E for g in gs]

    def q_t(g):
        return jnp.concatenate([qt_ref[(g * B_GROUP + r) * 64:(g * B_GROUP + r + 1) * 64, :] for r in range(B_GROUP)],
                               axis=1)

    o_cmp, prob = _compressed_multi([kc_ref[g] for g in gs], [vct_ref[g] for g in gs], [q_t(g) for g in gs],
                                    t_row, slopes, n_ch)
    for g in gs:
        cmp_ref[g] = o_cmp[g]
    score = jnp.concatenate([_selection_scores(prob[g], tq, tok_i, n_sel, n_sel, n_ch) for g in gs], axis=1)
    keep = _rank_select(score, n_sel)
    for j in range(n_sel):
        keep_ref[j] = (keep[j:j + 1, :] - 1.0) * (-NEG)

    for g in gs:
        qaug_ref[g] = jnp.concatenate([q_t(g), _query_aug(slopes[g], t_row)], axis=0)
    for c in range(2 * G):
        m, l, acc = _attn_init(lanes)
        m_ref[c] = m
        l_ref[c] = l
        acc_ref[c] = acc

    def step(kt, with_window, masked, n_tiles=1):
        assert not (masked and n_tiles > 1)
        width = n_tiles * tk
        n_rows = width // SEL_BLOCK
        off = pl.multiple_of(kt * width, width)
        rows = [keep_ref[n_rows * kt + j] for j in range(n_rows)]
        if masked:
            dist = t_row - (key_off + (kt * tk).astype(f32))
            causal = jnp.where(dist >= 0.0, 0.0, NEG)
            band = jnp.where(dist <= float(WINDOW), causal, NEG)
        s, v, chains = [], [], []
        for g in gs:
            sg = jnp.dot(ks_ref[g, pl.ds(off, width), :], qaug_ref[g], preferred_element_type=f32)
            if masked:
                sg = sg + causal
            s.append(jnp.concatenate([sg[j * SEL_BLOCK:(j + 1) * SEL_BLOCK] + _tile_heads(rows[j][:, g * tq:(g + 1) * tq])
                                      for j in range(n_rows)], axis=0))
            v.append(vst_ref[g * 64:(g + 1) * 64, pl.ds(off, width)])
            chains.append(g)
        if with_window:
            for g in gs:
                sg = jnp.dot(kw_ref[g, pl.ds(off, width), :], qaug_ref[g], preferred_element_type=f32)
                s.append(sg + band if masked else sg)
                v.append(vwt_ref[g * 64:(g + 1) * 64, pl.ds(off, width)])
                chains.append(G + g)
        out = _attn_step_multi(s, [(m_ref[c], l_ref[c], acc_ref[c]) for c in chains], v)
        for c, (m, l, acc) in zip(chains, out):
            m_ref[c] = m
            l_ref[c] = l
            acc_ref[c] = acc

    assert tq == tk and WINDOW % tk == 0
    last = i
    lo = jnp.maximum(i - WINDOW // tk, 0)

    def plain_loop(with_window, n_tiles=1):
        def body(kt, carry):
            step(kt, with_window, False, n_tiles)
            return carry
        return body

    lax.fori_loop(0, lo // 2, plain_loop(False, 2), 0)

    @pl.when(lo % 2 == 1)
    def _():
        step(lo - 1, False, False)

    step(lo, True, True)
    lax.fori_loop(lo + 1, last, plain_loop(True), 0)

    @pl.when(last > lo)
    def _():
        step(last, True, True)

    gates_t = jax.nn.sigmoid(gates_ref[...]).T

    def gate_row(branch, g):
        base = branch * B_HEADS + g * B_GROUP
        return jnp.concatenate([gates_t[base + r:base + r + 1, :] for r in range(B_GROUP)], axis=1)

    o_sel = [_attn_finish((m_ref[g], l_ref[g], acc_ref[g])) for g in gs]
    o_win = [_attn_finish((m_ref[G + g], l_ref[G + g], acc_ref[G + g])) for g in gs]
    o_t = [gate_row(0, g) * cmp_ref[g] + gate_row(1, g) * o_sel[g] + gate_row(2, g) * o_win[g] for g in gs]
    for g in gs:
        for r in range(B_GROUP):
            ot_ref[(g * B_GROUP + r) * 64:(g * B_GROUP + r + 1) * 64, :] = o_t[g][:, r * tq:(r + 1) * tq]

    gb = gb_ref[...]
    o_ref[...] = ot_ref[...].T * (gb * jax.nn.sigmoid(gb))


def _nsa_prompt(u3, qt, ks, vst, kw, vwt, a, b, wck, wcv, tq):
    B, T, _ = u3.shape
    n_ch = T // CMP_STRIDE
    n_sel = T // SEL_BLOCK
    lanes = B_GROUP * tq
    per_b3 = lambda s1, s2: pl.BlockSpec((None, s1, s2), lambda b, i: (b, 0, 0))
    per_b4 = pl.BlockSpec((None, B_KV_HEADS, T, B_HEAD_DIM + K_AUG), lambda b, i: (b, 0, 0, 0))
    w_spec = pl.BlockSpec((B_HEAD_DIM, B_HEAD_DIM), lambda b, i: (0, 0))
    return pl.pallas_call(
        functools.partial(_nsa_prompt_kernel, seq=T, tq=tq),
        out_shape=jax.ShapeDtypeStruct((B, T, B_WIDTH), f32),
        grid=(B, T // tq),
        in_specs=[pl.BlockSpec((None, B_WIDTH, tq), lambda b, i: (b, 0, i)),
                  per_b4, per_b3(B_KV_WIDTH, T), per_b4, per_b3(B_KV_WIDTH, T),
                  per_b3(n_ch, 2 * B_KV_WIDTH), per_b3(n_ch, 2 * B_KV_WIDTH), w_spec, w_spec,
                  pl.BlockSpec((None, tq, 128), lambda b, i: (b, i, C_NG // 128)),
                  pl.BlockSpec((None, tq, B_WIDTH), lambda b, i: (b, i, C_GB // B_WIDTH))],
        out_specs=pl.BlockSpec((None, tq, B_WIDTH), lambda b, i: (b, i, 0)),
        scratch_shapes=[pltpu.VMEM((B_KV_HEADS, n_ch, B_HEAD_DIM), bf16),
                        pltpu.VMEM((B_KV_HEADS, B_HEAD_DIM, n_ch), bf16),
                        pltpu.VMEM((n_sel, 1, B_KV_HEADS * tq), f32),
                        pltpu.VMEM((B_KV_HEADS, B_HEAD_DIM, lanes), f32),
                        pltpu.VMEM((B_KV_HEADS, B_HEAD_DIM + K_AUG, lanes), bf16),
                        pltpu.VMEM((2 * B_KV_HEADS, 1, lanes), f32),
                        pltpu.VMEM((2 * B_KV_HEADS, 1, lanes), f32),
                        pltpu.VMEM((2 * B_KV_HEADS, B_HEAD_DIM, lanes), f32),
                        pltpu.VMEM((B_WIDTH, tq), f32)],
        compiler_params=_cparams(("parallel", "arbitrary")),
        name="nsa_prompt",
    )(qt, ks, vst, kw, vwt, a, b, wck, wcv, u3, u3)


def _softmax_rows_update(sc, m, l, acc, pv_fn):
    m_new = jnp.maximum(m, jnp.max(sc, axis=-1, keepdims=True))
    alpha = jnp.exp(m - m_new)
    p = jnp.where(sc > 0.5 * NEG, jnp.exp(sc - m_new), 0.0)
    return m_new, l * alpha + jnp.sum(p, axis=-1, keepdims=True), acc * alpha + pv_fn(p)


def _sample_rows(past, t_pad):
    R = B_GROUP * t_pad
    row = lax.broadcasted_iota(jnp.int32, (R, 1), 0)
    t_col = (past + row % t_pad).astype(f32)
    slope_col = []
    for g in range(B_KV_HEADS):
        sc_ = jnp.zeros((R, 1), f32)
        for r in range(B_GROUP):
            sc_ = jnp.where(row // t_pad == r, _alibi_slope(g * B_GROUP + r), sc_)
        slope_col.append(sc_)
    return t_col, slope_col


def _sample_queries(q_ref):
    q = q_ref[...] * ATTN_SCALE
    return [jnp.concatenate([q[:, (g * B_GROUP + r) * 64:(g * B_GROUP + r + 1) * 64] for r in range(B_GROUP)],
                            axis=0).astype(bf16) for g in range(B_KV_HEADS)]


def _ksl(g):
    return slice(g * B_HEAD_DIM, (g + 1) * B_HEAD_DIM)


def _vsl(g):
    return slice(B_KV_WIDTH + g * B_HEAD_DIM, B_KV_WIDTH + (g + 1) * B_HEAD_DIM)


def _nsa_sample_cmp_kernel(pt_ref, *refs, n_pages, pages_per_step, t_pad):
    P = pages_per_step
    cmp_pages = refs[:P]
    q_ref, pw0_ref, pw1_ref, pool_ref, wck_ref, wcv_ref, ocmp_ref, keep_ref, at_ref, bt_ref = refs[P:]
    s = pl.program_id(1)
    S = n_pages // P
    G = B_KV_HEADS
    gs = range(G)
    past = n_pages * PAGE_SIZE
    n_ch = past // CMP_STRIDE
    n_sel = past // SEL_BLOCK + 1
    n_j = keep_ref.shape[1]
    t_col, slope_col = _sample_rows(past, t_pad)
    ksl, vsl = _ksl, _vsl

    pw0, pw1 = pw0_ref[...], pw1_ref[...]
    xa = jnp.concatenate([(cmp_pages[k][...] * pw0).astype(bf16) for k in range(P)], axis=1)
    xb = jnp.concatenate([(cmp_pages[k][...] * pw1).astype(bf16) for k in range(P)], axis=1)
    n_step = P * PAGE_SIZE // CMP_STRIDE
    off = pl.multiple_of(s * n_step, n_step)
    at_ref[:, pl.ds(off, n_step)] = jnp.dot(xa, pool_ref[...], preferred_element_type=f32)
    bt_ref[:, pl.ds(off, n_step)] = jnp.dot(xb, pool_ref[...], preferred_element_type=f32)

    @pl.when(s == S - 1)
    def _():
        pooled = at_ref[...] + pltpu.roll(bt_ref[...], n_ch - 1, 1)
        qg = _sample_queries(q_ref)
        kct =[_mm_tn(wck_ref[...], pooled[ksl(g), :]).astype(bf16) for g in gs]
        vct = [_mm_tn(wcv_ref[...], pooled[vsl(g), :]).astype(bf16) for g in gs]
        blk_end = (lax.broadcasted_iota(jnp.int32, (1, n_ch), 1) * CMP_STRIDE + (CMP_BLOCK - 1)).astype(f32)
        dist = t_col - blk_end
        mask = dist >= 0.0
        sc = [jnp.dot(qg[g], kct[g], preferred_element_type=f32) for g in gs]
        sc = [jnp.where(mask, sc[g] - slope_col[g] * dist, NEG) for g in gs]
        mx = [jnp.max(sc[g], axis=-1, keepdims=True) for g in gs]
        pr = [jnp.where(mask, jnp.exp(sc[g] - mx[g]), 0.0) for g in gs]
        inv = [1.0 / jnp.maximum(jnp.sum(pr[g], axis=-1, keepdims=True), 1e-30) for g in gs]
        pr = [pr[g] * inv[g] for g in gs]
        for g in gs:
            ocmp_ref[g] = _mm_nt(pr[g], vct[g])
        psum = [pr[g][0:t_pad] for g in gs]
        for r in range(1, B_GROUP):
            psum = [psum[g] + pr[g][r * t_pad:(r + 1) * t_pad] for g in gs]
        psum = jnp.concatenate(psum, axis=0)
        n_i = lax.broadcasted_iota(jnp.int32, (n_ch, n_j), 0) * CMP_STRIDE
        j_i = lax.broadcasted_iota(jnp.int32, (n_ch, n_j), 1) * SEL_BLOCK
        overlap = jnp.maximum(jnp.minimum(n_i + CMP_BLOCK, j_i + SEL_BLOCK) - jnp.maximum(n_i, j_i), 0).astype(f32) \
            * (1.0 / CMP_BLOCK)
        imp = jnp.dot(psum, overlap, preferred_element_type=f32, precision=HI)
        j = lax.broadcasted_iota(jnp.int32, (1, n_j), 1)
        tok = past + lax.broadcasted_iota(jnp.int32, (G * t_pad, 1), 0) % t_pad
        back = tok // SEL_BLOCK - j
        forced = (j == 0) | ((back >= 0) & (back < N_LOCAL))
        score = jnp.where(forced, FORCED_SCORE, jnp.where(back >= 0, imp, -1.0))
        score = jnp.where(j < n_sel, score, -2.0)
        rank = jnp.zeros(score.shape, f32)
        for i in range(n_sel):
            col = score[:, i:i + 1]
            rank = rank + jnp.where(col > score, 1.0, jnp.where(col == score, jnp.where(j > i, 1.0, 0.0), 0.0))
        keep_ref[...] = jnp.where(rank < float(min(N_SELECT, n_sel)), jnp.where(score >= 0.0, 1.0, 0.0), 0.0)

def _nsa_sample_sel_kernel(pt_ref, need_ref, src_ref, *refs, n_pages, pages_per_step, t_pad):
    P = pages_per_step
    sel_pages = refs[:P]
    (win_ref, q_ref, news_ref, neww_ref, gates_ref, gb_ref, ocmp_ref, keep_ref,
     o_ref, qg_ref, m_ref, l_ref, acc_ref) = refs[P:]
    b = pl.program_id(0)
    s = pl.program_id(1)
    S = n_pages // P
    G = B_KV_HEADS
    gs = range(G)
    past = n_pages * PAGE_SIZE
    n_sel = past // SEL_BLOCK + 1
    n_j = keep_ref.shape[1]
    R = B_GROUP * t_pad
    t_col, slope_col = _sample_rows(past, t_pad)
    ksl, vsl = _ksl, _vsl

    @pl.when(s == 0)
    def _():
        qg = _sample_queries(q_ref)
        for g in gs:
            qg_ref[g] = qg[g]
            m_ref[g] = jnp.full((R, 1), NEG, f32)
            l_ref[g] = jnp.zeros((R, 1), f32)
            acc_ref[g] = jnp.zeros((R, B_HEAD_DIM), f32)

    @pl.when(need_ref[b, s] == 1)
    def _():
        ss = s
        n_blk = P * PAGE_SIZE // SEL_BLOCK
        n_key = P * PAGE_SIZE
        pick = (lax.broadcasted_iota(jnp.int32, (n_j, n_blk), 0)
                == lax.broadcasted_iota(jnp.int32, (n_j, n_blk), 1) + ss * n_blk).astype(bf16)
        keep_blk = jnp.dot(keep_ref[...].astype(bf16), pick, preferred_element_type=f32)
        expand = (lax.broadcasted_iota(jnp.int32, (n_blk, n_key), 0)
                  == lax.broadcasted_iota(jnp.int32, (n_blk, n_key), 1) // SEL_BLOCK).astype(bf16)
        bias = (jnp.dot(keep_blk.astype(bf16), expand, preferred_element_type=f32) - 1.0) * (-NEG)
        kpos = (ss * n_key + lax.broadcasted_iota(jnp.int32, (1, n_key), 1)).astype(f32)
        dist = t_col - kpos
        kt = [jnp.concatenate([sel_pages[k][ksl(g), :].astype(bf16) for k in range(P)], axis=1) for g in gs]
        vt = [jnp.concatenate([sel_pages[k][vsl(g), :].astype(bf16) for k in range(P)], axis=1) for g in gs]
        sc = [jnp.dot(qg_ref[g], kt[g], preferred_element_type=f32) for g in gs]
        sc = [sc[g] - slope_col[g] * dist + jnp.concatenate([bias[g * t_pad:(g + 1) * t_pad]] * B_GROUP, axis=0)
              for g in gs]
        m_new = [jnp.maximum(m_ref[g], jnp.max(sc[g], axis=-1, keepdims=True)) for g in gs]
        alpha = [jnp.exp(m_ref[g] - m_new[g]) for g in gs]
        pr = [jnp.where(sc[g] > 0.5 * NEG, jnp.exp(sc[g] - m_new[g]), 0.0) for g in gs]
        l_new = [l_ref[g] * alpha[g] + jnp.sum(pr[g], axis=-1, keepdims=True) for g in gs]
        pv = [_mm_nt(pr[g], vt[g]) for g in gs]
        for g in gs:
            acc_ref[g] = acc_ref[g] * alpha[g] + pv[g]
            m_ref[g] = m_new[g]
            l_ref[g] = l_new[g]

    @pl.when(s == S - 1)
    def _():
        xs = news_ref[...]
        xw = neww_ref[...]
        win = win_ref[...]
        n_win = win.shape[1]
        new_dist = t_col - (past + lax.broadcasted_iota(jnp.int32, (1, t_pad), 1)).astype(f32)
        win_dist = t_col - (past - n_win + lax.broadcasted_iota(jnp.int32, (1, n_win), 1)).astype(f32)
        keep_new = keep_ref[:, n_sel - 1:n_sel]
        gates = jax.nn.sigmoid(gates_ref[...])
        gb = gb_ref[...]
        silu_gb = gb * jax.nn.sigmoid(gb)
        qg = [qg_ref[g] for g in gs]
        sc = [_mm_nt(qg[g], xs[:, ksl(g)]) - slope_col[g] * new_dist for g in gs]
        sc = [jnp.where(new_dist >= 0.0, sc[g], NEG)
              + (jnp.concatenate([keep_new[g * t_pad:(g + 1) * t_pad]] * B_GROUP, axis=0) - 1.0) * (-NEG) for g in gs]
        sel = [_softmax_rows_update(sc[g], m_ref[g], l_ref[g], acc_ref[g], lambda p, g=g: _mm(p, xs[:, vsl(g)]))
               for g in gs]
        o_sel = [sel[g][2] * (1.0 / jnp.maximum(sel[g][1], 1e-30)) for g in gs]
        sw = [jnp.dot(qg[g], win[ksl(g), :].astype(bf16), preferred_element_type=f32) - slope_col[g] * win_dist
              for g in gs]
        sw = [jnp.where(win_dist <= float(WINDOW), sw[g], NEG) for g in gs]
        sn = [_mm_nt(qg[g], xw[:, ksl(g)]) - slope_col[g] * new_dist for g in gs]
        sn = [jnp.where(new_dist >= 0.0, jnp.where(new_dist <= float(WINDOW), sn[g], NEG), NEG) for g in gs]
        mw = [jnp.maximum(jnp.max(sw[g], axis=-1, keepdims=True), jnp.max(sn[g], axis=-1, keepdims=True)) for g in gs]
        pw = [jnp.where(sw[g] > 0.5 * NEG, jnp.exp(sw[g] - mw[g]), 0.0) for g in gs]
        pn = [jnp.where(sn[g] > 0.5 * NEG, jnp.exp(sn[g] - mw[g]), 0.0) for g in gs]
        lw = [jnp.sum(pw[g], axis=-1, keepdims=True) + jnp.sum(pn[g], axis=-1, keepdims=True) for g in gs]
        o_win = [(_mm_nt(pw[g], win[vsl(g), :]) + _mm(pn[g], xw[:, vsl(g)])) * (1.0 / jnp.maximum(lw[g], 1e-30))
                 for g in gs]
        for g in gs:
            def gate_col(branch):
                base = branch * B_HEADS + g * B_GROUP
                return jnp.concatenate([gates[:, base + r:base + r + 1] for r in range(B_GROUP)], axis=0)

            o = gate_col(0) * ocmp_ref[g] + gate_col(1) * o_sel[g] + gate_col(2) * o_win[g]
            for r in range(B_GROUP):
                cols = slice((g * B_GROUP + r) * 64, (g * B_GROUP + r + 1) * 64)
                o_ref[:, cols] = o[r * t_pad:(r + 1) * t_pad, :] * silu_gb[:, cols]


def _nsa_sample(u3, cache_cmp_t, cache_sel_t, cache_win_t, page_table, pw0_t, pw1_t, wck, wcv, *, t_new, pages_per_step):
    DB, t_pad, _ = u3.shape
    n_pages = page_table.shape[1]
    P = pages_per_step
    P2 = min(8, n_pages)
    assert t_new < CMP_STRIDE and t_new <= t_pad and n_pages % P == 0 and (P * PAGE_SIZE // CMP_STRIDE) % 128 == 0
    assert n_pages % P2 == 0
    S, S2 = n_pages // P, n_pages // P2
    past = n_pages * PAGE_SIZE
    n_ch = past // CMP_STRIDE
    n_sel = past // SEL_BLOCK + 1
    n_j = -(-n_sel // 128) * 128
    R = B_GROUP * t_pad
    G = B_KV_HEADS
    kvw = 2 * B_KV_WIDTH
    pool = (lax.broadcasted_iota(jnp.int32, (P * PAGE_SIZE, P * PAGE_SIZE // CMP_STRIDE), 0) // CMP_STRIDE
            == lax.broadcasted_iota(jnp.int32, (P * PAGE_SIZE, P * PAGE_SIZE // CMP_STRIDE), 1)).astype(bf16)
    page = lambda fn: pl.BlockSpec((None, kvw, PAGE_SIZE), fn)

    ublk = lambda w, off: pl.BlockSpec((None, t_pad, w), lambda b, s, pt: (b, 0, off // w))
    const = lambda s1, s2: pl.BlockSpec((s1, s2), lambda b, s, pt: (0, 0))
    o_cmp, keep = pl.pallas_call(
        functools.partial(_nsa_sample_cmp_kernel, n_pages=n_pages, pages_per_step=P, t_pad=t_pad),
        out_shape=(jax.ShapeDtypeStruct((DB, G, R, B_HEAD_DIM), f32), jax.ShapeDtypeStruct((DB, G * t_pad, n_j), f32)),
        grid_spec=pltpu.PrefetchScalarGridSpec(
            num_scalar_prefetch=1,
            grid=(DB, S),
            in_specs=[page(lambda b, s, pt, k=k: (pt[b, s * P + k], 0, 0)) for k in range(P)] + [
                ublk(B_WIDTH, C_Q), const(kvw, PAGE_SIZE), const(kvw, PAGE_SIZE), const(*pool.shape),
                const(B_HEAD_DIM, B_HEAD_DIM), const(B_HEAD_DIM, B_HEAD_DIM)],
            out_specs=(pl.BlockSpec((None, G, R, B_HEAD_DIM), lambda b, s, pt: (b, 0, 0, 0)),
                       pl.BlockSpec((None, G * t_pad, n_j), lambda b, s, pt: (b, 0, 0))),
            scratch_shapes=[pltpu.VMEM((kvw, n_ch), f32), pltpu.VMEM((kvw, n_ch), f32)]),
        compiler_params=_cparams(("parallel", "arbitrary")),
        name="nsa_sample_cmp",
    )(page_table, *([cache_cmp_t] * P), u3, pw0_t, pw1_t, pool, wck, wcv)

    blocks_per_step = P2 * PAGE_SIZE // SEL_BLOCK
    need = jnp.any(keep[:, :, :S2 * blocks_per_step].reshape(DB, G * t_pad, S2, blocks_per_step) > 0.5, axis=(1, 3))
    src = jnp.maximum(lax.cummax(jnp.where(need, jnp.arange(S2, dtype=jnp.int32), -1), axis=1), 0)

    ublk = lambda w, off: pl.BlockSpec((None, t_pad, w), lambda b, s, pt, nd, sr: (b, 0, off // w))
    return pl.pallas_call(
        functools.partial(_nsa_sample_sel_kernel, n_pages=n_pages, pages_per_step=P2, t_pad=t_pad),
        out_shape=jax.ShapeDtypeStruct((DB, t_pad, B_WIDTH), f32),
        grid_spec=pltpu.PrefetchScalarGridSpec(
            num_scalar_prefetch=3,
            grid=(DB, S2),
            in_specs=[page(lambda b, s, pt, nd, sr, k=k: (pt[b, sr[b, s] * P2 + k], 0, 0)) for k in range(P2)] + [
                pl.BlockSpec((None, kvw, cache_win_t.shape[2]), lambda b, s, pt, nd, sr: (b, 0, 0)),
                ublk(B_WIDTH, C_Q), ublk(kvw, C_KVS), ublk(kvw, C_KVW), ublk(128, C_NG), ublk(B_WIDTH, C_GB),
                pl.BlockSpec((None, G, R, B_HEAD_DIM), lambda b, s, pt, nd, sr: (b, 0, 0, 0)),
                pl.BlockSpec((None, G * t_pad, n_j), lambda b, s, pt, nd, sr: (b, 0, 0))],
            out_specs=pl.BlockSpec((None, t_pad, B_WIDTH), lambda b, s, pt, nd, sr: (b, 0, 0)),
            scratch_shapes=[pltpu.VMEM((G, R, B_HEAD_DIM), bf16),
                            pltpu.VMEM((G, R, 1), f32), pltpu.VMEM((G, R, 1), f32),
                            pltpu.VMEM((G, R, B_HEAD_DIM), f32)]),
        compiler_params=_cparams(("parallel", "arbitrary")),
        name="nsa_sample_sel",
    )(page_table, need.astype(jnp.int32), src, *([cache_sel_t] * P2), cache_win_t, u3, u3, u3, u3, u3, o_cmp, keep)


def _pos_weight_tiles(pos_k, pos_v, rows):
    def half(lo):
        t = jnp.concatenate([jnp.tile(pos_k[lo:lo + CMP_STRIDE], (1, B_KV_HEADS)),
                             jnp.tile(pos_v[lo:lo + CMP_STRIDE], (1, B_KV_HEADS))], axis=1)
        return jnp.tile(t, (rows // CMP_STRIDE, 1))
    return half(0), half(CMP_STRIDE)


def _a_cols(u_rows):
    return jnp.concatenate([u_rows[..., :4 * A_WIDTH], u_rows[..., C_LR:C_LR + DECAY_RANK + ICLR_RANK]], axis=-1)


def kernel(x_prompt, x_sample, cache_cmp_kv, cache_sel_kv, cache_win_kv, state_rwkv, state_shift, page_table, norm_in,
           w_in, mu_shift, w0, w_up, a0, a_up, k_k, k_a, r_k, gn_w, gn_b, cmp_pos_k, cmp_pos_v, w_cmp_k, w_cmp_v,
           w_pa, w_pb, w_o, norm_out):
    assert w_in.shape[0] == 1, "one layer"
    B, T, _ = x_prompt.shape
    DB, TS, _ = x_sample.shape
    H, N = A_HEADS, A_HEAD_DIM
    kvs = (2, B_KV_HEADS, B_HEAD_DIM)
    p = dict(mu_shift=mu_shift[0], w0=w0[0], w_up=w_up[0], a0=a0[0], a_up=a_up[0], k_k=k_k[0], k_a=k_a[0],
             r_k=r_k[0].reshape(-1), gn_w=gn_w[0], gn_b=gn_b[0])
    w_packed = _pack_w_in(w_in[0])
    wpa, wpb, wo = w_pa[0].astype(bf16), w_pb[0].astype(bf16), w_o[0].astype(bf16)

    xp = x_prompt.reshape(B * T, D_MODEL)
    up = _proj_in(xp, norm_in[0], w_packed)
    up3 = up.reshape(B, T, U_COLS)
    oa_p, s_p = _rwkv_wide(up3, jnp.zeros((B, 1, 4 * A_WIDTH), f32), jnp.zeros((B, 1, 128), f32),
                      jnp.zeros((B, H, N, N), f32), p, chunk=64, t_valid=64, rows=math.gcd(B, 4))
    tt = 512
    pw0, pw1 = _pos_weight_tiles(cmp_pos_k[0], cmp_pos_v[0], tt)
    qt, ks, vst, kw, vwt, pa, pb = _nsa_prep(up3, pw0, pw1, tt)
    ob_p = _nsa_prompt(up3, qt, ks, vst, kw, vwt, pa, pb, w_cmp_k[0], w_cmp_v[0], 128)
    y_p = _merge(xp, oa_p.reshape(B * T, A_WIDTH), ob_p.reshape(B * T, B_WIDTH), up, wpa, wpb, wo, norm_out)

    t_pad = 8
    xs = x_sample.reshape(DB * TS, D_MODEL)
    us = _proj_in(xs, norm_in[0], w_packed)
    us3 = us.reshape(DB, TS, U_COLS)
    us3p = jnp.pad(us3, ((0, 0), (0, t_pad - TS), (0, 0)))
    shift0 = state_shift[0]
    oa_s, s_s = _rwkv_wide(us3p, shift0[:, None, :4 * A_WIDTH], shift0[:, None, 4 * A_WIDTH:], state_rwkv[0], p,
                      chunk=t_pad, t_valid=TS, rows=2 if DB % 2 == 0 else 1)
    n_pool = cache_cmp_kv.shape[1]
    win = cache_win_kv[0]
    rows_last = lambda c, lead: jnp.transpose(c, (0, 2, 3, 4, 1)).reshape(lead, 2 * B_KV_WIDTH, c.shape[1])
    ob_s = _nsa_sample(us3p, rows_last(cache_cmp_kv[0], n_pool), rows_last(cache_sel_kv[0], n_pool),
                       rows_last(win, DB), page_table, pw0[:PAGE_SIZE].T, pw1[:PAGE_SIZE].T, w_cmp_k[0], w_cmp_v[0],
                       t_new=TS, pages_per_step=min(32, page_table.shape[1]))
    y_s = _merge(xs, oa_s[:, :TS].reshape(DB * TS, A_WIDTH), ob_s[:, :TS].reshape(DB * TS, B_WIDTH), us, wpa, wpb, wo,
                 norm_out)

    def kv_out(u3_, col, lead, t):
        return u3_[..., col:col + 2 * B_KV_WIDTH].reshape((1, lead, t) + kvs)

    wk = min(WINDOW, T)
    new_w_s = kv_out(us3, C_KVW, DB, TS)[0]
    s_win = jnp.concatenate([win, new_w_s], axis=1)[:, TS:][None]
    return (y_p.reshape(B, T, D_MODEL), y_s.reshape(DB, TS, D_MODEL),
            kv_out(up3, C_KVC, B, T), kv_out(up3, C_KVS, B, T), kv_out(up3[:, T - wk:], C_KVW, B, wk),
            s_p[None], _a_cols(up3[:, T - 1])[None],
            kv_out(us3, C_KVC, DB, TS), kv_out(us3, C_KVS, DB, TS), s_win,
            s_s[None], _a_cols(us3[:, TS - 1])[None])
```
